```python
import math
import jax, jax.numpy as jnp
from jax import lax
import numpy as np

D_MODEL = 1024
BATCH = 16
SEQ = 2048
DEPTH = 2

CTX_LEN = 256
GRID_W = 64
f32 = jnp.float32
RMS_EPS = 1e-6
ROPE_BASE = 10000.0
N_MOD = 9
D_FF = (11 * D_MODEL) // 4
MIX_W = D_MODEL
GROUP_W = MIX_W // 4

MLSTM_HEAD_DIM = 64
MLSTM_HEADS = GROUP_W // MLSTM_HEAD_DIM
MLSTM_CHUNK = 64
MLA_NOPE = 64
MLA_ROPE_DIM = 32
MLA_V = 64
MLA_HEADS = GROUP_W // MLA_V
MLA_Q_LORA = GROUP_W
MLA_KV_LORA = GROUP_W // 2
ATTN_BLK = 128
SWA_HEAD_DIM = 64
SWA_HEADS = GROUP_W // SWA_HEAD_DIM
SWA_KV_HEADS = SWA_HEADS // 2
SWA_WINDOW = 128
SWA_BLK = 128
SSD_HEAD_DIM = 64
SSD_HEADS = GROUP_W // SSD_HEAD_DIM
SSD_STATE = 64
SSD_GROUPS = 2
SSD_CONV = 5
SSD_CHUNK = 128
SSD_NORM_GROUP = GROUP_W // SSD_GROUPS
SSD_XBC = GROUP_W + 2 * SSD_GROUPS * SSD_STATE

IN_SEGMENTS = (
    ("m_q", GROUP_W), ("m_k", GROUP_W), ("m_v", GROUP_W), ("m_o", GROUP_W), ("m_gates", 4 * MLSTM_HEADS),
    ("a_q", MLA_Q_LORA), ("a_kv", MLA_KV_LORA), ("a_kr", MLA_ROPE_DIM),
    ("w_q", SWA_HEADS * SWA_HEAD_DIM), ("w_k", SWA_KV_HEADS * SWA_HEAD_DIM), ("w_v", SWA_KV_HEADS * SWA_HEAD_DIM),
    ("s_z", GROUP_W), ("s_xbc", SSD_XBC), ("s_dt", 2 * SSD_HEADS),
)
IN_COLS = sum(n for _, n in IN_SEGMENTS)

kernel_name = "hybrid_parallel_heads_diffusion_block"


def rms_norm(x, g, eps=RMS_EPS):
    xf = x.astype(f32)
    y = xf * lax.rsqrt(jnp.mean(xf * xf, axis=-1, keepdims=True) + eps)
    return (y * g.astype(f32)).astype(x.dtype)


def modulate(x, shift, scale):
    return x * (1 + scale) + shift


def swiglu(x, wi, wo):
    g, u = jnp.split(x @ wi, 2, axis=-1)
    return (jax.nn.silu(g) * u) @ wo


def split_cols(u):
    parts, o = {}, 0
    for name, n in IN_SEGMENTS:
        parts[name] = u[..., o:o + n]
        o += n
    return parts


def axial_rope(rows, rot_dim):
    row = jnp.repeat(jnp.arange(rows), GRID_W).astype(f32)
    col = jnp.tile(jnp.arange(GRID_W), rows).astype(f32)
    nf = rot_dim // 4
    inv = ROPE_BASE ** (-jnp.arange(nf, dtype=f32) / nf)
    ar, ac = row[:, None] * inv, col[:, None] * inv
    ang = jnp.concatenate([ar, ar, ac, ac], axis=-1)
    return jnp.cos(ang)[:, None, :], jnp.sin(ang)[:, None, :]


def apply_rope(x, cos, sin):
    r = x.shape[-1]
    x4 = x.reshape(x.shape[:-1] + (2, 2, r // 4))
    rot = jnp.stack([-x4[..., 1, :], x4[..., 0, :]], axis=-2).reshape(x.shape)
    return (x * cos + rot * sin).astype(x.dtype)


def to_chunks(a, L):
    b, h, t = a.shape[:3]
    return jnp.moveaxis(a.reshape((b, h, t // L, L) + a.shape[3:]), 2, 0)


def from_chunks(a):
    a = jnp.moveaxis(a, 0, 2)
    return a.reshape(a.shape[:2] + (a.shape[2] * a.shape[3],) + a.shape[4:])


def bidirectional(scan_fn, ctx_f, lat_f, ctx_b, lat_b, init):
    flip = lambda t: tuple(jnp.flip(a, axis=2) for a in t)
    h_cf, s_f = scan_fn(*ctx_f, init)
    h_lf, _ = scan_fn(*lat_f, s_f)
    h_cb, s_b = scan_fn(*flip(ctx_b), init)
    h_lb, _ = scan_fn(*flip(lat_b), s_b)
    return h_cf + jnp.flip(h_cb, axis=2), h_lf + jnp.flip(h_lb, axis=2)


def mlstm_scan(q, k, v, logi, logf, state):
    L = MLSTM_CHUNK
    tril = jnp.tril(jnp.ones((L, L), bool))

    def body(carry, inp):
        C, n, m = carry
        qc, kc, vc, ic, fc = inp
        b = jnp.cumsum(fc, axis=-1)
        D = jnp.where(tril, b[..., :, None] - b[..., None, :] + ic[..., None, :], -jnp.inf)
        m_inter = b + m[..., None]
        m_t = jnp.maximum(m_inter, jnp.max(D, axis=-1))
        w_inter = jnp.exp(m_inter - m_t)
        S = jnp.einsum('bhtd,bhsd->bhts', qc, kc) * jnp.exp(D - m_t[..., None])
        num = w_inter[..., None] * jnp.einsum('bhtd,bhde->bhte', qc, C) + jnp.einsum('bhts,bhse->bhte', S, vc)
        den = w_inter * jnp.einsum('bhtd,bhd->bht', qc, n) + jnp.sum(S, axis=-1)
        h = num / jnp.maximum(jnp.abs(den), jnp.exp(-m_t))[..., None]
        bL = b[..., -1]
        g = bL[..., None] - b + ic
        m_new = jnp.maximum(bL + m, jnp.max(g, axis=-1))
        a = jnp.exp(bL + m - m_new)
        w = jnp.exp(g - m_new[..., None])
        C_new = a[..., None, None] * C + jnp.einsum('bhs,bhsd,bhse->bhde', w, kc, vc)
        n_new = a[..., None] * n + jnp.einsum('bhs,bhsd->bhd', w, kc)
        return (C_new, n_new, m_new), h

    state, h = lax.scan(body, state, tuple(to_chunks(a, L) for a in (q, k, v, logi, logf)))
    return from_chunks(h), state


def mlstm_mixer(p_lat, p_ctx, gate_b, out_norm, need_ctx):
    H, dh = MLSTM_HEADS, MLSTM_HEAD_DIM

    def prep(p):
        b, t = p["m_q"].shape[:2]
        heads = lambda a: a.reshape(b, t, H, dh).transpose(0, 2, 1, 3).astype(f32)
        q, k, v = heads(p["m_q"]), heads(p["m_k"]) * dh ** -0.5, heads(p["m_v"])
        g = (p["m_gates"] + gate_b).astype(f32).reshape(b, t, 4, H).transpose(2, 0, 3, 1)
        return (q, k, v, g[0], jax.nn.log_sigmoid(g[1])), (q, k, v, g[2], jax.nn.log_sigmoid(g[3]))

    cf, cb = prep(p_ctx)
    lf, lb = prep(p_lat)
    b = p_lat["m_q"].shape[0]
    init = (jnp.zeros((b, H, dh, dh), f32), jnp.zeros((b, H, dh), f32), jnp.zeros((b, H), f32))
    h_c, h_l = bidirectional(mlstm_scan, cf, lf, cb, lb, init)

    def finish(h, p):
        bb, hh, t, d = h.shape
        hn = rms_norm(h.transpose(0, 2, 1, 3), out_norm.reshape(hh, d)).reshape(bb, t, hh * d)
        return (jax.nn.sigmoid(p["m_o"].astype(f32)) * hn).astype(p["m_o"].dtype)

    return finish(h_l, p_lat), (finish(h_c, p_ctx) if need_ctx else None)


def block_attention(q, k, v):
    b, t, h, dq = q.shape
    nb = t // ATTN_BLK
    scale = dq ** -0.5
    qb = q.reshape(b, nb, ATTN_BLK, h, dq).transpose(1, 0, 2, 3, 4)

    def one(qblk):
        s = jnp.einsum('bqhd,bkhd->bhqk', qblk, k).astype(f32) * scale
        p = jax.nn.softmax(s, axis=-1).astype(v.dtype)
        return jnp.einsum('bhqk,bkhd->bqhd', p, v)

    o = lax.map(one, qb)
    return o.transpose(1, 0, 2, 3, 4).reshape(b, t, h * v.shape[-1])


def mla_mixer(p_lat, p_ctx, q_norm, kv_norm, wq_b, wkv_b, q_gain, k_gain, rope, need_ctx):
    H = MLA_HEADS

    def proj(p, rope):
        b, t = p["a_q"].shape[:2]
        qh = (rms_norm(p["a_q"], q_norm) @ wq_b).reshape(b, t, H, MLA_NOPE + MLA_ROPE_DIM)
        q_nope = rms_norm(qh[..., :MLA_NOPE], q_gain[:MLA_NOPE])
        q_rope = rms_norm(qh[..., MLA_NOPE:], q_gain[MLA_NOPE:])
        kv = (rms_norm(p["a_kv"], kv_norm) @ wkv_b).reshape(b, t, H, MLA_NOPE + MLA_V)
        k_nope = rms_norm(kv[..., :MLA_NOPE], k_gain[:MLA_NOPE])
        v = kv[..., MLA_NOPE:]
        k_rope = rms_norm(p["a_kr"], k_gain[MLA_NOPE:])[:, :, None, :]
        if rope is not None:
            q_rope = apply_rope(q_rope, *rope)
            k_rope = apply_rope(k_rope, *rope)
        q = jnp.concatenate([q_nope, q_rope], axis=-1)
        k = jnp.concatenate([k_nope, jnp.broadcast_to(k_rope, (b, t, H, MLA_ROPE_DIM))], axis=-1)
        return q, k, v

    ql, kl, vl = proj(p_lat, rope)
    qc, kc, vc = proj(p_ctx, None)
    out_l = block_attention(ql, jnp.concatenate([kc, kl], axis=1), jnp.concatenate([vc, vl], axis=1))
    out_c = block_attention(qc, kc, vc) if need_ctx else None
    return out_l, out_c


def swa_mixer(p_lat, p_ctx, q_gain, k_gain, sink, rope, need_ctx):
    H, KV, dh = SWA_HEADS, SWA_KV_HEADS, SWA_HEAD_DIM
    G = H // KV
    scale = dh ** -0.5

    def proj(p, rope):
        b, t = p["w_q"].shape[:2]
        q = rms_norm(p["w_q"].reshape(b, t, H, dh), q_gain)
        k = rms_norm(p["w_k"].reshape(b, t, KV, dh), k_gain)
        v = p["w_v"].reshape(b, t, KV, dh)
        if rope is not None:
            q, k = apply_rope(q, *rope), apply_rope(k, *rope)
        return q, k, v

    ql, kl, vl = proj(p_lat, rope)
    qc, kc, vc = proj(p_ctx, None)
    b, t = ql.shape[:2]
    nb = t // SWA_BLK
    qb = ql.reshape(b, nb, SWA_BLK, KV, G, dh)

    def band(a):
        ap = jnp.pad(a, ((0, 0), (SWA_BLK, SWA_BLK), (0, 0), (0, 0))).reshape(b, nb + 2, SWA_BLK, KV, dh)
        return jnp.concatenate([ap[:, :-2], ap[:, 1:-1], ap[:, 2:]], axis=2)

    kb, vb = band(kl), band(vl)
    s_loc = jnp.einsum('bnqkgd,bnskd->bkgnqs', qb, kb).astype(f32) * scale
    blk = jnp.arange(nb)[:, None, None]
    qpos = blk * SWA_BLK + jnp.arange(SWA_BLK)[None, :, None]
    kpos = (blk - 1) * SWA_BLK + jnp.arange(3 * SWA_BLK)[None, None, :]
    valid = (kpos >= 0) & (kpos < t) & (jnp.abs(qpos - kpos) <= SWA_WINDOW)
    s_loc = jnp.where(valid, s_loc, -1e30)
    s_ctx = jnp.einsum('bnqkgd,bckd->bkgnqc', qb, kc).astype(f32) * scale
    s_sink = jnp.broadcast_to(sink.reshape(1, KV, G, 1, 1, 1).astype(f32), s_loc.shape[:-1] + (1,))
    p = jax.nn.softmax(jnp.concatenate([s_loc, s_ctx, s_sink], axis=-1), axis=-1).astype(vl.dtype)
    nl, ncx = 3 * SWA_BLK, kc.shape[1]
    o = (jnp.einsum('bkgnqs,bnskd->bnqkgd', p[..., :nl], vb)
         + jnp.einsum('bkgnqc,bckd->bnqkgd', p[..., nl:nl + ncx], vc))
    out_l = o.reshape(b, t, H * dh)
    out_c = None
    if need_ctx:
        cl = qc.shape[1]
        qg = qc.reshape(b, cl, KV, G, dh)
        s = jnp.einsum('bqkgd,bckd->bkgqc', qg, kc).astype(f32) * scale
        ss = jnp.broadcast_to(sink.reshape(1, KV, G, 1, 1).astype(f32), s.shape[:-1] + (1,))
        pc = jax.nn.softmax(jnp.concatenate([s, ss], axis=-1), axis=-1)[..., :-1].astype(vc.dtype)
        out_c = jnp.einsum('bkgqc,bckd->bqkgd', pc, vc).reshape(b, cl, H * dh)
    return out_l, out_c


def depthwise_conv(x, w, bias):
    ch = x.shape[-1]
    y = lax.conv_general_dilated(x, w[:, None, :].astype(x.dtype), window_strides=(1,),
                                 padding=[(SSD_CONV // 2, SSD_CONV // 2)],
                                 dimension_numbers=("NWC", "WIO", "NWC"), feature_group_count=ch)
    return y + bias


def ssd_scan(x, dt, a, Bm, Cm, state):
    L = SSD_CHUNK
    tril = jnp.tril(jnp.ones((L, L), bool))

    def body(S, inp):
        xc, dtc, ac, Bc, Cc = inp
        cum = jnp.cumsum(ac, axis=-1)
        seg = jnp.exp(jnp.where(tril, cum[..., :, None] - cum[..., None, :], -jnp.inf))
        w = jnp.einsum('bhtn,bhsn->bhts', Cc, Bc) * seg * dtc[..., None, :]
        y = jnp.einsum('bhts,bhsp->bhtp', w, xc) + jnp.exp(cum)[..., None] * jnp.einsum('bhtn,bhpn->bhtp', Cc, S)
        dec = jnp.exp(cum[..., -1:] - cum) * dtc
        S_new = jnp.exp(cum[..., -1])[..., None, None] * S + jnp.einsum('bhs,bhsp,bhsn->bhpn', dec, xc, Bc)
        return S_new, y

    state, y = lax.scan(body, state, tuple(to_chunks(t_, L) for t_ in (x, dt, a, Bm, Cm)))
    return from_chunks(y), state


def ssd_mixer(p_lat, p_ctx, conv_w, conv_b, dt_bias, a_log, d_skip, norm_g, need_ctx):
    H, P, N, G = SSD_HEADS, SSD_HEAD_DIM, SSD_STATE, SSD_GROUPS
    A = -jnp.exp(a_log.astype(f32)).reshape(2, H)

    def prep(p):
        b, t = p["s_xbc"].shape[:2]
        xbc = jax.nn.silu(depthwise_conv(p["s_xbc"], conv_w, conv_b)).astype(f32)
        xs, Bm, Cm = jnp.split(xbc, [GROUP_W, GROUP_W + G * N], axis=-1)
        xh = xs.reshape(b, t, H, P)
        rep = lambda m: jnp.repeat(m.reshape(b, t, G, N), H // G, axis=2).transpose(0, 2, 1, 3)
        Bh, Ch = rep(Bm), rep(Cm)
        dt = jax.nn.softplus((p["s_dt"] + dt_bias).astype(f32)).reshape(b, t, 2, H).transpose(2, 0, 3, 1)
        xt = xh.transpose(0, 2, 1, 3)
        fwd = (xt, dt[0], dt[0] * A[0][None, :, None], Bh, Ch)
        bwd = (xt, dt[1], dt[1] * A[1][None, :, None], Bh, Ch)
        return fwd, bwd, xh

    cf, cb, xc = prep(p_ctx)
    lf, lb, xl = prep(p_lat)
    b = xl.shape[0]
    init = jnp.zeros((b, H, P, N), f32)
    y_c, y_l = bidirectional(ssd_scan, cf, lf, cb, lb, init)

    def finish(y, xh, p):
        bb, t = xh.shape[:2]
        yy = (y.transpose(0, 2, 1, 3) + d_skip[:, None] * xh).reshape(bb, t, GROUP_W)
        g = (yy * jax.nn.silu(p["s_z"].astype(f32))).reshape(bb, t, GROUP_W // SSD_NORM_GROUP, SSD_NORM_GROUP)
        out = rms_norm(g, norm_g.reshape(GROUP_W // SSD_NORM_GROUP, SSD_NORM_GROUP)).reshape(bb, t, GROUP_W)
        return out.astype(p["s_z"].dtype)

    return finish(y_l, xl, p_lat), (finish(y_c, xc, p_ctx) if need_ctx else None)


def setup_inputs(seed: int = 0) -> dict:
    key = jax.random.key(seed)
    ks = iter(jax.random.split(key, 64))
    L = DEPTH

    def nrm(shape, scale=1.0):
        return jax.random.normal(next(ks), shape, f32) * scale

    def gain(shape):
        return 1.0 + nrm(shape, 0.05)

    fb = jnp.linspace(3.0, 6.0, MLSTM_HEADS, dtype=f32)
    mlstm_gate_b = jnp.concatenate([nrm((L, MLSTM_HEADS), 0.1), fb + nrm((L, MLSTM_HEADS), 0.1),
                                    nrm((L, MLSTM_HEADS), 0.1), fb + nrm((L, MLSTM_HEADS), 0.1)], axis=-1)
    dt0 = jnp.exp(jax.random.uniform(next(ks), (L, 2 * SSD_HEADS), f32, math.log(1e-3), math.log(1e-1)))
    ssd_dt_bias = dt0 + jnp.log(-jnp.expm1(-dt0))
    ssd_a_log = jnp.log(jax.random.uniform(next(ks), (L, 2 * SSD_HEADS), f32, 1.0, 16.0))
    qk_mla = MLA_NOPE + MLA_ROPE_DIM
    return {
        "x": nrm((BATCH, SEQ, D_MODEL)),
        "c": nrm((BATCH, D_MODEL)),
        "ctx": nrm((BATCH, CTX_LEN, D_MODEL)),
        "c_ctx": nrm((D_MODEL,)),
        "w_mod": nrm((L, D_MODEL, N_MOD * D_MODEL), 0.02),
        "b_mod": nrm((L, N_MOD * D_MODEL), 0.02),
        "ffn1_norm": gain((L, D_MODEL)),
        "ffn1_wi": nrm((L, D_MODEL, 2 * D_FF), D_MODEL ** -0.5),
        "ffn1_wo": nrm((L, D_FF, D_MODEL), D_FF ** -0.5),
        "mix_norm": gain((L, D_MODEL)),
        "w_in": nrm((L, D_MODEL, IN_COLS), D_MODEL ** -0.5),
        "w_out": nrm((L, MIX_W, D_MODEL), MIX_W ** -0.5),
        "mlstm_gate_b": mlstm_gate_b,
        "mlstm_out_norm": gain((L, GROUP_W)),
        "mla_q_norm": gain((L, MLA_Q_LORA)),
        "mla_kv_norm": gain((L, MLA_KV_LORA)),
        "mla_wq_b": nrm((L, MLA_Q_LORA, MLA_HEADS * qk_mla), MLA_Q_LORA ** -0.5),
        "mla_wkv_b": nrm((L, MLA_KV_LORA, MLA_HEADS * (MLA_NOPE + MLA_V)), MLA_KV_LORA ** -0.5),
        "mla_q_gain": gain((L, qk_mla)),
        "mla_k_gain": gain((L, qk_mla)),
        "swa_q_gain": gain((L, SWA_HEAD_DIM)),
        "swa_k_gain": gain((L, SWA_HEAD_DIM)),
        "swa_sink": nrm((L, SWA_HEADS), 0.5),
        "ssd_conv_w": nrm((L, SSD_CONV, SSD_XBC), SSD_CONV ** -0.5),
        "ssd_conv_b": nrm((L, SSD_XBC), 0.02),
        "ssd_dt_bias": ssd_dt_bias,
        "ssd_a_log": ssd_a_log,
        "ssd_d": 1.0 + nrm((L, SSD_HEADS), 0.1),
        "ssd_norm": gain((L, GROUP_W)),
        "ffn2_norm": gain((L, D_MODEL)),
        "ffn2_wi": nrm((L, D_MODEL, 2 * D_FF), D_MODEL ** -0.5),
        "ffn2_wo": nrm((L, D_FF, D_MODEL), D_FF ** -0.5),
    }


def reference(x, c, ctx, c_ctx, w_mod, b_mod, ffn1_norm, ffn1_wi, ffn1_wo, mix_norm, w_in, w_out,
              mlstm_gate_b, mlstm_out_norm, mla_q_norm, mla_kv_norm, mla_wq_b, mla_wkv_b, mla_q_gain,
              mla_k_gain, swa_q_gain, swa_k_gain, swa_sink, ssd_conv_w, ssd_conv_b, ssd_dt_bias, ssd_a_log,
              ssd_d, ssd_norm, ffn2_norm, ffn2_wi, ffn2_wo):
    b, t, d = x.shape
    rows = t // GRID_W
    rope_mla = axial_rope(rows, MLA_ROPE_DIM)
    rope_swa = axial_rope(rows, SWA_HEAD_DIM)
    sc, scc = jax.nn.silu(c), jax.nn.silu(c_ctx)
    h, hc = x, ctx
    for l in range(DEPTH):
        need_ctx = l < DEPTH - 1
        mod = (sc @ w_mod[l] + b_mod[l]).reshape(b, N_MOD, 1, d)
        modc = (scc @ w_mod[l] + b_mod[l]).reshape(1, N_MOD, 1, d)
        h = h + 0.5 * mod[:, 2] * swiglu(modulate(rms_norm(h, ffn1_norm[l]), mod[:, 0], mod[:, 1]),
                                         ffn1_wi[l], ffn1_wo[l])
        hc = hc + 0.5 * modc[:, 2] * swiglu(modulate(rms_norm(hc, ffn1_norm[l]), modc[:, 0], modc[:, 1]),
                                            ffn1_wi[l], ffn1_wo[l])
        u = split_cols(modulate(rms_norm(h, mix_norm[l]), mod[:, 3], mod[:, 4]) @ w_in[l])
        uc = split_cols(modulate(rms_norm(hc, mix_norm[l]), modc[:, 3], modc[:, 4]) @ w_in[l])
        a_l, a_c = mlstm_mixer(u, uc, mlstm_gate_b[l], mlstm_out_norm[l], need_ctx)
        m_l, m_c = mla_mixer(u, uc, mla_q_norm[l], mla_kv_norm[l], mla_wq_b[l], mla_wkv_b[l],
                             mla_q_gain[l], mla_k_gain[l], rope_mla, need_ctx)
        w_l, w_c = swa_mixer(u, uc, swa_q_gain[l], swa_k_gain[l], swa_sink[l], rope_swa, need_ctx)
        s_l, s_c = ssd_mixer(u, uc, ssd_conv_w[l], ssd_conv_b[l], ssd_dt_bias[l], ssd_a_log[l],
                             ssd_d[l], ssd_norm[l], need_ctx)
        h = h + mod[:, 5] * (jnp.concatenate([a_l, m_l, w_l, s_l], axis=-1) @ w_out[l])
        if need_ctx:
            hc = hc + modc[:, 5] * (jnp.concatenate([a_c, m_c, w_c, s_c], axis=-1) @ w_out[l])
            hc = hc + 0.5 * modc[:, 8] * swiglu(modulate(rms_norm(hc, ffn2_norm[l]), modc[:, 6], modc[:, 7]),
                                                ffn2_wi[l], ffn2_wo[l])
        h = h + 0.5 * mod[:, 8] * swiglu(modulate(rms_norm(h, ffn2_norm[l]), mod[:, 6], mod[:, 7]),
                                         ffn2_wi[l], ffn2_wo[l])
    return h
```

```python
import functools
import math

import jax
import jax.numpy as jnp
from jax import lax
from jax.experimental import pallas as pl
from jax.experimental.pallas import tpu as pltpu

f32 = jnp.float32
bf16 = jnp.bfloat16

RMS_EPS = 1e-6
ROPE_BASE = 10000.0
GRID_W = 64
N_MOD = 9
HEADS = 4
HEAD_DIM = 64
GROUP_W = HEADS * HEAD_DIM
MLA_NOPE = 64
MLA_ROPE = 32
MLA_QK = MLA_NOPE + MLA_ROPE
SWA_KV_HEADS = 2
SWA_WINDOW = 128
SWA_BLK = 128
SSD_STATE = 64
SSD_GROUPS = 2
SSD_CONV = 5
SSD_XBC = GROUP_W + 2 * SSD_GROUPS * SSD_STATE
SSD_NORM_GROUP = 128

LANES = 128
ROW_TILE = 512
FF_CHUNK = 256
SCAN_CHUNK = 128
MLA_QBLK = 256
NEG = -1e30
VMEM_LIMIT = 56 * 1024 * 1024

P_QKVO = 0
P_MG = 1024
P_MLA = 1152
P_SWA = 1664
P_Z = 2176
P_XBC = 2432
P_DT = 2944
P_COLS = 3072


def _dot(a, b):
    return jnp.dot(a, b, preferred_element_type=f32)


def _dot_nt(a, b):
    return lax.dot_general(a, b, (((1,), (1,)), ((), ())), preferred_element_type=f32)


def _dot_tn(a, b):
    return lax.dot_general(a, b, (((0,), (0,)), ((), ())), preferred_element_type=f32)


def _sigmoid(x):
    return 1.0 / (1.0 + jnp.exp(-x))


def _silu(x):
    return x * _sigmoid(x)


def _softplus(x):
    return jnp.maximum(x, 0.0) + jnp.log(1.0 + jnp.exp(-jnp.abs(x)))


def _log_sigmoid(x):
    return -_softplus(-x)


def _split3(x):
    hi = x.astype(bf16)
    r = x - hi.astype(f32)
    mid = r.astype(bf16)
    lo = (r - mid.astype(f32)).astype(bf16)
    return hi, mid, lo


def _tri_dot(tri, x):
    hi, mid, lo = _split3(x)
    return _dot(tri, hi) + _dot(tri, mid) + _dot(tri, lo)


def _dot_tri(x, tri):
    hi, mid, lo = _split3(x)
    return _dot(hi, tri) + _dot(mid, tri) + _dot(lo, tri)


def _group_rsqrt(x, gsz, n_real=None):
    n = x.shape[-1]
    lane = lax.broadcasted_iota(jnp.int32, x.shape, 1)
    sq = x * x
    out = jnp.zeros_like(x)
    for g in range(n // gsz):
        msk = (lane >= g * gsz) & (lane < (g + 1) * gsz)
        s = jnp.sum(jnp.where(msk, sq, 0.0), axis=1, keepdims=True) * (1.0 / (n_real or gsz))
        out = jnp.where(msk, lax.rsqrt(s + RMS_EPS), out)
    return out


def _row_rsqrt(x, n_real):
    return lax.rsqrt(jnp.sum(x * x, axis=1, keepdims=True) * (1.0 / n_real) + RMS_EPS)


def _rope(x, cos, sin, quarter):
    lane = lax.broadcasted_iota(jnp.int32, x.shape, 1)
    first = (lane % (2 * quarter)) < quarter
    up = pltpu.roll(x, LANES - quarter, 1)
    down = pltpu.roll(x, quarter, 1)
    rot = jnp.where(first, -up, down)
    return x * cos + rot * sin


def _resident(shape):
    nd = len(shape)
    return pl.BlockSpec(shape, lambda *_: (0,) * nd, pipeline_mode=pl.Buffered(1))


def _params(sem):
    return pltpu.CompilerParams(dimension_semantics=sem, vmem_limit_bytes=VMEM_LIMIT)


def _mod_body(c_ref, w_ref, b_ref, o_ref):
    s = _silu(c_ref[...])
    o_ref[...] = jnp.dot(s, w_ref[...], preferred_element_type=f32,
                         precision=lax.Precision.HIGHEST) + b_ref[...]


def _modulation(cc, w, b):
    m, d = cc.shape
    n = w.shape[1]
    tn = 1024
    return pl.pallas_call(
        _mod_body,
        grid=(n // tn,),
        in_specs=[pl.BlockSpec((m, d), lambda j: (0, 0)),
                  pl.BlockSpec((d, tn), lambda j: (0, j)),
                  pl.BlockSpec((1, tn), lambda j: (0, j))],
        out_specs=pl.BlockSpec((m, tn), lambda j: (0, j)),
        out_shape=jax.ShapeDtypeStruct((m, n), f32),
        compiler_params=_params(("arbitrary",)),
        name="modulation",
    )(cc, w, b.reshape(1, n))


def _norm_mod(x, gain, shift, scale):
    xn = x * _row_rsqrt(x, x.shape[1]) * gain
    return (xn * (1.0 + scale) + shift).astype(bf16)


def _ffn_body(x_ref, mod_ref, g_ref, wig_ref, wiu_ref, wo_ref, o_ref, *, mi):
    x = x_ref[...]
    xb = _norm_mod(x, g_ref[...], mod_ref[mi:mi + 1, :], mod_ref[mi + 1:mi + 2, :])
    acc = jnp.zeros(x.shape, f32)
    for c in range(wig_ref.shape[1] // FF_CHUNK):
        sl = slice(c * FF_CHUNK, (c + 1) * FF_CHUNK)
        g = _dot(xb, wig_ref[:, sl])
        u = _dot(xb, wiu_ref[:, sl])
        acc = acc + _dot((_silu(g) * u).astype(bf16), wo_ref[sl, :])
    o_ref[...] = x + (0.5 * mod_ref[mi + 2:mi + 3, :]) * acc


def _mod_spec(d, mod_base, tiles_per_mod):
    if tiles_per_mod is None:
        return pl.BlockSpec((None, N_MOD, d), lambda i: (mod_base, 0, 0))
    return pl.BlockSpec((None, N_MOD, d), lambda i: (mod_base + i // tiles_per_mod, 0, 0))


def _ffn(x, mod, gain, wig, wiu, wo, mi, mod_base, tiles_per_mod):
    rows, d = x.shape
    dff = wig.shape[1]
    return pl.pallas_call(
        functools.partial(_ffn_body, mi=mi),
        grid=(rows // ROW_TILE,),
        in_specs=[pl.BlockSpec((ROW_TILE, d), lambda i: (i, 0)),
                  _mod_spec(d, mod_base, tiles_per_mod),
                  _resident((1, d)), _resident((d, dff)), _resident((d, dff)), _resident((dff, d))],
        out_specs=pl.BlockSpec((ROW_TILE, d), lambda i: (i, 0)),
        out_shape=jax.ShapeDtypeStruct((rows, d), f32),
        compiler_params=_params(("parallel",)),
        name="ffn",
    )(x, mod, gain, wig, wiu, wo)


def _inproj_body(x_ref, mod_ref, g_ref, w_ref, qkvo_ref, mg_ref, mla_ref, swa_ref, z_ref, xbc_ref, dt_ref):
    xb = _norm_mod(x_ref[...], g_ref[...], mod_ref[3:4, :], mod_ref[4:5, :])
    qkvo_ref[...] = _dot(xb, w_ref[:, P_QKVO:P_MG]).astype(bf16)
    mg_ref[...] = _dot(xb, w_ref[:, P_MG:P_MLA])
    mla_ref[...] = _dot(xb, w_ref[:, P_MLA:P_SWA]).astype(bf16)
    swa_ref[...] = _dot(xb, w_ref[:, P_SWA:P_Z]).astype(bf16)
    z_ref[...] = _dot(xb, w_ref[:, P_Z:P_XBC]).astype(bf16)
    xbc_ref[...] = _dot(xb, w_ref[:, P_XBC:P_DT]).astype(bf16)
    dt_ref[...] = _dot(xb, w_ref[:, P_DT:P_COLS])


def _inproj(x, mod, gain, w, mod_base, tiles_per_mod):
    rows, d = x.shape
    widths = [(P_MG - P_QKVO, bf16), (P_MLA - P_MG, f32), (P_SWA - P_MLA, bf16), (P_Z - P_SWA, bf16),
              (P_XBC - P_Z, bf16), (P_DT - P_XBC, bf16), (P_COLS - P_DT, f32)]
    return pl.pallas_call(
        _inproj_body,
        grid=(rows // ROW_TILE,),
        in_specs=[pl.BlockSpec((ROW_TILE, d), lambda i: (i, 0)),
                  _mod_spec(d, mod_base, tiles_per_mod),
                  _resident((1, d)), _resident((d, P_COLS))],
        out_specs=[pl.BlockSpec((ROW_TILE, n), lambda i: (i, 0)) for n, _ in widths],
        out_shape=[jax.ShapeDtypeStruct((rows, n), dt) for n, dt in widths],
        compiler_params=_params(("parallel",)),
        name="inproj",
    )(x, mod, gain, w)


def _outproj_body(x_ref, mod_ref, a_ref, m_ref, w_ref, s_ref, wo_ref, o_ref):
    acc = _dot(a_ref[...], wo_ref[0:GROUP_W, :])
    acc = acc + _dot(m_ref[...], wo_ref[GROUP_W:2 * GROUP_W, :])
    acc = acc + _dot(w_ref[...], wo_ref[2 * GROUP_W:3 * GROUP_W, :])
    acc = acc + _dot(s_ref[...], wo_ref[3 * GROUP_W:4 * GROUP_W, :])
    o_ref[...] = x_ref[...] + mod_ref[5:6, :] * acc


def _outproj(x, mod, mixed, wo, mod_base, tiles_per_mod):
    rows, d = x.shape
    return pl.pallas_call(
        _outproj_body,
        grid=(rows // ROW_TILE,),
        in_specs=[pl.BlockSpec((ROW_TILE, d), lambda i: (i, 0)),
                  _mod_spec(d, mod_base, tiles_per_mod)]
                 + [pl.BlockSpec((ROW_TILE, GROUP_W), lambda i: (i, 0))] * 4
                 + [_resident(wo.shape)],
        out_specs=pl.BlockSpec((ROW_TILE, d), lambda i: (i, 0)),
        out_shape=jax.ShapeDtypeStruct((rows, d), f32),
        compiler_params=_params(("parallel",)),
        name="outproj",
    )(x, mod, *mixed, wo)


def _tri_masks(n):
    r = lax.broadcasted_iota(jnp.int32, (n, n), 0)
    c = lax.broadcasted_iota(jnp.int32, (n, n), 1)
    return r >= c, r <= c


def _seg_blocks(cl, t, width):
    return (pl.BlockSpec((cl, width), lambda b: (b, 0)), pl.BlockSpec((t, width), lambda b: (b, 0)))


def _seg_blocks_t(cl, t, nrow):
    return (pl.BlockSpec((nrow, cl), lambda b: (0, b)), pl.BlockSpec((nrow, t), lambda b: (0, b)))


def _mlstm_body(qc_ref, ql_ref, gcc_ref, gcl_ref, grc_ref, grl_ref, gbc_ref, gbr_ref, on_ref, oc_ref, ol_ref,
                hfc, hbc, hfl, hbl, *, need_ctx):
    L = SCAN_CHUNK
    lower, upper = _tri_masks(L)
    tri_lo = jnp.where(lower, 1.0, 0.0).astype(bf16)
    tri_up = jnp.where(upper, 1.0, 0.0).astype(bf16)
    ones_half = jnp.ones((L, HEAD_DIM), bf16)

    def direction(q_ref, gc_ref, gr_ref, h_ref, j, rev, carry):
        r0 = pl.multiple_of(j * L, L)
        gcol = gc_ref[pl.ds(r0, L), :] + gbc_ref[...]
        grow = gr_ref[:, pl.ds(r0, L)] + gbr_ref[...]
        bcols = _tri_dot(tri_up if rev else tri_lo, _log_sigmoid(gcol))
        brows = _dot_tri(_log_sigmoid(grow), tri_lo if rev else tri_up)
        mask = upper if rev else lower
        base = 2 * HEADS if rev else 0
        cs, ms = carry
        new_c, new_m = [], []
        for h in range(HEADS):
            ci, cf = base + h, base + HEADS + h
            icol, bcol = gcol[:, ci:ci + 1], bcols[:, cf:cf + 1]
            irow, brow = grow[ci:ci + 1, :], brows[cf:cf + 1, :]
            q = q_ref[pl.ds(r0, L), h * HEAD_DIM:(h + 1) * HEAD_DIM]
            k = q_ref[pl.ds(r0, L), GROUP_W + h * HEAD_DIM:GROUP_W + (h + 1) * HEAD_DIM] * (HEAD_DIM ** -0.5)
            v = q_ref[pl.ds(r0, L), 2 * GROUP_W + h * HEAD_DIM:2 * GROUP_W + (h + 1) * HEAD_DIM]
            vext = jnp.concatenate([v, ones_half], axis=1)
            C, m = cs[h], ms[h]
            D = jnp.where(mask, bcol - (brow - irow), NEG)
            m_inter = bcol + m
            m_t = jnp.maximum(m_inter, jnp.max(D, axis=1, keepdims=True))
            S = (_dot_nt(q, k) * jnp.exp(D - m_t)).astype(bf16)
            hx = jnp.exp(m_inter - m_t) * _dot(q, C.astype(bf16)) + _dot(S, vext)
            den = jnp.maximum(jnp.abs(hx[:, HEAD_DIM:HEAD_DIM + 1]), jnp.exp(-m_t))
            h_ref[pl.ds(r0, L), h * HEAD_DIM:(h + 1) * HEAD_DIM] = hx[:, :HEAD_DIM] / den
            bL = bcol[0:1] if rev else bcol[L - 1:L]
            g = bL - bcol + icol
            m_new = jnp.maximum(bL + m, jnp.max(g, axis=0, keepdims=True))
            wv = (jnp.exp(g - m_new) * vext.astype(f32)).astype(bf16)
            new_c.append(jnp.exp(bL + m - m_new) * C + _dot_tn(k, wv))
            new_m.append(m_new)
        return new_c, new_m

    def segment(q_ref, gc_ref, gr_ref, hf_ref, hb_ref, carry):
        n = q_ref.shape[0] // L

        def body(i, carry):
            fw, bw = carry
            fw = direction(q_ref, gc_ref, gr_ref, hf_ref, i, False, fw)
            bw = direction(q_ref, gc_ref, gr_ref, hb_ref, n - 1 - i, True, bw)
            return fw, bw

        return lax.fori_loop(0, n, body, carry)

    def init():
        return ([jnp.zeros((HEAD_DIM, 2 * HEAD_DIM), f32) for _ in range(HEADS)],
                [jnp.zeros((1, 1), f32) for _ in range(HEADS)])

    carry = segment(qc_ref, gcc_ref, grc_ref, hfc, hbc, (init(), init()))
    segment(ql_ref, gcl_ref, grl_ref, hfl, hbl, carry)

    def finish(q_ref, hf_ref, hb_ref, o_ref):
        blk = 256
        for r in range(0, o_ref.shape[0], blk):
            hh = hf_ref[r:r + blk, :] + hb_ref[r:r + blk, :]
            hn = hh * _group_rsqrt(hh, HEAD_DIM) * on_ref[...]
            og = q_ref[r:r + blk, 3 * GROUP_W:4 * GROUP_W].astype(f32)
            o_ref[r:r + blk, :] = (_sigmoid(og) * hn).astype(o_ref.dtype)

    finish(ql_ref, hfl, hbl, ol_ref)
    if need_ctx:
        finish(qc_ref, hfc, hbc, oc_ref)
    else:
        oc_ref[...] = jnp.zeros(oc_ref.shape, oc_ref.dtype)


def _mlstm(qkvo_c, qkvo_l, mg_c, mg_l, gate_b, out_norm, batch, need_ctx):
    cl, t = qkvo_c.shape[0] // batch, qkvo_l.shape[0] // batch
    ng = 4 * HEADS
    gb_col = jnp.zeros((1, LANES), f32).at[0, :ng].set(gate_b)
    gb_row = gate_b.reshape(ng, 1)
    qc_spec, ql_spec = _seg_blocks(cl, t, 4 * GROUP_W)
    gcc_spec, gcl_spec = _seg_blocks(cl, t, LANES)
    grc_spec, grl_spec = _seg_blocks_t(cl, t, ng)
    oc_spec, ol_spec = _seg_blocks(cl, t, GROUP_W)
    return pl.pallas_call(
        functools.partial(_mlstm_body, need_ctx=need_ctx),
        grid=(batch,),
        in_specs=[qc_spec, ql_spec, gcc_spec, gcl_spec, grc_spec, grl_spec,
                  _resident((1, LANES)), _resident((ng, 1)), _resident((1, GROUP_W))],
        out_specs=[oc_spec, ol_spec],
        out_shape=[jax.ShapeDtypeStruct((batch * cl, GROUP_W), bf16),
                   jax.ShapeDtypeStruct((batch * t, GROUP_W), bf16)],
        scratch_shapes=[pltpu.VMEM((cl, GROUP_W), f32), pltpu.VMEM((cl, GROUP_W), f32),
                        pltpu.VMEM((t, GROUP_W), f32), pltpu.VMEM((t, GROUP_W), f32)],
        compiler_params=_params(("parallel",)),
        name="mlstm",
    )(qkvo_c, qkvo_l, mg_c, mg_l, mg_c[:, :ng].T, mg_l[:, :ng].T, gb_col, gb_row, out_norm.reshape(1, GROUP_W))


def _ssd_body(zc_ref, zl_ref, xc_ref, xl_ref, dcc_ref, dcl_ref, drc_ref, drl_ref, cw_ref, cb_ref, dbc_ref, dbr_ref,
              alc_ref, alr_ref, dsk_ref, ng_ref, oc_ref, ol_ref,
              xac, xal, yfc, ybc, yfl, ybl, xpad, *, need_ctx):
    L = SCAN_CHUNK
    N = SSD_STATE
    lower, upper = _tri_masks(L)
    tri_lo = jnp.where(lower, 1.0, 0.0).astype(bf16)
    tri_up = jnp.where(upper, 1.0, 0.0).astype(bf16)
    a_col = -jnp.exp(alc_ref[...])
    a_row = -jnp.exp(alr_ref[...])
    pad = 8
    half = SSD_CONV // 2

    def conv_act(x_ref, xa_ref):
        n = x_ref.shape[0]
        zeros = jnp.zeros((pad, SSD_XBC), f32)
        xpad[0:pad, :] = zeros
        xpad[pad + n:2 * pad + n, :] = zeros
        blk = 256
        for r in range(0, n, blk):
            xpad[pad + r:pad + r + blk, :] = x_ref[r:r + blk, :].astype(f32)
        for r in range(0, n, blk):
            y = jnp.zeros((blk, SSD_XBC), f32) + cb_ref[...]
            for kk in range(SSD_CONV):
                o = pad + r + kk - half
                y = y + cw_ref[kk:kk + 1, :] * xpad[o:o + blk, :]
            xa_ref[r:r + blk, :] = _silu(y)

    conv_act(xc_ref, xac)

    def direction(xa_ref, dc_ref, dr_ref, y_ref, j, rev, states):
        r0 = pl.multiple_of(j * L, L)
        dtc = _softplus(dc_ref[pl.ds(r0, L), :] + dbc_ref[...])
        dtr = _softplus(dr_ref[:, pl.ds(r0, L)] + dbr_ref[...])
        cumc = _tri_dot(tri_up if rev else tri_lo, dtc * a_col)
        cumr = _dot_tri(dtr * a_row, tri_lo if rev else tri_up)
        mask = upper if rev else lower
        base = HEADS if rev else 0
        new_states = []
        for g in range(SSD_GROUPS):
            Bm = xa_ref[pl.ds(r0, L), GROUP_W + g * N:GROUP_W + (g + 1) * N].astype(bf16)
            Cm = xa_ref[pl.ds(r0, L), GROUP_W + (SSD_GROUPS + g) * N:GROUP_W + (SSD_GROUPS + g + 1) * N].astype(bf16)
            G = _dot_nt(Cm, Bm)
            for hh in range(HEADS // SSD_GROUPS):
                h = g * (HEADS // SSD_GROUPS) + hh
                idx = base + h
                cum_c, cum_r = cumc[:, idx:idx + 1], cumr[idx:idx + 1, :]
                dt_c, dt_r = dtc[:, idx:idx + 1], dtr[idx:idx + 1, :]
                St = states[h]
                x = xa_ref[pl.ds(r0, L), h * HEAD_DIM:(h + 1) * HEAD_DIM]
                W = (G * jnp.exp(jnp.where(mask, cum_c - cum_r, NEG)) * dt_r).astype(bf16)
                y = _dot(W, x.astype(bf16)) + jnp.exp(cum_c) * _dot(Cm, St.astype(bf16))
                y_ref[pl.ds(r0, L), h * HEAD_DIM:(h + 1) * HEAD_DIM] = y
                cumL = cum_c[0:1] if rev else cum_c[L - 1:L]
                dec = jnp.exp(cumL - cum_c) * dt_c
                new_states.append(jnp.exp(cumL) * St + _dot_tn(Bm, (dec * x).astype(bf16)))
        return new_states

    def segment(xa_ref, dc_ref, dr_ref, yf_ref, yb_ref, carry):
        n = xa_ref.shape[0] // L

        def body(i, carry):
            fw, bw = carry
            fw = direction(xa_ref, dc_ref, dr_ref, yf_ref, i, False, fw)
            bw = direction(xa_ref, dc_ref, dr_ref, yb_ref, n - 1 - i, True, bw)
            return fw, bw

        return lax.fori_loop(0, n, body, carry)

    init = [jnp.zeros((N, HEAD_DIM), f32) for _ in range(HEADS)]
    carry = segment(xac, dcc_ref, drc_ref, yfc, ybc, (init, list(init)))
    conv_act(xl_ref, xal)
    segment(xal, dcl_ref, drl_ref, yfl, ybl, carry)

    def finish(z_ref, xa_ref, yf_ref, yb_ref, o_ref):
        blk = 256
        for r in range(0, o_ref.shape[0], blk):
            yy = yf_ref[r:r + blk, :] + yb_ref[r:r + blk, :] + dsk_ref[...] * xa_ref[r:r + blk, 0:GROUP_W]
            gt = yy * _silu(z_ref[r:r + blk, :].astype(f32))
            o_ref[r:r + blk, :] = (gt * _group_rsqrt(gt, SSD_NORM_GROUP) * ng_ref[...]).astype(o_ref.dtype)

    finish(zl_ref, xal, yfl, ybl, ol_ref)
    if need_ctx:
        finish(zc_ref, xac, yfc, ybc, oc_ref)
    else:
        oc_ref[...] = jnp.zeros(oc_ref.shape, oc_ref.dtype)


def _ssd(z_c, z_l, xbc_c, xbc_l, dt_c, dt_l, conv_w, conv_b, dt_bias, a_log, d_skip, norm_g, batch, need_ctx):
    cl, t = z_c.shape[0] // batch, z_l.shape[0] // batch
    nd = 2 * HEADS
    pad_row = lambda v: jnp.zeros((1, LANES), f32).at[0, :nd].set(v)
    cw = jnp.zeros((8, SSD_XBC), f32).at[:SSD_CONV].set(conv_w)
    zc_spec, zl_spec = _seg_blocks(cl, t, GROUP_W)
    xc_spec, xl_spec = _seg_blocks(cl, t, SSD_XBC)
    dcc_spec, dcl_spec = _seg_blocks(cl, t, LANES)
    drc_spec, drl_spec = _seg_blocks_t(cl, t, nd)
    oc_spec, ol_spec = _seg_blocks(cl, t, GROUP_W)
    return pl.pallas_call(
        functools.partial(_ssd_body, need_ctx=need_ctx),
        grid=(batch,),
        in_specs=[zc_spec, zl_spec, xc_spec, xl_spec, dcc_spec, dcl_spec, drc_spec, drl_spec,
                  _resident((8, SSD_XBC)), _resident((1, SSD_XBC)), _resident((1, LANES)), _resident((nd, 1)),
                  _resident((1, LANES)), _resident((nd, 1)), _resident((1, GROUP_W)), _resident((1, GROUP_W))],
        out_specs=[oc_spec, ol_spec],
        out_shape=[jax.ShapeDtypeStruct((batch * cl, GROUP_W), bf16),
                   jax.ShapeDtypeStruct((batch * t, GROUP_W), bf16)],
        scratch_shapes=[pltpu.VMEM((cl, SSD_XBC), f32), pltpu.VMEM((t, SSD_XBC), f32),
                        pltpu.VMEM((cl, GROUP_W), f32), pltpu.VMEM((cl, GROUP_W), f32),
                        pltpu.VMEM((t, GROUP_W), f32), pltpu.VMEM((t, GROUP_W), f32),
                        pltpu.VMEM((t + 16, SSD_XBC), f32)],
        compiler_params=_params(("parallel",)),
        name="ssd",
    )(z_c, z_l, xbc_c, xbc_l, dt_c, dt_l, dt_c[:, :nd].T, dt_l[:, :nd].T, cw, conv_b.reshape(1, SSD_XBC),
      pad_row(dt_bias), dt_bias.reshape(nd, 1), pad_row(a_log), a_log.reshape(nd, 1),
      jnp.repeat(d_skip, HEAD_DIM).reshape(1, GROUP_W), norm_g.reshape(1, GROUP_W))


def _mla_body(ac_ref, al_ref, qn_ref, kvn_ref, wq_ref, wkv_ref, qg_ref, kg_ref, krg_ref, cos_ref, sin_ref,
              oc_ref, ol_ref, q_s, k_s, v_s, *, need_ctx):
    cl, t = ac_ref.shape[0], al_ref.shape[0]
    scale = MLA_QK ** -0.5
    lane = lax.broadcasted_iota(jnp.int32, (1, LANES), 1)
    is_nope = lane < MLA_NOPE
    is_rope = (lane >= MLA_NOPE) & (lane < MLA_QK)

    def head_norm(x):
        sq = x * x
        s_n = jnp.sum(jnp.where(is_nope, sq, 0.0), axis=1, keepdims=True) * (1.0 / MLA_NOPE)
        s_r = jnp.sum(jnp.where(is_rope, sq, 0.0), axis=1, keepdims=True) * (1.0 / MLA_ROPE)
        return x * jnp.where(is_nope, lax.rsqrt(s_n + RMS_EPS), lax.rsqrt(s_r + RMS_EPS))

    def project(a_ref, row0, n, roped):
        blk = 256
        for r in range(0, n, blk):
            a = a_ref[r:r + blk, :].astype(f32)
            aq, akv, akr = a[:, 0:GROUP_W], a[:, GROUP_W:GROUP_W + LANES], a[:, GROUP_W + LANES:GROUP_W + 2 * LANES]
            qh = _dot((aq * _row_rsqrt(aq, GROUP_W) * qn_ref[...]).astype(bf16), wq_ref[...])
            kv = _dot((akv * _row_rsqrt(akv, LANES) * kvn_ref[...]).astype(bf16), wkv_ref[...])
            kr = akr * _row_rsqrt(akr, MLA_ROPE) * krg_ref[...]
            if roped:
                cos, sin = cos_ref[r:r + blk, :], sin_ref[r:r + blk, :]
                kr = _rope(kr, cos, sin, MLA_ROPE // 4)
            for h in range(HEADS):
                qx = head_norm(qh[:, h * LANES:(h + 1) * LANES]) * qg_ref[...]
                if roped:
                    qx = _rope(qx, cos, sin, MLA_ROPE // 4)
                q_s[h, row0 + r:row0 + r + blk, :] = (qx * scale).astype(bf16)
                kx = kv[:, h * LANES:(h + 1) * LANES]
                kx = kx * _row_rsqrt(kx, MLA_NOPE) * kg_ref[...]
                k_s[h, row0 + r:row0 + r + blk, :] = (kx + kr).astype(bf16)
            v_s[row0 + r:row0 + r + blk, :] = kv[:, HEADS * LANES:].astype(bf16)

    project(ac_ref, 0, cl, False)
    project(al_ref, cl, t, True)

    half_lane = lax.broadcasted_iota(jnp.int32, (1, LANES), 1) < HEAD_DIM

    def attend(q0, nq, nk, o_ref, o0):
        outs = []
        for pair in range(HEADS // 2):
            res = []
            for h in (2 * pair, 2 * pair + 1):
                s = _dot_nt(q_s[h, pl.ds(q0, nq), :], k_s[h, 0:nk, :])
                p = jnp.exp(s - jnp.max(s, axis=1, keepdims=True))
                den = jnp.sum(p, axis=1, keepdims=True)
                res.append(_dot(p.astype(bf16), v_s[0:nk, pair * LANES:(pair + 1) * LANES]) / den)
            outs.append(jnp.where(half_lane, res[0], res[1]))
        o_ref[pl.ds(o0, nq), :] = jnp.concatenate(outs, axis=1).astype(o_ref.dtype)

    def body(i, _):
        o0 = pl.multiple_of(i * MLA_QBLK, MLA_QBLK)
        attend(cl + o0, MLA_QBLK, cl + t, ol_ref, o0)
        return 0

    lax.fori_loop(0, t // MLA_QBLK, body, 0)
    if need_ctx:
        attend(0, cl, cl, oc_ref, 0)
    else:
        oc_ref[...] = jnp.zeros(oc_ref.shape, oc_ref.dtype)


def _mla(a_c, a_l, q_norm, kv_norm, wq_b, wkv_b, q_gain, k_gain, cos, sin, batch, need_ctx):
    cl, t = a_c.shape[0] // batch, a_l.shape[0] // batch
    kvl = kv_norm.shape[0]
    wq = jnp.zeros((GROUP_W, HEADS, LANES), f32).at[:, :, :MLA_QK].set(wq_b.reshape(GROUP_W, HEADS, MLA_QK))
    wkv4 = wkv_b.reshape(kvl, HEADS, MLA_NOPE + HEAD_DIM)
    wk = jnp.zeros((kvl, HEADS, LANES), f32).at[:, :, :MLA_NOPE].set(wkv4[:, :, :MLA_NOPE])
    wkv = jnp.concatenate([wk.reshape(kvl, HEADS * LANES), wkv4[:, :, MLA_NOPE:].reshape(kvl, GROUP_W)], axis=1)
    slab = lambda v, off: jnp.zeros((1, LANES), f32).at[0, off:off + v.shape[0]].set(v)
    qg = slab(q_gain, 0)
    kg = slab(k_gain[:MLA_NOPE], 0)
    krg = slab(k_gain[MLA_NOPE:], MLA_NOPE)
    ac_spec, al_spec = _seg_blocks(cl, t, 2 * GROUP_W)
    oc_spec, ol_spec = _seg_blocks(cl, t, GROUP_W)
    return pl.pallas_call(
        functools.partial(_mla_body, need_ctx=need_ctx),
        grid=(batch,),
        in_specs=[ac_spec, al_spec, _resident((1, GROUP_W)), _resident((1, LANES)),
                  _resident((GROUP_W, HEADS * LANES)), _resident((kvl, HEADS * LANES + GROUP_W)),
                  _resident((1, LANES)), _resident((1, LANES)), _resident((1, LANES)),
                  _resident((t, LANES)), _resident((t, LANES))],
        out_specs=[oc_spec, ol_spec],
        out_shape=[jax.ShapeDtypeStruct((batch * cl, GROUP_W), bf16),
                   jax.ShapeDtypeStruct((batch * t, GROUP_W), bf16)],
        scratch_shapes=[pltpu.VMEM((HEADS, cl + t, LANES), bf16), pltpu.VMEM((HEADS, cl + t, LANES), bf16),
                        pltpu.VMEM((cl + t, GROUP_W), bf16)],
        compiler_params=_params(("parallel",)),
        name="mla",
    )(a_c, a_l, q_norm.reshape(1, GROUP_W), kv_norm.reshape(1, kvl),
      wq.reshape(GROUP_W, HEADS * LANES).astype(bf16), wkv.astype(bf16), qg, kg, krg, cos, sin)


def _swa_body(wc_ref, wl_ref, qg_ref, kg_ref, sink_ref, cos_ref, sin_ref, oc_ref, ol_ref,
              q_s, qc_s, k_s, v_s, kc_s, vc_s, *, need_ctx):
    cl, t = wc_ref.shape[0], wl_ref.shape[0]
    scale = HEAD_DIM ** -0.5
    kvw = SWA_KV_HEADS * HEAD_DIM
    blk = SWA_BLK
    quarter = HEAD_DIM // 4

    def project(w_ref, n, roped, qdst, kdst, vdst, row0):
        step = 256
        for r in range(0, n, step):
            w = w_ref[r:r + step, :].astype(f32)
            q, k = w[:, 0:GROUP_W], w[:, GROUP_W:GROUP_W + kvw]
            q = q * _group_rsqrt(q, HEAD_DIM) * qg_ref[...]
            k = k * _group_rsqrt(k, HEAD_DIM) * kg_ref[...]
            if roped:
                cos, sin = cos_ref[r:r + step, :], sin_ref[r:r + step, :]
                q = jnp.concatenate([_rope(q[:, 0:LANES], cos, sin, quarter),
                                     _rope(q[:, LANES:2 * LANES], cos, sin, quarter)], axis=1)
                k = _rope(k, cos, sin, quarter)
            qdst[r:r + step, :] = (q * scale).astype(bf16)
            kdst[row0 + r:row0 + r + step, :] = k.astype(bf16)
            vdst[row0 + r:row0 + r + step, :] = w_ref[r:r + step, GROUP_W + kvw:GROUP_W + 2 * kvw]

    zeros = jnp.zeros((blk, kvw), bf16)
    for s in (k_s, v_s):
        s[0:blk, :] = zeros
        s[blk + t:2 * blk + t, :] = zeros
    project(wc_ref, cl, False, qc_s, kc_s, vc_s, 0)
    project(wl_ref, t, True, q_s, k_s, v_s, blk)

    rr = lax.broadcasted_iota(jnp.int32, (blk, 3 * blk), 0)
    jj = lax.broadcasted_iota(jnp.int32, (blk, 3 * blk), 1)
    band = (jj - rr >= 0) & (jj - rr <= 2 * SWA_WINDOW)

    def lat_block(n, _):
        r0 = pl.multiple_of(n * blk, blk)
        kpos = jj + (n - 1) * blk
        valid = band & (kpos >= 0) & (kpos < t)
        outs = []
        for h in range(HEADS):
            kv = h // (HEADS // SWA_KV_HEADS)
            ksl = slice(kv * HEAD_DIM, (kv + 1) * HEAD_DIM)
            q = q_s[pl.ds(r0, blk), h * HEAD_DIM:(h + 1) * HEAD_DIM]
            s_loc = jnp.where(valid, _dot_nt(q, k_s[pl.ds(r0, 3 * blk), ksl]), NEG)
            s_ctx = _dot_nt(q, kc_s[:, ksl])
            sink = sink_ref[:, h:h + 1]
            m = jnp.maximum(jnp.maximum(jnp.max(s_loc, axis=1, keepdims=True),
                                        jnp.max(s_ctx, axis=1, keepdims=True)), sink)
            p_loc, p_ctx = jnp.exp(s_loc - m), jnp.exp(s_ctx - m)
            den = (jnp.sum(p_loc, axis=1, keepdims=True) + jnp.sum(p_ctx, axis=1, keepdims=True)
                   + jnp.exp(sink - m))
            o = _dot(p_loc.astype(bf16), v_s[pl.ds(r0, 3 * blk), ksl]) + _dot(p_ctx.astype(bf16), vc_s[:, ksl])
            outs.append(o / den)
        ol_ref[pl.ds(r0, blk), :] = jnp.concatenate(outs, axis=1).astype(ol_ref.dtype)
        return 0

    lax.fori_loop(0, t // blk, lat_block, 0)

    if need_ctx:
        outs = []
        for h in range(HEADS):
            kv = h // (HEADS // SWA_KV_HEADS)
            ksl = slice(kv * HEAD_DIM, (kv + 1) * HEAD_DIM)
            s = _dot_nt(qc_s[:, h * HEAD_DIM:(h + 1) * HEAD_DIM], kc_s[:, ksl])
            sink = sink_ref[:, h:h + 1]
            m = jnp.maximum(jnp.max(s, axis=1, keepdims=True), sink)
            p = jnp.exp(s - m)
            den = jnp.sum(p, axis=1, keepdims=True) + jnp.exp(sink - m)
            outs.append(_dot(p.astype(bf16), vc_s[:, ksl]) / den)
        oc_ref[...] = jnp.concatenate(outs, axis=1).astype(oc_ref.dtype)
    else:
        oc_ref[...] = jnp.zeros(oc_ref.shape, oc_ref.dtype)


def _swa(w_c, w_l, q_gain, k_gain, sink, cos, sin, batch, need_ctx):
    cl, t = w_c.shape[0] // batch, w_l.shape[0] // batch
    kvw = SWA_KV_HEADS * HEAD_DIM
    wc_spec, wl_spec = _seg_blocks(cl, t, 2 * GROUP_W)
    oc_spec, ol_spec = _seg_blocks(cl, t, GROUP_W)
    return pl.pallas_call(
        functools.partial(_swa_body, need_ctx=need_ctx),
        grid=(batch,),
        in_specs=[wc_spec, wl_spec, _resident((1, GROUP_W)), _resident((1, kvw)), _resident((1, LANES)),
                  _resident((t, LANES)), _resident((t, LANES))],
        out_specs=[oc_spec, ol_spec],
        out_shape=[jax.ShapeDtypeStruct((batch * cl, GROUP_W), bf16),
                   jax.ShapeDtypeStruct((batch * t, GROUP_W), bf16)],
        scratch_shapes=[pltpu.VMEM((t, GROUP_W), bf16), pltpu.VMEM((cl, GROUP_W), bf16),
                        pltpu.VMEM((t + 2 * SWA_BLK, kvw), bf16), pltpu.VMEM((t + 2 * SWA_BLK, kvw), bf16),
                        pltpu.VMEM((cl, kvw), bf16), pltpu.VMEM((cl, kvw), bf16)],
        compiler_params=_params(("parallel",)),
        name="swa",
    )(w_c, w_l, jnp.tile(q_gain, HEADS).reshape(1, GROUP_W), jnp.tile(k_gain, SWA_KV_HEADS).reshape(1, kvw),
      jnp.zeros((1, LANES), f32).at[0, :HEADS].set(sink), cos, sin)


def _rope_tables(t, rot_dim, lane0):
    pos = jnp.arange(t)
    row, col = (pos // GRID_W).astype(f32), (pos % GRID_W).astype(f32)
    nf = rot_dim // 4
    inv = ROPE_BASE ** (-jnp.arange(nf, dtype=f32) / nf)
    ar, ac = row[:, None] * inv, col[:, None] * inv
    ang = jnp.concatenate([ar, ar, ac, ac], axis=-1)
    return jnp.cos(ang), jnp.sin(ang)


def _pack_in_weight(w_in):
    o = 0
    seg = {}
    for name, n in (("qkvo", 4 * GROUP_W), ("mg", 4 * HEADS), ("a_q", GROUP_W), ("a_kv", GROUP_W // 2),
                    ("a_kr", MLA_ROPE), ("swa", 2 * GROUP_W), ("z", GROUP_W), ("xbc", SSD_XBC), ("dt", 2 * HEADS)):
        seg[name] = w_in[:, o:o + n]
        o += n
    d = w_in.shape[0]
    zeros = lambda n: jnp.zeros((d, n), w_in.dtype)
    packed = jnp.concatenate([
        seg["qkvo"], seg["mg"], zeros(LANES - 4 * HEADS),
        seg["a_q"], seg["a_kv"], zeros(MLA_NOPE), seg["a_kr"], zeros(LANES - MLA_QK),
        seg["swa"], seg["z"], seg["xbc"], seg["dt"], zeros(LANES - 2 * HEADS)], axis=1)
    return packed.astype(bf16)


def kernel(x, c, ctx, c_ctx, w_mod, b_mod, ffn1_norm, ffn1_wi, ffn1_wo, mix_norm, w_in, w_out, mlstm_gate_b, mlstm_out_norm, mla_q_norm, mla_kv_norm, mla_wq_b, mla_wkv_b, mla_q_gain, mla_k_gain, swa_q_gain, swa_k_gain, swa_sink, ssd_conv_w, ssd_conv_b, ssd_dt_bias, ssd_a_log, ssd_d, ssd_norm, ffn2_norm, ffn2_wi, ffn2_wo):
    b, t, d = x.shape
    cl = ctx.shape[1]
    depth = w_mod.shape[0]
    dff = ffn1_wo.shape[1]
    lat_tiles = t // ROW_TILE

    cos_m, sin_m = _rope_tables(t, MLA_ROPE, MLA_NOPE)
    pad_id = lambda tab, fill: jnp.concatenate(
        [jnp.full((t, MLA_NOPE), fill, f32), tab, jnp.full((t, LANES - MLA_QK), fill, f32)], axis=1)
    cos_m, sin_m = pad_id(cos_m, 1.0), pad_id(sin_m, 0.0)
    cos_s, sin_s = _rope_tables(t, HEAD_DIM, 0)
    cos_s, sin_s = jnp.tile(cos_s, (1, LANES // HEAD_DIM)), jnp.tile(sin_s, (1, LANES // HEAD_DIM))

    h = x.reshape(b * t, d)
    hc = ctx.reshape(b * cl, d)
    cc = jnp.concatenate([c, c_ctx[None, :]], axis=0)
    for l in range(depth):
        need_ctx = l < depth - 1
        mod = _modulation(cc, w_mod[l], b_mod[l]).reshape(b + 1, N_MOD, d)
        streams = ((0, lat_tiles), (b, None))

        def ffn(v, s, norm, wi, wo, mi):
            return _ffn(v, mod, norm.reshape(1, d), wi[:, :dff].astype(bf16), wi[:, dff:].astype(bf16),
                        wo.astype(bf16), mi, *s)

        h = ffn(h, streams[0], ffn1_norm[l], ffn1_wi[l], ffn1_wo[l], 0)
        hc = ffn(hc, streams[1], ffn1_norm[l], ffn1_wi[l], ffn1_wo[l], 0)
        w_packed = _pack_in_weight(w_in[l])
        qkvo_l, mg_l, mla_l, swa_l, z_l, xbc_l, dt_l = _inproj(h, mod, mix_norm[l].reshape(1, d), w_packed, *streams[0])
        qkvo_c, mg_c, mla_c, swa_c, z_c, xbc_c, dt_c = _inproj(hc, mod, mix_norm[l].reshape(1, d), w_packed, *streams[1])
        a_c, a_l = _mlstm(qkvo_c, qkvo_l, mg_c, mg_l, mlstm_gate_b[l], mlstm_out_norm[l], b, need_ctx)
        m_c, m_l = _mla(mla_c, mla_l, mla_q_norm[l], mla_kv_norm[l], mla_wq_b[l], mla_wkv_b[l], mla_q_gain[l],
                        mla_k_gain[l], cos_m, sin_m, b, need_ctx)
        s_c, s_l = _swa(swa_c, swa_l, swa_q_gain[l], swa_k_gain[l], swa_sink[l], cos_s, sin_s, b, need_ctx)
        d_c, d_l = _ssd(z_c, z_l, xbc_c, xbc_l, dt_c, dt_l, ssd_conv_w[l], ssd_conv_b[l], ssd_dt_bias[l],
                        ssd_a_log[l], ssd_d[l], ssd_norm[l], b, need_ctx)
        wo_b = w_out[l].astype(bf16)
        h = _outproj(h, mod, (a_l, m_l, s_l, d_l), wo_b, *streams[0])
        h = ffn(h, streams[0], ffn2_norm[l], ffn2_wi[l], ffn2_wo[l], 6)
        if need_ctx:
            hc = _outproj(hc, mod, (a_c, m_c, s_c, d_c), wo_b, *streams[1])
            hc = ffn(hc, streams[1], ffn2_norm[l], ffn2_wi[l], ffn2_wo[l], 6)
    return h.reshape(b, t, d)
```

```python
import functools
import math

import jax
import jax.numpy as jnp
from jax import lax
from jax.experimental import pallas as pl
from jax.experimental.pallas import tpu as pltpu

f32 = jnp.float32
bf16 = jnp.bfloat16

RMS_EPS = 1e-6
ROPE_BASE = 10000.0
GRID_W = 64
N_MOD = 9
HEADS = 4
HEAD_DIM = 64
GROUP_W = HEADS * HEAD_DIM
MLA_NOPE = 64
MLA_ROPE = 32
MLA_QK = MLA_NOPE + MLA_ROPE
SWA_KV_HEADS = 2
SWA_WINDOW = 128
SWA_BLK = 128
SSD_STATE = 64
SSD_GROUPS = 2
SSD_CONV = 5
SSD_XBC = GROUP_W + 2 * SSD_GROUPS * SSD_STATE
SSD_NORM_GROUP = 128

LANES = 128
ROW_TILE = 512
FF_CHUNK = 256
SCAN_CHUNK = 128
MLA_QBLK = 256
NEG = -1e30
VMEM_LIMIT = 56 * 1024 * 1024

P_QKVO = 0
P_MG = 1024
P_MLA = 1152
P_SWA = 1664
P_Z = 2176
P_XBC = 2432
P_DT = 2944
P_COLS = 3072


def _dot(a, b):
    return jnp.dot(a, b, preferred_element_type=f32)


def _dot_nt(a, b):
    return lax.dot_general(a, b, (((1,), (1,)), ((), ())), preferred_element_type=f32)


def _dot_tn(a, b):
    return lax.dot_general(a, b, (((0,), (0,)), ((), ())), preferred_element_type=f32)


def _sigmoid(x):
    return 1.0 / (1.0 + jnp.exp(-x))


def _silu(x):
    return x * _sigmoid(x)


def _softplus(x):
    return jnp.maximum(x, 0.0) + jnp.log(1.0 + jnp.exp(-jnp.abs(x)))


def _log_sigmoid(x):
    return -_softplus(-x)


def _split3(x):
    hi = x.astype(bf16)
    r = x - hi.astype(f32)
    mid = r.astype(bf16)
    lo = (r - mid.astype(f32)).astype(bf16)
    return hi, mid, lo


def _tri_dot(tri, x):
    hi, mid, lo = _split3(x)
    return _dot(tri, hi) + _dot(tri, mid) + _dot(tri, lo)


def _dot_tri(x, tri):
    hi, mid, lo = _split3(x)
    return _dot(hi, tri) + _dot(mid, tri) + _dot(lo, tri)


def _dot2(x, w):
    hi = x.astype(bf16)
    mid = (x - hi.astype(f32)).astype(bf16)
    return _dot(hi, w) + _dot(mid, w)


def _group_mean_matrix(n, gsz):
    r = lax.broadcasted_iota(jnp.int32, (n, n), 0) // gsz
    c = lax.broadcasted_iota(jnp.int32, (n, n), 1) // gsz
    return jnp.where(r == c, 1.0 / gsz, 0.0).astype(bf16)


def _group_rsqrt(x, gmat):
    return lax.rsqrt(_dot2(x * x, gmat) + RMS_EPS)


def _row_rsqrt(x, n_real):
    return lax.rsqrt(jnp.sum(x * x, axis=1, keepdims=True) * (1.0 / n_real) + RMS_EPS)


def _rotate_half_matrix(quarter):
    r = lax.broadcasted_iota(jnp.int32, (LANES, LANES), 0)
    c = lax.broadcasted_iota(jnp.int32, (LANES, LANES), 1)
    first = (c % (2 * quarter)) < quarter
    return jnp.where(first & (r == c + quarter), -1.0, jnp.where(~first & (r == c - quarter), 1.0, 0.0)).astype(bf16)


def _rope(x, cos, sin, rot):
    return x * cos + _dot2(x, rot) * sin


def _resident(shape):
    nd = len(shape)
    return pl.BlockSpec(shape, lambda *_: (0,) * nd, pipeline_mode=pl.Buffered(1))


def _params(sem):
    return pltpu.CompilerParams(dimension_semantics=sem, vmem_limit_bytes=VMEM_LIMIT)


def _mod_body(c_ref, w_ref, b_ref, o_ref):
    s = _silu(c_ref[...])
    o_ref[...] = jnp.dot(s, w_ref[...], preferred_element_type=f32,
                         precision=lax.Precision.HIGHEST) + b_ref[...]


def _modulation(cc, w, b):
    m, d = cc.shape
    n = w.shape[1]
    tn = 1024
    return pl.pallas_call(
        _mod_body,
        grid=(n // tn,),
        in_specs=[pl.BlockSpec((m, d), lambda j: (0, 0)),
                  pl.BlockSpec((d, tn), lambda j: (0, j)),
                  pl.BlockSpec((1, tn), lambda j: (0, j))],
        out_specs=pl.BlockSpec((m, tn), lambda j: (0, j)),
        out_shape=jax.ShapeDtypeStruct((m, n), f32),
        compiler_params=_params(("arbitrary",)),
        name="modulation",
    )(cc, w, b.reshape(1, n))


def _norm_mod(x, gain, shift, scale):
    xn = x * _row_rsqrt(x, x.shape[1]) * gain
    return (xn * (1.0 + scale) + shift).astype(bf16)


def _ffn_body(x_ref, mod_ref, g_ref, wig_ref, wiu_ref, wo_ref, o_ref, *, mi):
    x = x_ref[...]
    xb = _norm_mod(x, g_ref[...], mod_ref[mi:mi + 1, :], mod_ref[mi + 1:mi + 2, :])
    acc = jnp.zeros(x.shape, f32)
    for c in range(wig_ref.shape[1] // FF_CHUNK):
        sl = slice(c * FF_CHUNK, (c + 1) * FF_CHUNK)
        g = _dot(xb, wig_ref[:, sl])
        u = _dot(xb, wiu_ref[:, sl])
        acc = acc + _dot((_silu(g) * u).astype(bf16), wo_ref[sl, :])
    o_ref[...] = x + (0.5 * mod_ref[mi + 2:mi + 3, :]) * acc


def _mod_spec(d, mod_base, tiles_per_mod):
    if tiles_per_mod is None:
        return pl.BlockSpec((None, N_MOD, d), lambda i: (mod_base, 0, 0))
    return pl.BlockSpec((None, N_MOD, d), lambda i: (mod_base + i // tiles_per_mod, 0, 0))


def _ffn(x, mod, gain, wig, wiu, wo, mi, mod_base, tiles_per_mod):
    rows, d = x.shape
    dff = wig.shape[1]
    return pl.pallas_call(
        functools.partial(_ffn_body, mi=mi),
        grid=(rows // ROW_TILE,),
        in_specs=[pl.BlockSpec((ROW_TILE, d), lambda i: (i, 0)),
                  _mod_spec(d, mod_base, tiles_per_mod),
                  _resident((1, d)), _resident((d, dff)), _resident((d, dff)), _resident((dff, d))],
        out_specs=pl.BlockSpec((ROW_TILE, d), lambda i: (i, 0)),
        out_shape=jax.ShapeDtypeStruct((rows, d), f32),
        compiler_params=_params(("parallel",)),
        name="ffn",
    )(x, mod, gain, wig, wiu, wo)


def _inproj_body(x_ref, mod_ref, g_ref, w_ref, qkvo_ref, mg_ref, mla_ref, swa_ref, z_ref, xbc_ref, dt_ref):
    xb = _norm_mod(x_ref[...], g_ref[...], mod_ref[3:4, :], mod_ref[4:5, :])
    qkvo_ref[...] = _dot(xb, w_ref[:, P_QKVO:P_MG]).astype(bf16)
    mg_ref[...] = _dot(xb, w_ref[:, P_MG:P_MLA])
    mla_ref[...] = _dot(xb, w_ref[:, P_MLA:P_SWA]).astype(bf16)
    swa_ref[...] = _dot(xb, w_ref[:, P_SWA:P_Z]).astype(bf16)
    z_ref[...] = _dot(xb, w_ref[:, P_Z:P_XBC]).astype(bf16)
    xbc_ref[...] = _dot(xb, w_ref[:, P_XBC:P_DT]).astype(bf16)
    dt_ref[...] = _dot(xb, w_ref[:, P_DT:P_COLS])


def _inproj(x, mod, gain, w, mod_base, tiles_per_mod):
    rows, d = x.shape
    widths = [(P_MG - P_QKVO, bf16), (P_MLA - P_MG, f32), (P_SWA - P_MLA, bf16), (P_Z - P_SWA, bf16),
              (P_XBC - P_Z, bf16), (P_DT - P_XBC, bf16), (P_COLS - P_DT, f32)]
    return pl.pallas_call(
        _inproj_body,
        grid=(rows // ROW_TILE,),
        in_specs=[pl.BlockSpec((ROW_TILE, d), lambda i: (i, 0)),
                  _mod_spec(d, mod_base, tiles_per_mod),
                  _resident((1, d)), _resident((d, P_COLS))],
        out_specs=[pl.BlockSpec((ROW_TILE, n), lambda i: (i, 0)) for n, _ in widths],
        out_shape=[jax.ShapeDtypeStruct((rows, n), dt) for n, dt in widths],
        compiler_params=_params(("parallel",)),
        name="inproj",
    )(x, mod, gain, w)


def _outproj_body(x_ref, mod_ref, a_ref, m_ref, w_ref, s_ref, wo_ref, o_ref):
    acc = _dot(a_ref[...], wo_ref[0:GROUP_W, :])
    acc = acc + _dot(m_ref[...], wo_ref[GROUP_W:2 * GROUP_W, :])
    acc = acc + _dot(w_ref[...], wo_ref[2 * GROUP_W:3 * GROUP_W, :])
    acc = acc + _dot(s_ref[...], wo_ref[3 * GROUP_W:4 * GROUP_W, :])
    o_ref[...] = x_ref[...] + mod_ref[5:6, :] * acc


def _outproj(x, mod, mixed, wo, mod_base, tiles_per_mod):
    rows, d = x.shape
    return pl.pallas_call(
        _outproj_body,
        grid=(rows // ROW_TILE,),
        in_specs=[pl.BlockSpec((ROW_TILE, d), lambda i: (i, 0)),
                  _mod_spec(d, mod_base, tiles_per_mod)]
                 + [pl.BlockSpec((ROW_TILE, GROUP_W), lambda i: (i, 0))] * 4
                 + [_resident(wo.shape)],
        out_specs=pl.BlockSpec((ROW_TILE, d), lambda i: (i, 0)),
        out_shape=jax.ShapeDtypeStruct((rows, d), f32),
        compiler_params=_params(("parallel",)),
        name="outproj",
    )(x, mod, *mixed, wo)


def _tri_masks(n):
    r = lax.broadcasted_iota(jnp.int32, (n, n), 0)
    c = lax.broadcasted_iota(jnp.int32, (n, n), 1)
    return r >= c, r <= c


def _seg_blocks(cl, t, width):
    return (pl.BlockSpec((cl, width), lambda b: (b, 0)), pl.BlockSpec((t, width), lambda b: (b, 0)))


def _seg_blocks_t(cl, t, nrow):
    return (pl.BlockSpec((nrow, cl), lambda b: (0, b)), pl.BlockSpec((nrow, t), lambda b: (0, b)))


def _mlstm_body(qc_ref, ql_ref, gcc_ref, gcl_ref, grc_ref, grl_ref, gbc_ref, gbr_ref, on_ref, oc_ref, ol_ref,
                hfc, hbc, hfl, hbl, ktc, ktl, *, need_ctx):
    L = SCAN_CHUNK
    PW = 2 * HEAD_DIM
    lower, upper = _tri_masks(L)
    tri_lo = jnp.where(lower, 1.0, 0.0).astype(bf16)
    tri_up = jnp.where(upper, 1.0, 0.0).astype(bf16)
    lane = lax.broadcasted_iota(jnp.int32, (1, PW), 1)
    half0 = lane < HEAD_DIM
    row_half0 = lax.broadcasted_iota(jnp.int32, (PW, 1), 0) < HEAD_DIM
    rr = lax.broadcasted_iota(jnp.int32, (PW, PW), 0)
    cc = lax.broadcasted_iota(jnp.int32, (PW, PW), 1)
    blockdiag = (rr < HEAD_DIM) == (cc < HEAD_DIM)
    eye = jnp.where(rr == cc, 1.0, 0.0).astype(bf16)
    ones_bd = jnp.concatenate([jnp.where(half0, 1.0, 0.0) * jnp.ones((L, 1), f32),
                               jnp.where(half0, 0.0, 1.0) * jnp.ones((L, 1), f32)], axis=0).astype(bf16)
    ones_full = jnp.ones((L, PW), bf16)
    zero_b = jnp.zeros((), bf16)

    def transpose_keys(q_ref, kt_ref):
        for r in range(0, q_ref.shape[0], L):
            for p in range(HEADS // 2):
                k = q_ref[r:r + L, GROUP_W + p * PW:GROUP_W + (p + 1) * PW] * (HEAD_DIM ** -0.5)
                kt_ref[p * PW:(p + 1) * PW, r:r + L] = _dot_nt(eye, k).astype(bf16)

    transpose_keys(qc_ref, ktc)
    transpose_keys(ql_ref, ktl)

    def direction(q_ref, kt_ref, gc_ref, gr_ref, h_ref, j, rev, carry):
        r0 = pl.multiple_of(j * L, L)
        base = 2 * HEADS if rev else 0
        mask = upper if rev else lower
        gcol = gc_ref[pl.ds(r0, L), :] + gbc_ref[...]
        grow = gr_ref[:, pl.ds(r0, L)] + gbr_ref[...]
        bcols = _tri_dot(tri_up if rev else tri_lo, _log_sigmoid(gcol))
        lfr = _log_sigmoid(grow)
        brows = _dot_tri(lfr, tri_lo if rev else tri_up)
        r_all = grow[base:base + HEADS, :] - brows[base + HEADS:base + 2 * HEADS, :]
        wide = lambda x: jnp.broadcast_to(x, (HEADS, L))
        b_end = wide(jnp.sum(lfr[base + HEADS:base + 2 * HEADS, :], axis=1, keepdims=True))
        cs, ns, m_all = carry
        mm = jnp.maximum(m_all, wide(jnp.max(r_all, axis=1, keepdims=True)))
        a_all = jnp.exp(m_all - mm)
        w_all = jnp.exp(r_all - mm)
        new_c, new_n = [], []
        for p in range(HEADS // 2):
            qp = q_ref[pl.ds(r0, L), p * PW:(p + 1) * PW]
            kp = q_ref[pl.ds(r0, L), GROUP_W + p * PW:GROUP_W + (p + 1) * PW] * (HEAD_DIM ** -0.5)
            vp = q_ref[pl.ds(r0, L), 2 * GROUP_W + p * PW:2 * GROUP_W + (p + 1) * PW]
            ktp = kt_ref[p * PW:(p + 1) * PW, pl.ds(r0, L)]
            sb, wi, em = [], [], []
            for e in range(2):
                h = 2 * p + e
                rm = jnp.where(mask, r_all[h:h + 1, :], NEG)
                m_h = jnp.broadcast_to(m_all[h:h + 1, :], (L, L))
                c = jnp.maximum(m_h, jnp.broadcast_to(jnp.max(rm, axis=1, keepdims=True), (L, L)))
                qm = jnp.where(half0 if e == 0 else ~half0, qp, zero_b)
                sb.append((_dot_nt(qm, kp) * jnp.exp(rm - c)).astype(bf16))
                wi.append(jnp.exp(m_h - c))
                em.append(jnp.exp(-(bcols[:, base + HEADS + h:base + HEADS + h + 1] + c)))
            s2 = jnp.concatenate(sb, axis=1)
            vbd = jnp.concatenate([jnp.where(half0, vp, zero_b), jnp.where(half0, zero_b, vp)], axis=0)
            wi2 = jnp.where(half0, wi[0], wi[1])
            num = wi2 * _dot(qp, cs[p].astype(bf16)) + _dot(s2, vbd)
            den = wi2 * _dot(qp, ns[p].astype(bf16)) + _dot(s2, ones_bd)
            h_ref[pl.ds(r0, L), p * PW:(p + 1) * PW] = num / jnp.maximum(jnp.abs(den), jnp.where(half0, em[0], em[1]))
            wsel = jnp.where(row_half0, w_all[2 * p:2 * p + 1, :], w_all[2 * p + 1:2 * p + 2, :])
            asel = jnp.where(row_half0, a_all[2 * p:2 * p + 1, :], a_all[2 * p + 1:2 * p + 2, :])
            ktw = (ktp.astype(f32) * wsel).astype(bf16)
            new_c.append(asel * cs[p] + jnp.where(blockdiag, _dot(ktw, vp), 0.0))
            new_n.append(asel * ns[p] + jnp.where(blockdiag, _dot(ktw, ones_full), 0.0))
        return new_c, new_n, b_end + mm

    def segment(q_ref, kt_ref, gc_ref, gr_ref, hf_ref, hb_ref, carry):
        n = q_ref.shape[0] // L

        def body(i, carry):
            fw, bw = carry
            fw = direction(q_ref, kt_ref, gc_ref, gr_ref, hf_ref, i, False, fw)
            bw = direction(q_ref, kt_ref, gc_ref, gr_ref, hb_ref, n - 1 - i, True, bw)
            return fw, bw

        return lax.fori_loop(0, n, body, carry)

    def init():
        return ([jnp.zeros((PW, PW), f32) for _ in range(HEADS // 2)],
                [jnp.zeros((PW, PW), f32) for _ in range(HEADS // 2)],
                jnp.zeros((HEADS, L), f32))

    carry = segment(qc_ref, ktc, gcc_ref, grc_ref, hfc, hbc, (init(), init()))
    segment(ql_ref, ktl, gcl_ref, grl_ref, hfl, hbl, carry)

    gmat = _group_mean_matrix(GROUP_W, HEAD_DIM)

    def finish(q_ref, hf_ref, hb_ref, o_ref):
        blk = 256
        for r in range(0, o_ref.shape[0], blk):
            hh = hf_ref[r:r + blk, :] + hb_ref[r:r + blk, :]
            hn = hh * _group_rsqrt(hh, gmat) * on_ref[...]
            og = q_ref[r:r + blk, 3 * GROUP_W:4 * GROUP_W].astype(f32)
            o_ref[r:r + blk, :] = (_sigmoid(og) * hn).astype(o_ref.dtype)

    finish(ql_ref, hfl, hbl, ol_ref)
    if need_ctx:
        finish(qc_ref, hfc, hbc, oc_ref)
    else:
        oc_ref[...] = jnp.zeros(oc_ref.shape, oc_ref.dtype)


def _mlstm(qkvo_c, qkvo_l, mg_c, mg_l, gate_b, out_norm, batch, need_ctx):
    cl, t = qkvo_c.shape[0] // batch, qkvo_l.shape[0] // batch
    ng = 4 * HEADS
    gb_col = jnp.zeros((1, LANES), f32).at[0, :ng].set(gate_b)
    gb_row = gate_b.reshape(ng, 1)
    qc_spec, ql_spec = _seg_blocks(cl, t, 4 * GROUP_W)
    gcc_spec, gcl_spec = _seg_blocks(cl, t, LANES)
    grc_spec, grl_spec = _seg_blocks_t(cl, t, ng)
    oc_spec, ol_spec = _seg_blocks(cl, t, GROUP_W)
    return pl.pallas_call(
        functools.partial(_mlstm_body, need_ctx=need_ctx),
        grid=(batch,),
        in_specs=[qc_spec, ql_spec, gcc_spec, gcl_spec, grc_spec, grl_spec,
                  _resident((1, LANES)), _resident((ng, 1)), _resident((1, GROUP_W))],
        out_specs=[oc_spec, ol_spec],
        out_shape=[jax.ShapeDtypeStruct((batch * cl, GROUP_W), bf16),
                   jax.ShapeDtypeStruct((batch * t, GROUP_W), bf16)],
        scratch_shapes=[pltpu.VMEM((cl, GROUP_W), f32), pltpu.VMEM((cl, GROUP_W), f32),
                        pltpu.VMEM((t, GROUP_W), f32), pltpu.VMEM((t, GROUP_W), f32),
                        pltpu.VMEM((GROUP_W, cl), bf16), pltpu.VMEM((GROUP_W, t), bf16)],
        compiler_params=_params(("parallel",)),
        name="mlstm",
    )(qkvo_c, qkvo_l, mg_c, mg_l, mg_c[:, :ng].T, mg_l[:, :ng].T, gb_col, gb_row, out_norm.reshape(1, GROUP_W))


def _ssd_body(zc_ref, zl_ref, xc_ref, xl_ref, dcc_ref, dcl_ref, drc_ref, drl_ref, cw_ref, cb_ref, dbc_ref, dbr_ref,
              alc_ref, alr_ref, dsk_ref, ng_ref, oc_ref, ol_ref,
              xac, xal, yfc, ybc, yfl, ybl, xpad, btc, btl, *, need_ctx):
    L = SCAN_CHUNK
    N = SSD_STATE
    lower, upper = _tri_masks(L)
    tri_lo = jnp.where(lower, 1.0, 0.0).astype(bf16)
    tri_up = jnp.where(upper, 1.0, 0.0).astype(bf16)
    a_col = -jnp.exp(alc_ref[...])
    a_row = -jnp.exp(alr_ref[...])
    pad = 8
    half = SSD_CONV // 2

    def conv_act(x_ref, xa_ref):
        n = x_ref.shape[0]
        zeros = jnp.zeros((pad, SSD_XBC), f32)
        xpad[0:pad, :] = zeros
        xpad[pad + n:2 * pad + n, :] = zeros
        blk = 256
        for r in range(0, n, blk):
            xpad[pad + r:pad + r + blk, :] = x_ref[r:r + blk, :].astype(f32)
        for r in range(0, n, blk):
            y = jnp.zeros((blk, SSD_XBC), f32) + cb_ref[...]
            for kk in range(SSD_CONV):
                o = pad + r + kk - half
                y = y + cw_ref[kk:kk + 1, :] * xpad[o:o + blk, :]
            xa_ref[r:r + blk, :] = _silu(y)

    conv_act(xc_ref, xac)

    PW = 2 * HEAD_DIM
    lane = lax.broadcasted_iota(jnp.int32, (1, PW), 1)
    half0 = lane < HEAD_DIM
    row_half0 = lax.broadcasted_iota(jnp.int32, (PW, 1), 0) < N
    rr = lax.broadcasted_iota(jnp.int32, (PW, PW), 0)
    cc = lax.broadcasted_iota(jnp.int32, (PW, PW), 1)
    eye = jnp.where(rr == cc, 1.0, 0.0).astype(bf16)
    zero_b = jnp.zeros((), bf16)

    def transpose_b(xa_ref, bt_ref):
        for r in range(0, xa_ref.shape[0], L):
            bm = xa_ref[r:r + L, GROUP_W:GROUP_W + PW].astype(bf16)
            bt_ref[:, r:r + L] = _dot_nt(eye, bm).astype(bf16)

    transpose_b(xac, btc)

    def direction(xa_ref, bt_ref, dc_ref, dr_ref, y_ref, j, rev, states):
        r0 = pl.multiple_of(j * L, L)
        base = HEADS if rev else 0
        mask = upper if rev else lower
        dtc = _softplus(dc_ref[pl.ds(r0, L), :] + dbc_ref[...])
        dtr = _softplus(dr_ref[:, pl.ds(r0, L)] + dbr_ref[...])
        cumc = _tri_dot(tri_up if rev else tri_lo, dtc * a_col)
        ar = dtr * a_row
        cumr = _dot_tri(ar, tri_lo if rev else tri_up)
        cum_end = jnp.broadcast_to(jnp.sum(ar[base:base + HEADS, :], axis=1, keepdims=True), (HEADS, L))
        dec_all = jnp.exp(cum_end - cumr[base:base + HEADS, :]) * dtr[base:base + HEADS, :]
        aexp_all = jnp.exp(cum_end)
        bp = xa_ref[pl.ds(r0, L), GROUP_W:GROUP_W + PW].astype(bf16)
        cp = xa_ref[pl.ds(r0, L), GROUP_W + PW:GROUP_W + 2 * PW].astype(bf16)
        btp = bt_ref[:, pl.ds(r0, L)].astype(f32)
        new_states = []
        for g in range(SSD_GROUPS):
            G = _dot_nt(jnp.where(half0 if g == 0 else ~half0, cp, zero_b), bp)
            xpb = xa_ref[pl.ds(r0, L), g * PW:(g + 1) * PW].astype(bf16)
            ws, ecs, us = [], [], []
            for e in range(2):
                h = 2 * g + e
                idx = base + h
                cum_cb = jnp.broadcast_to(cumc[:, idx:idx + 1], (L, L))
                seg = jnp.exp(jnp.where(mask, cum_cb - cumr[idx:idx + 1, :], NEG))
                ws.append((G * seg * dtr[idx:idx + 1, :]).astype(bf16))
                ecs.append(jnp.exp(cum_cb))
                us.append(aexp_all[h:h + 1, :] * states[g] + _dot((btp * dec_all[h:h + 1, :]).astype(bf16), xpb))
            xbd = jnp.concatenate([jnp.where(half0, xpb, zero_b), jnp.where(half0, zero_b, xpb)], axis=0)
            y = _dot(jnp.concatenate(ws, axis=1), xbd) + jnp.where(half0, ecs[0], ecs[1]) * _dot(cp, states[g].astype(bf16))
            y_ref[pl.ds(r0, L), g * PW:(g + 1) * PW] = y
            new_states.append(jnp.where(row_half0 if g == 0 else ~row_half0, jnp.where(half0, us[0], us[1]), 0.0))
        return new_states

    def segment(xa_ref, bt_ref, dc_ref, dr_ref, yf_ref, yb_ref, carry):
        n = xa_ref.shape[0] // L

        def body(i, carry):
            fw, bw = carry
            fw = direction(xa_ref, bt_ref, dc_ref, dr_ref, yf_ref, i, False, fw)
            bw = direction(xa_ref, bt_ref, dc_ref, dr_ref, yb_ref, n - 1 - i, True, bw)
            return fw, bw

        return lax.fori_loop(0, n, body, carry)

    init = [jnp.zeros((PW, PW), f32) for _ in range(SSD_GROUPS)]
    carry = segment(xac, btc, dcc_ref, drc_ref, yfc, ybc, (init, list(init)))
    conv_act(xl_ref, xal)
    transpose_b(xal, btl)
    segment(xal, btl, dcl_ref, drl_ref, yfl, ybl, carry)

    gmat = _group_mean_matrix(GROUP_W, SSD_NORM_GROUP)

    def finish(z_ref, xa_ref, yf_ref, yb_ref, o_ref):
        blk = 256
        for r in range(0, o_ref.shape[0], blk):
            yy = yf_ref[r:r + blk, :] + yb_ref[r:r + blk, :] + dsk_ref[...] * xa_ref[r:r + blk, 0:GROUP_W]
            gt = yy * _silu(z_ref[r:r + blk, :].astype(f32))
            o_ref[r:r + blk, :] = (gt * _group_rsqrt(gt, gmat) * ng_ref[...]).astype(o_ref.dtype)

    finish(zl_ref, xal, yfl, ybl, ol_ref)
    if need_ctx:
        finish(zc_ref, xac, yfc, ybc, oc_ref)
    else:
        oc_ref[...] = jnp.zeros(oc_ref.shape, oc_ref.dtype)


def _ssd(z_c, z_l, xbc_c, xbc_l, dt_c, dt_l, conv_w, conv_b, dt_bias, a_log, d_skip, norm_g, batch, need_ctx):
    cl, t = z_c.shape[0] // batch, z_l.shape[0] // batch
    nd = 2 * HEADS
    pad_row = lambda v: jnp.zeros((1, LANES), f32).at[0, :nd].set(v)
    cw = jnp.zeros((8, SSD_XBC), f32).at[:SSD_CONV].set(conv_w)
    zc_spec, zl_spec = _seg_blocks(cl, t, GROUP_W)
    xc_spec, xl_spec = _seg_blocks(cl, t, SSD_XBC)
    dcc_spec, dcl_spec = _seg_blocks(cl, t, LANES)
    drc_spec, drl_spec = _seg_blocks_t(cl, t, nd)
    oc_spec, ol_spec = _seg_blocks(cl, t, GROUP_W)
    return pl.pallas_call(
        functools.partial(_ssd_body, need_ctx=need_ctx),
        grid=(batch,),
        in_specs=[zc_spec, zl_spec, xc_spec, xl_spec, dcc_spec, dcl_spec, drc_spec, drl_spec,
                  _resident((8, SSD_XBC)), _resident((1, SSD_XBC)), _resident((1, LANES)), _resident((nd, 1)),
                  _resident((1, LANES)), _resident((nd, 1)), _resident((1, GROUP_W)), _resident((1, GROUP_W))],
        out_specs=[oc_spec, ol_spec],
        out_shape=[jax.ShapeDtypeStruct((batch * cl, GROUP_W), bf16),
                   jax.ShapeDtypeStruct((batch * t, GROUP_W), bf16)],
        scratch_shapes=[pltpu.VMEM((cl, SSD_XBC), f32), pltpu.VMEM((t, SSD_XBC), f32),
                        pltpu.VMEM((cl, GROUP_W), f32), pltpu.VMEM((cl, GROUP_W), f32),
                        pltpu.VMEM((t, GROUP_W), f32), pltpu.VMEM((t, GROUP_W), f32),
                        pltpu.VMEM((t + 16, SSD_XBC), f32),
                        pltpu.VMEM((2 * SSD_STATE, cl), bf16), pltpu.VMEM((2 * SSD_STATE, t), bf16)],
        compiler_params=_params(("parallel",)),
        name="ssd",
    )(z_c, z_l, xbc_c, xbc_l, dt_c, dt_l, dt_c[:, :nd].T, dt_l[:, :nd].T, cw, conv_b.reshape(1, SSD_XBC),
      pad_row(dt_bias), dt_bias.reshape(nd, 1), pad_row(a_log), a_log.reshape(nd, 1),
      jnp.repeat(d_skip, HEAD_DIM).reshape(1, GROUP_W), norm_g.reshape(1, GROUP_W))


def _mla_body(ac_ref, al_ref, qn_ref, kvn_ref, wq_ref, wkv_ref, qg_ref, kg_ref, krg_ref, cos_ref, sin_ref,
              oc_ref, ol_ref, q_s, k_s, v_s, *, need_ctx):
    cl, t = ac_ref.shape[0], al_ref.shape[0]
    scale = MLA_QK ** -0.5
    rr = lax.broadcasted_iota(jnp.int32, (LANES, LANES), 0)
    cc = lax.broadcasted_iota(jnp.int32, (LANES, LANES), 1)
    in_nope = (rr < MLA_NOPE) & (cc < MLA_NOPE)
    in_rope = (rr >= MLA_NOPE) & (rr < MLA_QK) & (cc >= MLA_NOPE) & (cc < MLA_QK)
    head_gmat = jnp.where(in_nope, 1.0 / MLA_NOPE, jnp.where(in_rope, 1.0 / MLA_ROPE, 0.0)).astype(bf16)
    rot = _rotate_half_matrix(MLA_ROPE // 4)

    def head_norm(x):
        return x * _group_rsqrt(x, head_gmat)

    def project(a_ref, row0, n, roped):
        blk = 256
        for r in range(0, n, blk):
            a = a_ref[r:r + blk, :].astype(f32)
            aq, akv, akr = a[:, 0:GROUP_W], a[:, GROUP_W:GROUP_W + LANES], a[:, GROUP_W + LANES:GROUP_W + 2 * LANES]
            qh = _dot((aq * _row_rsqrt(aq, GROUP_W) * qn_ref[...]).astype(bf16), wq_ref[...])
            kv = _dot((akv * _row_rsqrt(akv, kvn_ref.shape[1]) * kvn_ref[...]).astype(bf16), wkv_ref[...])
            kr = akr * _row_rsqrt(akr, MLA_ROPE) * krg_ref[...]
            if roped:
                cos, sin = cos_ref[r:r + blk, :], sin_ref[r:r + blk, :]
                kr = _rope(kr, cos, sin, rot)
            for h in range(HEADS):
                qx = head_norm(qh[:, h * LANES:(h + 1) * LANES]) * qg_ref[...]
                if roped:
                    qx = _rope(qx, cos, sin, rot)
                q_s[h, row0 + r:row0 + r + blk, :] = (qx * scale).astype(bf16)
                kx = kv[:, h * LANES:(h + 1) * LANES]
                kx = kx * _row_rsqrt(kx, MLA_NOPE) * kg_ref[...]
                k_s[h, row0 + r:row0 + r + blk, :] = (kx + kr).astype(bf16)
            v_s[row0 + r:row0 + r + blk, :] = kv[:, HEADS * LANES:].astype(bf16)

    project(ac_ref, 0, cl, False)
    project(al_ref, cl, t, True)

    half_lane = lax.broadcasted_iota(jnp.int32, (1, LANES), 1) < HEAD_DIM

    def attend(q0, nq, nk, o_ref, o0):
        outs = []
        for pair in range(HEADS // 2):
            res = []
            for h in (2 * pair, 2 * pair + 1):
                s = _dot_nt(q_s[h, pl.ds(q0, nq), :], k_s[h, 0:nk, :])
                p = jnp.exp(s - jnp.max(s, axis=1, keepdims=True))
                den = jnp.sum(p, axis=1, keepdims=True)
                res.append(_dot(p.astype(bf16), v_s[0:nk, pair * LANES:(pair + 1) * LANES]) / den)
            outs.append(jnp.where(half_lane, res[0], res[1]))
        o_ref[pl.ds(o0, nq), :] = jnp.concatenate(outs, axis=1).astype(o_ref.dtype)

    def body(i, _):
        o0 = pl.multiple_of(i * MLA_QBLK, MLA_QBLK)
        attend(cl + o0, MLA_QBLK, cl + t, ol_ref, o0)
        return 0

    lax.fori_loop(0, t // MLA_QBLK, body, 0)
    if need_ctx:
        attend(0, cl, cl, oc_ref, 0)
    else:
        oc_ref[...] = jnp.zeros(oc_ref.shape, oc_ref.dtype)


def _mla(a_c, a_l, q_norm, kv_norm, wq_b, wkv_b, q_gain, k_gain, cos, sin, batch, need_ctx):
    cl, t = a_c.shape[0] // batch, a_l.shape[0] // batch
    kvl = kv_norm.shape[0]
    wq = jnp.zeros((GROUP_W, HEADS, LANES), f32).at[:, :, :MLA_QK].set(wq_b.reshape(GROUP_W, HEADS, MLA_QK))
    wkv4 = wkv_b.reshape(kvl, HEADS, MLA_NOPE + HEAD_DIM)
    wk = jnp.zeros((kvl, HEADS, LANES), f32).at[:, :, :MLA_NOPE].set(wkv4[:, :, :MLA_NOPE])
    wkv = jnp.concatenate([wk.reshape(kvl, HEADS * LANES), wkv4[:, :, MLA_NOPE:].reshape(kvl, GROUP_W)], axis=1)
    slab = lambda v, off: jnp.zeros((1, LANES), f32).at[0, off:off + v.shape[0]].set(v)
    qg = slab(q_gain, 0)
    kg = slab(k_gain[:MLA_NOPE], 0)
    krg = slab(k_gain[MLA_NOPE:], MLA_NOPE)
    ac_spec, al_spec = _seg_blocks(cl, t, 2 * GROUP_W)
    oc_spec, ol_spec = _seg_blocks(cl, t, GROUP_W)
    return pl.pallas_call(
        functools.partial(_mla_body, need_ctx=need_ctx),
        grid=(batch,),
        in_specs=[ac_spec, al_spec, _resident((1, GROUP_W)), _resident((1, LANES)),
                  _resident((GROUP_W, HEADS * LANES)), _resident((kvl, HEADS * LANES + GROUP_W)),
                  _resident((1, LANES)), _resident((1, LANES)), _resident((1, LANES)),
                  _resident((t, LANES)), _resident((t, LANES))],
        out_specs=[oc_spec, ol_spec],
        out_shape=[jax.ShapeDtypeStruct((batch * cl, GROUP_W), bf16),
                   jax.ShapeDtypeStruct((batch * t, GROUP_W), bf16)],
        scratch_shapes=[pltpu.VMEM((HEADS, cl + t, LANES), bf16), pltpu.VMEM((HEADS, cl + t, LANES), bf16),
                        pltpu.VMEM((cl + t, GROUP_W), bf16)],
        compiler_params=_params(("parallel",)),
        name="mla",
    )(a_c, a_l, q_norm.reshape(1, GROUP_W), kv_norm.reshape(1, kvl),
      wq.reshape(GROUP_W, HEADS * LANES).astype(bf16), wkv.astype(bf16), qg, kg, krg, cos, sin)


def _swa_body(wc_ref, wl_ref, qg_ref, kg_ref, sink_ref, cos_ref, sin_ref, oc_ref, ol_ref,
              q_s, qc_s, k_s, v_s, kc_s, vc_s, *, need_ctx):
    cl, t = wc_ref.shape[0], wl_ref.shape[0]
    scale = HEAD_DIM ** -0.5
    kvw = SWA_KV_HEADS * HEAD_DIM
    blk = SWA_BLK
    gmat_q = _group_mean_matrix(GROUP_W, HEAD_DIM)
    gmat_k = _group_mean_matrix(kvw, HEAD_DIM)
    rot = _rotate_half_matrix(HEAD_DIM // 4)

    def project(w_ref, n, roped, qdst, kdst, vdst, row0):
        step = 256
        for r in range(0, n, step):
            w = w_ref[r:r + step, :].astype(f32)
            q, k = w[:, 0:GROUP_W], w[:, GROUP_W:GROUP_W + kvw]
            q = q * _group_rsqrt(q, gmat_q) * qg_ref[...]
            k = k * _group_rsqrt(k, gmat_k) * kg_ref[...]
            if roped:
                cos, sin = cos_ref[r:r + step, :], sin_ref[r:r + step, :]
                q = jnp.concatenate([_rope(q[:, 0:LANES], cos, sin, rot),
                                     _rope(q[:, LANES:2 * LANES], cos, sin, rot)], axis=1)
                k = _rope(k, cos, sin, rot)
            qdst[r:r + step, :] = (q * scale).astype(bf16)
            kdst[row0 + r:row0 + r + step, :] = k.astype(bf16)
            vdst[row0 + r:row0 + r + step, :] = w_ref[r:r + step, GROUP_W + kvw:GROUP_W + 2 * kvw]

    zeros = jnp.zeros((blk, kvw), bf16)
    for s in (k_s, v_s):
        s[0:blk, :] = zeros
        s[blk + t:2 * blk + t, :] = zeros
    project(wc_ref, cl, False, qc_s, kc_s, vc_s, 0)
    project(wl_ref, t, True, q_s, k_s, v_s, blk)

    rr = lax.broadcasted_iota(jnp.int32, (2 * blk, 3 * blk), 0) % blk
    jj = lax.broadcasted_iota(jnp.int32, (2 * blk, 3 * blk), 1)
    band = (jj - rr >= 0) & (jj - rr <= 2 * SWA_WINDOW)
    half0 = lax.broadcasted_iota(jnp.int32, (1, LANES), 1) < HEAD_DIM
    zero_b = jnp.zeros((), bf16)

    def wide(x, n):
        return jnp.broadcast_to(x, (x.shape[0], n))

    def stacked_queries(q_ref, r0, n, kv):
        keep = half0 if kv == 0 else ~half0
        return jnp.concatenate([jnp.where(keep, q_ref[pl.ds(r0, n), 0:LANES], zero_b),
                                jnp.where(keep, q_ref[pl.ds(r0, n), LANES:2 * LANES], zero_b)], axis=0)

    def stacked_sink(n, kv):
        first = lax.broadcasted_iota(jnp.int32, (2 * n, 1), 0) < n
        return jnp.where(first, sink_ref[2 * kv:2 * kv + 1, :], sink_ref[2 * kv + 1:2 * kv + 2, :])

    def lat_block(n, _):
        r0 = pl.multiple_of(n * blk, blk)
        kpos = jj + (n - 1) * blk
        valid = band & (kpos >= 0) & (kpos < t)
        res = []
        for kv in range(SWA_KV_HEADS):
            q2 = stacked_queries(q_s, r0, blk, kv)
            s_loc = jnp.where(valid, _dot_nt(q2, k_s[pl.ds(r0, 3 * blk), :]), NEG)
            s_ctx = _dot_nt(q2, kc_s[...])
            sink = stacked_sink(blk, kv)
            m = jnp.maximum(wide(jnp.maximum(jnp.max(s_loc, axis=1, keepdims=True),
                                             jnp.max(s_ctx, axis=1, keepdims=True)), LANES), sink)
            p_loc = jnp.exp(s_loc - jnp.concatenate([m] * 3, axis=1))
            p_ctx = jnp.exp(s_ctx - jnp.concatenate([m] * (cl // LANES), axis=1))
            den = wide(jnp.sum(p_loc, axis=1, keepdims=True) + jnp.sum(p_ctx, axis=1, keepdims=True), LANES) \
                + jnp.exp(sink - m)
            o = _dot(p_loc.astype(bf16), v_s[pl.ds(r0, 3 * blk), :]) + _dot(p_ctx.astype(bf16), vc_s[...])
            res.append(o / den)
        ol_ref[pl.ds(r0, blk), :] = jnp.concatenate(
            [jnp.where(half0, res[0][0:blk], res[1][0:blk]),
             jnp.where(half0, res[0][blk:2 * blk], res[1][blk:2 * blk])], axis=1).astype(ol_ref.dtype)
        return 0

    lax.fori_loop(0, t // blk, lat_block, 0)

    if need_ctx:
        res = []
        for kv in range(SWA_KV_HEADS):
            s = _dot_nt(stacked_queries(qc_s, 0, cl, kv), kc_s[...])
            sink = stacked_sink(cl, kv)
            m = jnp.maximum(wide(jnp.max(s, axis=1, keepdims=True), LANES), sink)
            p = jnp.exp(s - jnp.concatenate([m] * (cl // LANES), axis=1))
            den = wide(jnp.sum(p, axis=1, keepdims=True), LANES) + jnp.exp(sink - m)
            res.append(_dot(p.astype(bf16), vc_s[...]) / den)
        oc_ref[...] = jnp.concatenate(
            [jnp.where(half0, res[0][0:cl], res[1][0:cl]),
             jnp.where(half0, res[0][cl:2 * cl], res[1][cl:2 * cl])], axis=1).astype(oc_ref.dtype)
    else:
        oc_ref[...] = jnp.zeros(oc_ref.shape, oc_ref.dtype)


def _swa_head_order(a, axis):
    blocks = jnp.split(a, HEADS, axis=axis)
    return jnp.concatenate([blocks[0], blocks[2], blocks[1], blocks[3]], axis=axis)


def _swa(w_c, w_l, q_gain, k_gain, sink, cos, sin, batch, need_ctx):
    cl, t = w_c.shape[0] // batch, w_l.shape[0] // batch
    kvw = SWA_KV_HEADS * HEAD_DIM
    wc_spec, wl_spec = _seg_blocks(cl, t, 2 * GROUP_W)
    oc_spec, ol_spec = _seg_blocks(cl, t, GROUP_W)
    return pl.pallas_call(
        functools.partial(_swa_body, need_ctx=need_ctx),
        grid=(batch,),
        in_specs=[wc_spec, wl_spec, _resident((1, GROUP_W)), _resident((1, kvw)), _resident((HEADS, LANES)),
                  _resident((t, LANES)), _resident((t, LANES))],
        out_specs=[oc_spec, ol_spec],
        out_shape=[jax.ShapeDtypeStruct((batch * cl, GROUP_W), bf16),
                   jax.ShapeDtypeStruct((batch * t, GROUP_W), bf16)],
        scratch_shapes=[pltpu.VMEM((t, GROUP_W), bf16), pltpu.VMEM((cl, GROUP_W), bf16),
                        pltpu.VMEM((t + 2 * SWA_BLK, kvw), bf16), pltpu.VMEM((t + 2 * SWA_BLK, kvw), bf16),
                        pltpu.VMEM((cl, kvw), bf16), pltpu.VMEM((cl, kvw), bf16)],
        compiler_params=_params(("parallel",)),
        name="swa",
    )(w_c, w_l, jnp.tile(q_gain, HEADS).reshape(1, GROUP_W), jnp.tile(k_gain, SWA_KV_HEADS).reshape(1, kvw),
      jnp.broadcast_to(sink.reshape(HEADS, 1), (HEADS, LANES)), cos, sin)


def _rope_tables(t, rot_dim, lane0):
    pos = jnp.arange(t)
    row, col = (pos // GRID_W).astype(f32), (pos % GRID_W).astype(f32)
    nf = rot_dim // 4
    inv = ROPE_BASE ** (-jnp.arange(nf, dtype=f32) / nf)
    ar, ac = row[:, None] * inv, col[:, None] * inv
    ang = jnp.concatenate([ar, ar, ac, ac], axis=-1)
    return jnp.cos(ang), jnp.sin(ang)


def _pack_in_weight(w_in):
    o = 0
    seg = {}
    for name, n in (("qkvo", 4 * GROUP_W), ("mg", 4 * HEADS), ("a_q", GROUP_W), ("a_kv", GROUP_W // 2),
                    ("a_kr", MLA_ROPE), ("swa", 2 * GROUP_W), ("z", GROUP_W), ("xbc", SSD_XBC), ("dt", 2 * HEADS)):
        seg[name] = w_in[:, o:o + n]
        o += n
    d = w_in.shape[0]
    zeros = lambda n: jnp.zeros((d, n), w_in.dtype)
    packed = jnp.concatenate([
        seg["qkvo"], seg["mg"], zeros(LANES - 4 * HEADS),
        seg["a_q"], seg["a_kv"], zeros(MLA_NOPE), seg["a_kr"], zeros(LANES - MLA_QK),
        _swa_head_order(seg["swa"][:, :GROUP_W], 1), seg["swa"][:, GROUP_W:],
        seg["z"], seg["xbc"], seg["dt"], zeros(LANES - 2 * HEADS)], axis=1)
    return packed.astype(bf16)


def kernel(x, c, ctx, c_ctx, w_mod, b_mod, ffn1_norm, ffn1_wi, ffn1_wo, mix_norm, w_in, w_out, mlstm_gate_b, mlstm_out_norm, mla_q_norm, mla_kv_norm, mla_wq_b, mla_wkv_b, mla_q_gain, mla_k_gain, swa_q_gain, swa_k_gain, swa_sink, ssd_conv_w, ssd_conv_b, ssd_dt_bias, ssd_a_log, ssd_d, ssd_norm, ffn2_norm, ffn2_wi, ffn2_wo):
    b, t, d = x.shape
    cl = ctx.shape[1]
    depth = w_mod.shape[0]
    dff = ffn1_wo.shape[1]
    lat_tiles = t // ROW_TILE

    cos_m, sin_m = _rope_tables(t, MLA_ROPE, MLA_NOPE)
    pad_id = lambda tab, fill: jnp.concatenate(
        [jnp.full((t, MLA_NOPE), fill, f32), tab, jnp.full((t, LANES - MLA_QK), fill, f32)], axis=1)
    cos_m, sin_m = pad_id(cos_m, 1.0), pad_id(sin_m, 0.0)
    cos_s, sin_s = _rope_tables(t, HEAD_DIM, 0)
    cos_s, sin_s = jnp.tile(cos_s, (1, LANES // HEAD_DIM)), jnp.tile(sin_s, (1, LANES // HEAD_DIM))

    h = x.reshape(b * t, d)
    hc = ctx.reshape(b * cl, d)
    cc = jnp.concatenate([c, c_ctx[None, :]], axis=0)
    for l in range(depth):
        need_ctx = l < depth - 1
        mod = _modulation(cc, w_mod[l], b_mod[l]).reshape(b + 1, N_MOD, d)
        streams = ((0, lat_tiles), (b, None))

        def ffn(v, s, norm, wi, wo, mi):
            return _ffn(v, mod, norm.reshape(1, d), wi[:, :dff].astype(bf16), wi[:, dff:].astype(bf16),
                        wo.astype(bf16), mi, *s)

        h = ffn(h, streams[0], ffn1_norm[l], ffn1_wi[l], ffn1_wo[l], 0)
        hc = ffn(hc, streams[1], ffn1_norm[l], ffn1_wi[l], ffn1_wo[l], 0)
        w_packed = _pack_in_weight(w_in[l])
        qkvo_l, mg_l, mla_l, swa_l, z_l, xbc_l, dt_l = _inproj(h, mod, mix_norm[l].reshape(1, d), w_packed, *streams[0])
        qkvo_c, mg_c, mla_c, swa_c, z_c, xbc_c, dt_c = _inproj(hc, mod, mix_norm[l].reshape(1, d), w_packed, *streams[1])
        a_c, a_l = _mlstm(qkvo_c, qkvo_l, mg_c, mg_l, mlstm_gate_b[l], mlstm_out_norm[l], b, need_ctx)
        m_c, m_l = _mla(mla_c, mla_l, mla_q_norm[l], mla_kv_norm[l], mla_wq_b[l], mla_wkv_b[l], mla_q_gain[l],
                        mla_k_gain[l], cos_m, sin_m, b, need_ctx)
        s_c, s_l = _swa(swa_c, swa_l, swa_q_gain[l], swa_k_gain[l], swa_sink[l], cos_s, sin_s, b, need_ctx)
        d_c, d_l = _ssd(z_c, z_l, xbc_c, xbc_l, dt_c, dt_l, ssd_conv_w[l], ssd_conv_b[l], ssd_dt_bias[l],
                        ssd_a_log[l], ssd_d[l], ssd_norm[l], b, need_ctx)
        wo_b = jnp.concatenate([w_out[l][:2 * GROUP_W], _swa_head_order(w_out[l][2 * GROUP_W:3 * GROUP_W], 0),
                                w_out[l][3 * GROUP_W:]], axis=0).astype(bf16)
        h = _outproj(h, mod, (a_l, m_l, s_l, d_l), wo_b, *streams[0])
        h = ffn(h, streams[0], ffn2_norm[l], ffn2_wi[l], ffn2_wo[l], 6)
        if need_ctx:
            hc = _outproj(hc, mod, (a_c, m_c, s_c, d_c), wo_b, *streams[1])
            hc = ffn(hc, streams[1], ffn2_norm[l], ffn2_wi[l], ffn2_wo[l], 6)
    return h.reshape(b, t, d)
```

```python
import functools

import jax
import jax.numpy as jnp
from jax import lax
from jax.experimental import pallas as pl
from jax.experimental.pallas import tpu as pltpu

f32 = jnp.float32
bf16 = jnp.bfloat16

RMS_EPS = 1e-6
ROPE_BASE = 10000.0
GRID_W = 64
N_MOD = 9
HEADS = 4
HEAD_DIM = 64
GROUP_W = HEADS * HEAD_DIM
MLA_NOPE = 64
MLA_ROPE = 32
MLA_QK = MLA_NOPE + MLA_ROPE
SWA_KV_HEADS = 2
SWA_WINDOW = 128
SWA_BLK = 128
SSD_STATE = 64
SSD_GROUPS = 2
SSD_CONV = 5
SSD_XBC = GROUP_W + 2 * SSD_GROUPS * SSD_STATE
SSD_NORM_GROUP = 128

LANES = 128
ROW_TILE = 512
FF_CHUNK = 256
SCAN_CHUNK = 128
SCAN_UNROLL = 1
SCAN_BATCH = 2
MLA_QBLK = 512
NEG = -1e30
VMEM_LIMIT = 56 * 1024 * 1024

P_QKVO = 0
P_MG = 1024
P_MLA = 1152
P_SWA = 1664
P_Z = 2176
P_XBC = 2432
P_DT = 2944
P_COLS = 3072


def _dot(a, b):
    return jnp.dot(a, b, preferred_element_type=f32)


def _dot_nt(a, b):
    return lax.dot_general(a, b, (((1,), (1,)), ((), ())), preferred_element_type=f32)


def _sigmoid(x):
    return 1.0 / (1.0 + jnp.exp(-x))


def _silu(x):
    return x * _sigmoid(x)


def _softplus(x):
    return jnp.maximum(x, 0.0) + jnp.log(1.0 + jnp.exp(-jnp.abs(x)))


def _log_sigmoid(x):
    return -_softplus(-x)


def _split3(x):
    hi = x.astype(bf16)
    r = x - hi.astype(f32)
    mid = r.astype(bf16)
    lo = (r - mid.astype(f32)).astype(bf16)
    return hi, mid, lo


def _dot_tri(x, tri):
    m = x.shape[0]
    terms = jnp.concatenate([p.astype(f32) for p in _split3(x)], axis=0).astype(bf16)
    y = _dot(terms, tri)
    return y[0:m] + y[m:2 * m] + y[2 * m:3 * m]


def _dot2(x, w):
    hi = x.astype(bf16)
    mid = (x - hi.astype(f32)).astype(bf16)
    return _dot(hi, w) + _dot(mid, w)


def _group_mean_matrix(n, gsz):
    r = lax.broadcasted_iota(jnp.int32, (n, n), 0) // gsz
    c = lax.broadcasted_iota(jnp.int32, (n, n), 1) // gsz
    return jnp.where(r == c, 1.0 / gsz, 0.0).astype(bf16)


def _group_rsqrt(x, gmat):
    return lax.rsqrt(_dot2(x * x, gmat) + RMS_EPS)


def _row_rsqrt(x, n_real):
    return lax.rsqrt(jnp.sum(x * x, axis=1, keepdims=True) * (1.0 / n_real) + RMS_EPS)


def _rotate_half_matrix(quarter):
    r = lax.broadcasted_iota(jnp.int32, (LANES, LANES), 0)
    c = lax.broadcasted_iota(jnp.int32, (LANES, LANES), 1)
    first = (c % (2 * quarter)) < quarter
    return jnp.where(first & (r == c + quarter), -1.0, jnp.where(~first & (r == c - quarter), 1.0, 0.0)).astype(bf16)


def _rope(x, cos, sin, rot):
    return x * cos + _dot2(x, rot) * sin


def _resident(shape):
    nd = len(shape)
    return pl.BlockSpec(shape, lambda *_: (0,) * nd, pipeline_mode=pl.Buffered(1))


def _params(sem):
    return pltpu.CompilerParams(dimension_semantics=sem, vmem_limit_bytes=VMEM_LIMIT)


def _mod_body(c_ref, w_ref, b_ref, o_ref):
    s = _silu(c_ref[...])
    o_ref[...] = jnp.dot(s, w_ref[...], preferred_element_type=f32,
                         precision=lax.Precision.HIGHEST) + b_ref[...]


def _modulation(cc, w, b):
    m, d = cc.shape
    n = w.shape[1]
    tn = 1024
    return pl.pallas_call(
        _mod_body,
        grid=(n // tn,),
        in_specs=[pl.BlockSpec((m, d), lambda j: (0, 0)),
                  pl.BlockSpec((d, tn), lambda j: (0, j)),
                  pl.BlockSpec((1, tn), lambda j: (0, j))],
        out_specs=pl.BlockSpec((m, tn), lambda j: (0, j)),
        out_shape=jax.ShapeDtypeStruct((m, n), f32),
        compiler_params=_params(("arbitrary",)),
        name="modulation",
    )(cc, w, b.reshape(1, n))


def _norm_mod(x, gain, shift, scale):
    xn = x * _row_rsqrt(x, x.shape[1]) * gain
    return (xn * (1.0 + scale) + shift).astype(bf16)


def _swiglu_half_step(x, mod_ref, g_ref, wig_ref, wiu_ref, wo_ref, mi):
    xb = _norm_mod(x, g_ref[...], mod_ref[mi:mi + 1, :], mod_ref[mi + 1:mi + 2, :])
    acc = jnp.zeros(x.shape, f32)
    for c in range(wig_ref.shape[1] // FF_CHUNK):
        sl = slice(c * FF_CHUNK, (c + 1) * FF_CHUNK)
        g = _dot(xb, wig_ref[:, sl])
        u = _dot(xb, wiu_ref[:, sl])
        acc = acc + _dot((_silu(g) * u).astype(bf16), wo_ref[sl, :])
    return x + (0.5 * mod_ref[mi + 2:mi + 3, :]) * acc


def _ffn_body(x_ref, mod_ref, g_ref, wig_ref, wiu_ref, wo_ref, o_ref, *, mi):
    o_ref[...] = _swiglu_half_step(x_ref[...], mod_ref, g_ref, wig_ref, wiu_ref, wo_ref, mi)


def _mix_ffn_body(x_ref, mod_ref, a_ref, m_ref, w_ref, s_ref, wout_ref, g_ref, wig_ref, wiu_ref, wo_ref, o_ref):
    acc = _dot(a_ref[...], wout_ref[0:GROUP_W, :])
    acc = acc + _dot(m_ref[...], wout_ref[GROUP_W:2 * GROUP_W, :])
    acc = acc + _dot(w_ref[...], wout_ref[2 * GROUP_W:3 * GROUP_W, :])
    acc = acc + _dot(s_ref[...], wout_ref[3 * GROUP_W:4 * GROUP_W, :])
    x = x_ref[...] + mod_ref[5:6, :] * acc
    o_ref[...] = _swiglu_half_step(x, mod_ref, g_ref, wig_ref, wiu_ref, wo_ref, 6)


def _mod_spec(d, mod_base, tiles_per_mod):
    if tiles_per_mod is None:
        return pl.BlockSpec((None, N_MOD, d), lambda i: (mod_base, 0, 0))
    return pl.BlockSpec((None, N_MOD, d), lambda i: (mod_base + i // tiles_per_mod, 0, 0))


def _ffn(x, mod, gain, wig, wiu, wo, mi, mod_base, tiles_per_mod):
    rows, d = x.shape
    dff = wig.shape[1]
    return pl.pallas_call(
        functools.partial(_ffn_body, mi=mi),
        grid=(rows // ROW_TILE,),
        in_specs=[pl.BlockSpec((ROW_TILE, d), lambda i: (i, 0)),
                  _mod_spec(d, mod_base, tiles_per_mod),
                  _resident((1, d)), _resident((d, dff)), _resident((d, dff)), _resident((dff, d))],
        out_specs=pl.BlockSpec((ROW_TILE, d), lambda i: (i, 0)),
        out_shape=jax.ShapeDtypeStruct((rows, d), f32),
        compiler_params=_params(("parallel",)),
        name="ffn",
    )(x, mod, gain, wig, wiu, wo)


def _inproj_body(x_ref, mod_ref, g_ref, w_ref, qkvo_ref, mg_ref, mla_ref, swa_ref, z_ref, xbc_ref, dt_ref):
    xb = _norm_mod(x_ref[...], g_ref[...], mod_ref[3:4, :], mod_ref[4:5, :])
    qkvo_ref[...] = _dot(xb, w_ref[:, P_QKVO:P_MG]).astype(bf16)
    mg_ref[...] = _dot(xb, w_ref[:, P_MG:P_MLA])
    mla_ref[...] = _dot(xb, w_ref[:, P_MLA:P_SWA]).astype(bf16)
    swa_ref[...] = _dot(xb, w_ref[:, P_SWA:P_Z]).astype(bf16)
    z_ref[...] = _dot(xb, w_ref[:, P_Z:P_XBC]).astype(bf16)
    xbc_ref[...] = _dot(xb, w_ref[:, P_XBC:P_DT]).astype(bf16)
    dt_ref[...] = _dot(xb, w_ref[:, P_DT:P_COLS])


def _inproj(x, mod, gain, w, mod_base, tiles_per_mod):
    rows, d = x.shape
    widths = [(P_MG - P_QKVO, bf16), (P_MLA - P_MG, f32), (P_SWA - P_MLA, bf16), (P_Z - P_SWA, bf16),
              (P_XBC - P_Z, bf16), (P_DT - P_XBC, bf16), (P_COLS - P_DT, f32)]
    return pl.pallas_call(
        _inproj_body,
        grid=(rows // ROW_TILE,),
        in_specs=[pl.BlockSpec((ROW_TILE, d), lambda i: (i, 0)),
                  _mod_spec(d, mod_base, tiles_per_mod),
                  _resident((1, d)), _resident((d, P_COLS))],
        out_specs=[pl.BlockSpec((ROW_TILE, n), lambda i: (i, 0)) for n, _ in widths],
        out_shape=[jax.ShapeDtypeStruct((rows, n), dt) for n, dt in widths],
        compiler_params=_params(("parallel",)),
        name="inproj",
    )(x, mod, gain, w)


def _mix_ffn(x, mod, mixed, w_out, gain, wig, wiu, wo, mod_base, tiles_per_mod):
    rows, d = x.shape
    dff = wig.shape[1]
    return pl.pallas_call(
        _mix_ffn_body,
        grid=(rows // ROW_TILE,),
        in_specs=[pl.BlockSpec((ROW_TILE, d), lambda i: (i, 0)),
                  _mod_spec(d, mod_base, tiles_per_mod)]
                 + [pl.BlockSpec((ROW_TILE, GROUP_W), lambda i: (i, 0))] * 4
                 + [_resident(w_out.shape), _resident((1, d)), _resident((d, dff)), _resident((d, dff)),
                    _resident((dff, d))],
        out_specs=pl.BlockSpec((ROW_TILE, d), lambda i: (i, 0)),
        out_shape=jax.ShapeDtypeStruct((rows, d), f32),
        compiler_params=_params(("parallel",)),
        name="mix_ffn",
    )(x, mod, *mixed, w_out, gain, wig, wiu, wo)


def _tri_masks(n):
    r = lax.broadcasted_iota(jnp.int32, (n, n), 0)
    c = lax.broadcasted_iota(jnp.int32, (n, n), 1)
    return r >= c, r <= c


def _seg_blocks(cl, t, width):
    return (pl.BlockSpec((cl, width), lambda b: (b, 0)), pl.BlockSpec((t, width), lambda b: (b, 0)))


def _seg_blocks_t(cl, t, nrow):
    return (pl.BlockSpec((nrow, cl), lambda b: (0, b)), pl.BlockSpec((nrow, t), lambda b: (0, b)))


def _mlstm_body(qc_ref, ql_ref, grc_ref, grl_ref, gbr_ref, on_ref, oc_ref, ol_ref,
                hsc, hsl, ktc, ktl, *, need_ctx):
    L = SCAN_CHUNK
    PW = 2 * HEAD_DIM
    lower, upper = _tri_masks(L)
    tri_lo = jnp.where(lower, 1.0, 0.0).astype(bf16)
    tri_up = jnp.where(upper, 1.0, 0.0).astype(bf16)
    lane = lax.broadcasted_iota(jnp.int32, (1, PW), 1)
    half0 = lane < HEAD_DIM
    row_half0 = lax.broadcasted_iota(jnp.int32, (PW, 1), 0) < HEAD_DIM
    rr = lax.broadcasted_iota(jnp.int32, (PW, PW), 0)
    cc = lax.broadcasted_iota(jnp.int32, (PW, PW), 1)
    rr2 = lax.broadcasted_iota(jnp.int32, (PW, 2 * PW), 0)
    cc2 = lax.broadcasted_iota(jnp.int32, (PW, 2 * PW), 1) % PW
    blockdiag2 = (rr2 < HEAD_DIM) == (cc2 < HEAD_DIM)
    eye = jnp.where(rr == cc, 1.0, 0.0).astype(bf16)
    ones_bd = jnp.concatenate([jnp.where(half0, 1.0, 0.0) * jnp.ones((L, 1), f32),
                               jnp.where(half0, 0.0, 1.0) * jnp.ones((L, 1), f32)], axis=0).astype(bf16)
    ones_full = jnp.ones((L, PW), bf16)
    zero_b = jnp.zeros((), bf16)

    def transpose_keys(q_ref, kt_ref):
        for r in range(0, q_ref.shape[0], L):
            for p in range(HEADS // 2):
                k = q_ref[r:r + L, GROUP_W + p * PW:GROUP_W + (p + 1) * PW] * (HEAD_DIM ** -0.5)
                kt_ref[p * PW:(p + 1) * PW, r:r + L] = _dot_nt(eye, k).astype(bf16)

    transpose_keys(qc_ref, ktc)
    transpose_keys(ql_ref, ktl)

    def direction(q_ref, kt_ref, gr_ref, j, rev, carry, off):
        r0 = pl.multiple_of(off + j * L, L)
        base = 2 * HEADS if rev else 0
        mask = upper if rev else lower
        grow = gr_ref[:, pl.ds(r0, L)] + gbr_ref[...]
        lfr = _log_sigmoid(grow)
        brows = _dot_tri(lfr, tri_lo if rev else tri_up)
        bcols = jnp.transpose(brows)
        r_all = grow[base:base + HEADS, :] - brows[base + HEADS:base + 2 * HEADS, :]
        wide = lambda x: jnp.broadcast_to(x, (HEADS, L))
        b_end = wide(jnp.sum(lfr[base + HEADS:base + 2 * HEADS, :], axis=1, keepdims=True))
        cns, m_all = carry
        mm = jnp.maximum(m_all, wide(jnp.max(r_all, axis=1, keepdims=True)))
        a_all = jnp.exp(m_all - mm)
        w_all = jnp.exp(r_all - mm)
        twice = lambda x: jnp.concatenate([x, x], axis=1)
        new_cn, outs = [], []
        for p in range(HEADS // 2):
            qp = q_ref[pl.ds(r0, L), p * PW:(p + 1) * PW]
            kp = q_ref[pl.ds(r0, L), GROUP_W + p * PW:GROUP_W + (p + 1) * PW] * (HEAD_DIM ** -0.5)
            vp = q_ref[pl.ds(r0, L), 2 * GROUP_W + p * PW:2 * GROUP_W + (p + 1) * PW]
            ktp = kt_ref[p * PW:(p + 1) * PW, pl.ds(r0, L)]
            qk = _dot_nt(jnp.concatenate([jnp.where(half0, qp, zero_b), jnp.where(half0, zero_b, qp)], axis=0), kp)
            sb, wi, em = [], [], []
            for e in range(2):
                h = 2 * p + e
                rm = jnp.where(mask, r_all[h:h + 1, :], NEG)
                m_h = jnp.broadcast_to(m_all[h:h + 1, :], (L, L))
                c = jnp.maximum(m_h, jnp.broadcast_to(jnp.max(rm, axis=1, keepdims=True), (L, L)))
                sb.append((qk[e * L:(e + 1) * L] * jnp.exp(rm - c)).astype(bf16))
                wi.append(jnp.exp(m_h - c))
                bcol = bcols[:, base + HEADS + h:base + HEADS + h + 1]
                em.append(jnp.exp(-(jnp.broadcast_to(bcol, (L, L)) + c)))
            s2 = jnp.concatenate(sb, axis=1)
            vbd = jnp.concatenate([jnp.where(half0, vp, zero_b), jnp.where(half0, zero_b, vp)], axis=0)
            hx = twice(jnp.where(half0, wi[0], wi[1])) * _dot(qp, cns[p].astype(bf16)) \
                + _dot(s2, jnp.concatenate([vbd, ones_bd], axis=1))
            den = jnp.maximum(jnp.abs(hx[:, PW:2 * PW]), jnp.where(half0, em[0], em[1]))
            outs.append(hx[:, 0:PW] / den)
            wsel = jnp.where(row_half0, w_all[2 * p:2 * p + 1, :], w_all[2 * p + 1:2 * p + 2, :])
            asel = jnp.where(row_half0, a_all[2 * p:2 * p + 1, :], a_all[2 * p + 1:2 * p + 2, :])
            ktw = (ktp.astype(f32) * wsel).astype(bf16)
            upd = _dot(ktw, jnp.concatenate([vp, ones_full], axis=1))
            new_cn.append(twice(asel) * cns[p] + jnp.where(blockdiag2, upd, 0.0))
        return (new_cn, b_end + mm), jnp.concatenate(outs, axis=1)

    def segment(q_ref, kt_ref, gr_ref, hs_ref, carry):
        seg = q_ref.shape[0] // SCAN_BATCH
        n = seg // L

        def body(i, carry):
            out, pieces = [], []
            for g, (fw, bw) in enumerate(carry):
                fw, h_f = direction(q_ref, kt_ref, gr_ref, i, False, fw, g * seg)
                bw, h_b = direction(q_ref, kt_ref, gr_ref, n - 1 - i, True, bw, g * seg)
                out.append((fw, bw))
                pieces += [h_f, h_b]
            hs_ref[i] = jnp.concatenate(pieces, axis=1)
            return tuple(out)

        return lax.fori_loop(0, n, body, carry, unroll=SCAN_UNROLL)

    def init():
        return ([jnp.zeros((PW, 2 * PW), f32) for _ in range(HEADS // 2)], jnp.zeros((HEADS, L), f32))

    carry = segment(qc_ref, ktc, grc_ref, hsc, tuple((init(), init()) for _ in range(SCAN_BATCH)))
    segment(ql_ref, ktl, grl_ref, hsl, carry)

    gmat = _group_mean_matrix(GROUP_W, HEAD_DIM)

    def finish(q_ref, hs_ref, o_ref):
        seg = o_ref.shape[0] // SCAN_BATCH
        n = seg // L
        for g in range(SCAN_BATCH):
            for j in range(n):
                r = g * seg + j * L
                c0 = g * 2 * GROUP_W
                hh = hs_ref[j, :, c0:c0 + GROUP_W] + hs_ref[n - 1 - j, :, c0 + GROUP_W:c0 + 2 * GROUP_W]
                hn = hh * _group_rsqrt(hh, gmat) * on_ref[...]
                og = q_ref[r:r + L, 3 * GROUP_W:4 * GROUP_W].astype(f32)
                o_ref[r:r + L, :] = (_sigmoid(og) * hn).astype(o_ref.dtype)

    finish(ql_ref, hsl, ol_ref)
    if need_ctx:
        finish(qc_ref, hsc, oc_ref)
    else:
        oc_ref[...] = jnp.zeros(oc_ref.shape, oc_ref.dtype)


def _mlstm(qkvo_c, qkvo_l, mg_c, mg_l, gate_b, out_norm, batch, need_ctx):
    cl, t = qkvo_c.shape[0] // batch, qkvo_l.shape[0] // batch
    ng = 4 * HEADS
    bc, bt = SCAN_BATCH * cl, SCAN_BATCH * t
    gb_row = gate_b.reshape(ng, 1)
    qc_spec, ql_spec = _seg_blocks(bc, bt, 4 * GROUP_W)
    grc_spec, grl_spec = _seg_blocks_t(bc, bt, ng)
    oc_spec, ol_spec = _seg_blocks(bc, bt, GROUP_W)
    return pl.pallas_call(
        functools.partial(_mlstm_body, need_ctx=need_ctx),
        grid=(batch // SCAN_BATCH,),
        in_specs=[qc_spec, ql_spec, grc_spec, grl_spec, _resident((ng, 1)), _resident((1, GROUP_W))],
        out_specs=[oc_spec, ol_spec],
        out_shape=[jax.ShapeDtypeStruct((batch * cl, GROUP_W), bf16),
                   jax.ShapeDtypeStruct((batch * t, GROUP_W), bf16)],
        scratch_shapes=[pltpu.VMEM((cl // SCAN_CHUNK, SCAN_CHUNK, 2 * SCAN_BATCH * GROUP_W), f32),
                        pltpu.VMEM((t // SCAN_CHUNK, SCAN_CHUNK, 2 * SCAN_BATCH * GROUP_W), f32),
                        pltpu.VMEM((GROUP_W, bc), bf16), pltpu.VMEM((GROUP_W, bt), bf16)],
        compiler_params=_params(("parallel",)),
        name="mlstm",
    )(qkvo_c, qkvo_l, mg_c[:, :ng].T, mg_l[:, :ng].T, gb_row, out_norm.reshape(1, GROUP_W))


def _ssd_body(zc_ref, zl_ref, xc_ref, xl_ref, drc_ref, drl_ref, cw_ref, cb_ref, dbr_ref,
              alr_ref, dsk_ref, ng_ref, oc_ref, ol_ref,
              xac, xal, ysc, ysl, xpad, btc, btl, *, need_ctx):
    L = SCAN_CHUNK
    N = SSD_STATE
    lower, upper = _tri_masks(L)
    tri_lo = jnp.where(lower, 1.0, 0.0).astype(bf16)
    tri_up = jnp.where(upper, 1.0, 0.0).astype(bf16)
    a_row = -jnp.exp(alr_ref[...])
    pad = 8
    half = SSD_CONV // 2

    def conv_act(x_ref, xa_ref):
        n = x_ref.shape[0] // SCAN_BATCH
        zeros = jnp.zeros((pad, SSD_XBC), f32)
        blk = 256
        for g in range(SCAN_BATCH):
            xpad[0:pad, :] = zeros
            xpad[pad + n:2 * pad + n, :] = zeros
            for r in range(0, n, blk):
                xpad[pad + r:pad + r + blk, :] = x_ref[g * n + r:g * n + r + blk, :].astype(f32)
            for r in range(0, n, blk):
                y = jnp.zeros((blk, SSD_XBC), f32) + cb_ref[...]
                for kk in range(SSD_CONV):
                    o = pad + r + kk - half
                    y = y + cw_ref[kk:kk + 1, :] * xpad[o:o + blk, :]
                xa_ref[g * n + r:g * n + r + blk, :] = _silu(y)

    conv_act(xc_ref, xac)

    PW = 2 * HEAD_DIM
    lane = lax.broadcasted_iota(jnp.int32, (1, PW), 1)
    half0 = lane < HEAD_DIM
    row_half0 = lax.broadcasted_iota(jnp.int32, (PW, 1), 0) < N
    rr = lax.broadcasted_iota(jnp.int32, (PW, PW), 0)
    cc = lax.broadcasted_iota(jnp.int32, (PW, PW), 1)
    eye = jnp.where(rr == cc, 1.0, 0.0).astype(bf16)
    zero_b = jnp.zeros((), bf16)

    def transpose_b(xa_ref, bt_ref):
        for r in range(0, xa_ref.shape[0], L):
            bm = xa_ref[r:r + L, GROUP_W:GROUP_W + PW].astype(bf16)
            bt_ref[:, r:r + L] = _dot_nt(eye, bm).astype(bf16)

    transpose_b(xac, btc)

    def direction(xa_ref, bt_ref, dr_ref, j, rev, states, off):
        r0 = pl.multiple_of(off + j * L, L)
        base = HEADS if rev else 0
        mask = upper if rev else lower
        dtr = _softplus(dr_ref[:, pl.ds(r0, L)] + dbr_ref[...])
        ar = dtr * a_row
        cumr = _dot_tri(ar, tri_lo if rev else tri_up)
        cumc = jnp.transpose(cumr)
        cum_end = jnp.broadcast_to(jnp.sum(ar[base:base + HEADS, :], axis=1, keepdims=True), (HEADS, L))
        dec_all = jnp.exp(cum_end - cumr[base:base + HEADS, :]) * dtr[base:base + HEADS, :]
        aexp_all = jnp.exp(cum_end)
        bp = xa_ref[pl.ds(r0, L), GROUP_W:GROUP_W + PW].astype(bf16)
        cp = xa_ref[pl.ds(r0, L), GROUP_W + PW:GROUP_W + 2 * PW].astype(bf16)
        btp = bt_ref[:, pl.ds(r0, L)].astype(f32)
        new_states, outs = [], []
        for g in range(SSD_GROUPS):
            G = _dot_nt(jnp.where(half0 if g == 0 else ~half0, cp, zero_b), bp)
            xpb = xa_ref[pl.ds(r0, L), g * PW:(g + 1) * PW].astype(bf16)
            uu = _dot(jnp.concatenate([btp * dec_all[2 * g:2 * g + 1, :], btp * dec_all[2 * g + 1:2 * g + 2, :]],
                                      axis=0).astype(bf16), xpb)
            ws, ecs, us = [], [], []
            for e in range(2):
                h = 2 * g + e
                idx = base + h
                cum_cb = jnp.broadcast_to(cumc[:, idx:idx + 1], (L, L))
                seg = jnp.exp(jnp.where(mask, cum_cb - cumr[idx:idx + 1, :], NEG))
                ws.append((G * seg * dtr[idx:idx + 1, :]).astype(bf16))
                ecs.append(jnp.exp(cum_cb))
                us.append(aexp_all[h:h + 1, :] * states[g] + uu[e * PW:(e + 1) * PW])
            xbd = jnp.concatenate([jnp.where(half0, xpb, zero_b), jnp.where(half0, zero_b, xpb)], axis=0)
            y = _dot(jnp.concatenate(ws, axis=1), xbd) + jnp.where(half0, ecs[0], ecs[1]) * _dot(cp, states[g].astype(bf16))
            outs.append(y)
            new_states.append(jnp.where(row_half0 if g == 0 else ~row_half0, jnp.where(half0, us[0], us[1]), 0.0))
        return new_states, jnp.concatenate(outs, axis=1)

    def segment(xa_ref, bt_ref, dr_ref, ys_ref, carry):
        seg = xa_ref.shape[0] // SCAN_BATCH
        n = seg // L

        def body(i, carry):
            out, pieces = [], []
            for g, (fw, bw) in enumerate(carry):
                fw, y_f = direction(xa_ref, bt_ref, dr_ref, i, False, fw, g * seg)
                bw, y_b = direction(xa_ref, bt_ref, dr_ref, n - 1 - i, True, bw, g * seg)
                out.append((fw, bw))
                pieces += [y_f, y_b]
            ys_ref[i] = jnp.concatenate(pieces, axis=1)
            return tuple(out)

        return lax.fori_loop(0, n, body, carry, unroll=SCAN_UNROLL)

    init = [jnp.zeros((PW, PW), f32) for _ in range(SSD_GROUPS)]
    carry = segment(xac, btc, drc_ref, ysc, tuple((init, list(init)) for _ in range(SCAN_BATCH)))
    conv_act(xl_ref, xal)
    transpose_b(xal, btl)
    segment(xal, btl, drl_ref, ysl, carry)

    gmat = _group_mean_matrix(GROUP_W, SSD_NORM_GROUP)

    def finish(z_ref, xa_ref, ys_ref, o_ref):
        seg = o_ref.shape[0] // SCAN_BATCH
        n = seg // L
        for g in range(SCAN_BATCH):
            for j in range(n):
                r = g * seg + j * L
                c0 = g * 2 * GROUP_W
                yy = ys_ref[j, :, c0:c0 + GROUP_W] + ys_ref[n - 1 - j, :, c0 + GROUP_W:c0 + 2 * GROUP_W] \
                    + dsk_ref[...] * xa_ref[r:r + L, 0:GROUP_W]
                gt = yy * _silu(z_ref[r:r + L, :].astype(f32))
                o_ref[r:r + L, :] = (gt * _group_rsqrt(gt, gmat) * ng_ref[...]).astype(o_ref.dtype)

    finish(zl_ref, xal, ysl, ol_ref)
    if need_ctx:
        finish(zc_ref, xac, ysc, oc_ref)
    else:
        oc_ref[...] = jnp.zeros(oc_ref.shape, oc_ref.dtype)


def _ssd(z_c, z_l, xbc_c, xbc_l, dt_c, dt_l, conv_w, conv_b, dt_bias, a_log, d_skip, norm_g, batch, need_ctx):
    cl, t = z_c.shape[0] // batch, z_l.shape[0] // batch
    nd = 2 * HEADS
    bc, bt = SCAN_BATCH * cl, SCAN_BATCH * t
    cw = jnp.zeros((8, SSD_XBC), f32).at[:SSD_CONV].set(conv_w)
    zc_spec, zl_spec = _seg_blocks(bc, bt, GROUP_W)
    xc_spec, xl_spec = _seg_blocks(bc, bt, SSD_XBC)
    drc_spec, drl_spec = _seg_blocks_t(bc, bt, nd)
    oc_spec, ol_spec = _seg_blocks(bc, bt, GROUP_W)
    return pl.pallas_call(
        functools.partial(_ssd_body, need_ctx=need_ctx),
        grid=(batch // SCAN_BATCH,),
        in_specs=[zc_spec, zl_spec, xc_spec, xl_spec, drc_spec, drl_spec,
                  _resident((8, SSD_XBC)), _resident((1, SSD_XBC)), _resident((nd, 1)), _resident((nd, 1)),
                  _resident((1, GROUP_W)), _resident((1, GROUP_W))],
        out_specs=[oc_spec, ol_spec],
        out_shape=[jax.ShapeDtypeStruct((batch * cl, GROUP_W), bf16),
                   jax.ShapeDtypeStruct((batch * t, GROUP_W), bf16)],
        scratch_shapes=[pltpu.VMEM((bc, SSD_XBC), f32), pltpu.VMEM((bt, SSD_XBC), f32),
                        pltpu.VMEM((cl // SCAN_CHUNK, SCAN_CHUNK, 2 * SCAN_BATCH * GROUP_W), f32),
                        pltpu.VMEM((t // SCAN_CHUNK, SCAN_CHUNK, 2 * SCAN_BATCH * GROUP_W), f32),
                        pltpu.VMEM((t + 16, SSD_XBC), f32),
                        pltpu.VMEM((2 * SSD_STATE, bc), bf16), pltpu.VMEM((2 * SSD_STATE, bt), bf16)],
        compiler_params=_params(("parallel",)),
        name="ssd",
    )(z_c, z_l, xbc_c, xbc_l, dt_c[:, :nd].T, dt_l[:, :nd].T, cw, conv_b.reshape(1, SSD_XBC),
      dt_bias.reshape(nd, 1), a_log.reshape(nd, 1),
      jnp.repeat(d_skip, HEAD_DIM).reshape(1, GROUP_W), norm_g.reshape(1, GROUP_W))


def _mla_body(ac_ref, al_ref, qn_ref, kvn_ref, wq_ref, wkv_ref, qg_ref, kg_ref, krg_ref, cos_ref, sin_ref,
              oc_ref, ol_ref, q_s, k_s, v_s, *, need_ctx):
    cl, t = ac_ref.shape[0], al_ref.shape[0]
    scale = MLA_QK ** -0.5
    rr = lax.broadcasted_iota(jnp.int32, (LANES, LANES), 0)
    cc = lax.broadcasted_iota(jnp.int32, (LANES, LANES), 1)
    in_nope = (rr < MLA_NOPE) & (cc < MLA_NOPE)
    in_rope = (rr >= MLA_NOPE) & (rr < MLA_QK) & (cc >= MLA_NOPE) & (cc < MLA_QK)
    head_gmat = jnp.where(in_nope, 1.0 / MLA_NOPE, jnp.where(in_rope, 1.0 / MLA_ROPE, 0.0)).astype(bf16)
    rot = _rotate_half_matrix(MLA_ROPE // 4)

    def head_norm(x):
        return x * _group_rsqrt(x, head_gmat)

    def project(a_ref, row0, n, roped):
        blk = 256
        for r in range(0, n, blk):
            a = a_ref[r:r + blk, :].astype(f32)
            aq, akv, akr = a[:, 0:GROUP_W], a[:, GROUP_W:GROUP_W + LANES], a[:, GROUP_W + LANES:GROUP_W + 2 * LANES]
            qh = _dot((aq * _row_rsqrt(aq, GROUP_W) * qn_ref[...]).astype(bf16), wq_ref[...])
            kv = _dot((akv * _row_rsqrt(akv, kvn_ref.shape[1]) * kvn_ref[...]).astype(bf16), wkv_ref[...])
            kr = akr * _row_rsqrt(akr, MLA_ROPE) * krg_ref[...]
            if roped:
                cos, sin = cos_ref[r:r + blk, :], sin_ref[r:r + blk, :]
                kr = _rope(kr, cos, sin, rot)
            for h in range(HEADS):
                qx = head_norm(qh[:, h * LANES:(h + 1) * LANES]) * qg_ref[...]
                if roped:
                    qx = _rope(qx, cos, sin, rot)
                q_s[h, row0 + r:row0 + r + blk, :] = (qx * scale).astype(bf16)
                kx = kv[:, h * LANES:(h + 1) * LANES]
                kx = kx * _row_rsqrt(kx, MLA_NOPE) * kg_ref[...]
                k_s[h, row0 + r:row0 + r + blk, :] = (kx + kr).astype(bf16)
            v_s[row0 + r:row0 + r + blk, :] = kv[:, HEADS * LANES:].astype(bf16)

    project(ac_ref, 0, cl, False)
    project(al_ref, cl, t, True)

    half_lane = lax.broadcasted_iota(jnp.int32, (1, LANES), 1) < HEAD_DIM

    def attend(q0, nq, nk, o_ref, o0):
        outs = []
        for pair in range(HEADS // 2):
            res = []
            for h in (2 * pair, 2 * pair + 1):
                s = _dot_nt(q_s[h, pl.ds(q0, nq), :], k_s[h, 0:nk, :])
                p = jnp.exp(s - jnp.max(s, axis=1, keepdims=True))
                den = jnp.sum(p, axis=1, keepdims=True)
                res.append(_dot(p.astype(bf16), v_s[0:nk, pair * LANES:(pair + 1) * LANES]) / den)
            outs.append(jnp.where(half_lane, res[0], res[1]))
        o_ref[pl.ds(o0, nq), :] = jnp.concatenate(outs, axis=1).astype(o_ref.dtype)

    def body(i, _):
        o0 = pl.multiple_of(i * MLA_QBLK, MLA_QBLK)
        attend(cl + o0, MLA_QBLK, cl + t, ol_ref, o0)
        return 0

    lax.fori_loop(0, t // MLA_QBLK, body, 0)
    if need_ctx:
        attend(0, cl, cl, oc_ref, 0)
    else:
        oc_ref[...] = jnp.zeros(oc_ref.shape, oc_ref.dtype)


def _mla(a_c, a_l, q_norm, kv_norm, wq_b, wkv_b, q_gain, k_gain, cos, sin, batch, need_ctx):
    cl, t = a_c.shape[0] // batch, a_l.shape[0] // batch
    kvl = kv_norm.shape[0]
    wq = jnp.zeros((GROUP_W, HEADS, LANES), f32).at[:, :, :MLA_QK].set(wq_b.reshape(GROUP_W, HEADS, MLA_QK))
    wkv4 = wkv_b.reshape(kvl, HEADS, MLA_NOPE + HEAD_DIM)
    wk = jnp.zeros((kvl, HEADS, LANES), f32).at[:, :, :MLA_NOPE].set(wkv4[:, :, :MLA_NOPE])
    wkv = jnp.concatenate([wk.reshape(kvl, HEADS * LANES), wkv4[:, :, MLA_NOPE:].reshape(kvl, GROUP_W)], axis=1)
    slab = lambda v, off: jnp.zeros((1, LANES), f32).at[0, off:off + v.shape[0]].set(v)
    qg = slab(q_gain, 0)
    kg = slab(k_gain[:MLA_NOPE], 0)
    krg = slab(k_gain[MLA_NOPE:], MLA_NOPE)
    ac_spec, al_spec = _seg_blocks(cl, t, 2 * GROUP_W)
    oc_spec, ol_spec = _seg_blocks(cl, t, GROUP_W)
    return pl.pallas_call(
        functools.partial(_mla_body, need_ctx=need_ctx),
        grid=(batch,),
        in_specs=[ac_spec, al_spec, _resident((1, GROUP_W)), _resident((1, LANES)),
                  _resident((GROUP_W, HEADS * LANES)), _resident((kvl, HEADS * LANES + GROUP_W)),
                  _resident((1, LANES)), _resident((1, LANES)), _resident((1, LANES)),
                  _resident((t, LANES)), _resident((t, LANES))],
        out_specs=[oc_spec, ol_spec],
        out_shape=[jax.ShapeDtypeStruct((batch * cl, GROUP_W), bf16),
                   jax.ShapeDtypeStruct((batch * t, GROUP_W), bf16)],
        scratch_shapes=[pltpu.VMEM((HEADS, cl + t, LANES), bf16), pltpu.VMEM((HEADS, cl + t, LANES), bf16),
                        pltpu.VMEM((cl + t, GROUP_W), bf16)],
        compiler_params=_params(("parallel",)),
        name="mla",
    )(a_c, a_l, q_norm.reshape(1, GROUP_W), kv_norm.reshape(1, kvl),
      wq.reshape(GROUP_W, HEADS * LANES).astype(bf16), wkv.astype(bf16), qg, kg, krg, cos, sin)


def _swa_body(wc_ref, wl_ref, qg_ref, kg_ref, sink_ref, cos_ref, sin_ref, oc_ref, ol_ref,
              q_s, qc_s, k_s, v_s, kc_s, vc_s, *, need_ctx):
    cl, t = wc_ref.shape[0], wl_ref.shape[0]
    scale = HEAD_DIM ** -0.5
    kvw = SWA_KV_HEADS * HEAD_DIM
    blk = SWA_BLK
    gmat_q = _group_mean_matrix(GROUP_W, HEAD_DIM)
    gmat_k = _group_mean_matrix(kvw, HEAD_DIM)
    rot = _rotate_half_matrix(HEAD_DIM // 4)

    def project(w_ref, n, roped, qdst, kdst, vdst, row0):
        step = 256
        for r in range(0, n, step):
            w = w_ref[r:r + step, :].astype(f32)
            q, k = w[:, 0:GROUP_W], w[:, GROUP_W:GROUP_W + kvw]
            q = q * _group_rsqrt(q, gmat_q) * qg_ref[...]
            k = k * _group_rsqrt(k, gmat_k) * kg_ref[...]
            if roped:
                cos, sin = cos_ref[r:r + step, :], sin_ref[r:r + step, :]
                q = jnp.concatenate([_rope(q[:, 0:LANES], cos, sin, rot),
                                     _rope(q[:, LANES:2 * LANES], cos, sin, rot)], axis=1)
                k = _rope(k, cos, sin, rot)
            qdst[r:r + step, :] = (q * scale).astype(bf16)
            kdst[row0 + r:row0 + r + step, :] = k.astype(bf16)
            vdst[row0 + r:row0 + r + step, :] = w_ref[r:r + step, GROUP_W + kvw:GROUP_W + 2 * kvw]

    zeros = jnp.zeros((blk, kvw), bf16)
    for s in (k_s, v_s):
        s[0:blk, :] = zeros
        s[blk + t:2 * blk + t, :] = zeros
    project(wc_ref, cl, False, qc_s, kc_s, vc_s, 0)
    project(wl_ref, t, True, q_s, k_s, v_s, blk)

    rr = lax.broadcasted_iota(jnp.int32, (2 * blk, 3 * blk), 0) % blk
    jj = lax.broadcasted_iota(jnp.int32, (2 * blk, 3 * blk), 1)
    band = (jj - rr >= 0) & (jj - rr <= 2 * SWA_WINDOW)
    half0 = lax.broadcasted_iota(jnp.int32, (1, LANES), 1) < HEAD_DIM
    zero_b = jnp.zeros((), bf16)

    def wide(x, n):
        return jnp.broadcast_to(x, (x.shape[0], n))

    def stacked_queries(q_ref, r0, n, kv):
        keep = half0 if kv == 0 else ~half0
        return jnp.concatenate([jnp.where(keep, q_ref[pl.ds(r0, n), 0:LANES], zero_b),
                                jnp.where(keep, q_ref[pl.ds(r0, n), LANES:2 * LANES], zero_b)], axis=0)

    def stacked_sink(n, kv):
        first = lax.broadcasted_iota(jnp.int32, (2 * n, 1), 0) < n
        return jnp.where(first, sink_ref[2 * kv:2 * kv + 1, :], sink_ref[2 * kv + 1:2 * kv + 2, :])

    def lat_block(n, _):
        r0 = pl.multiple_of(n * blk, blk)
        kpos = jj + (n - 1) * blk
        valid = band & (kpos >= 0) & (kpos < t)
        res = []
        for kv in range(SWA_KV_HEADS):
            q2 = stacked_queries(q_s, r0, blk, kv)
            s_loc = jnp.where(valid, _dot_nt(q2, k_s[pl.ds(r0, 3 * blk), :]), NEG)
            s_ctx = _dot_nt(q2, kc_s[...])
            sink = stacked_sink(blk, kv)
            m = jnp.maximum(wide(jnp.maximum(jnp.max(s_loc, axis=1, keepdims=True),
                                             jnp.max(s_ctx, axis=1, keepdims=True)), LANES), sink)
            p_loc = jnp.exp(s_loc - jnp.concatenate([m] * 3, axis=1))
            p_ctx = jnp.exp(s_ctx - jnp.concatenate([m] * (cl // LANES), axis=1))
            den = wide(jnp.sum(p_loc, axis=1, keepdims=True) + jnp.sum(p_ctx, axis=1, keepdims=True), LANES) \
                + jnp.exp(sink - m)
            o = _dot(p_loc.astype(bf16), v_s[pl.ds(r0, 3 * blk), :]) + _dot(p_ctx.astype(bf16), vc_s[...])
            res.append(o / den)
        ol_ref[pl.ds(r0, blk), :] = jnp.concatenate(
            [jnp.where(half0, res[0][0:blk], res[1][0:blk]),
             jnp.where(half0, res[0][blk:2 * blk], res[1][blk:2 * blk])], axis=1).astype(ol_ref.dtype)
        return 0

    lax.fori_loop(0, t // blk, lat_block, 0, unroll=SCAN_UNROLL)

    if need_ctx:
        res = []
        for kv in range(SWA_KV_HEADS):
            s = _dot_nt(stacked_queries(qc_s, 0, cl, kv), kc_s[...])
            sink = stacked_sink(cl, kv)
            m = jnp.maximum(wide(jnp.max(s, axis=1, keepdims=True), LANES), sink)
            p = jnp.exp(s - jnp.concatenate([m] * (cl // LANES), axis=1))
            den = wide(jnp.sum(p, axis=1, keepdims=True), LANES) + jnp.exp(sink - m)
            res.append(_dot(p.astype(bf16), vc_s[...]) / den)
        oc_ref[...] = jnp.concatenate(
            [jnp.where(half0, res[0][0:cl], res[1][0:cl]),
             jnp.where(half0, res[0][cl:2 * cl], res[1][cl:2 * cl])], axis=1).astype(oc_ref.dtype)
    else:
        oc_ref[...] = jnp.zeros(oc_ref.shape, oc_ref.dtype)


def _swa_head_order(a, axis):
    blocks = jnp.split(a, HEADS, axis=axis)
    return jnp.concatenate([blocks[0], blocks[2], blocks[1], blocks[3]], axis=axis)


def _swa(w_c, w_l, q_gain, k_gain, sink, cos, sin, batch, need_ctx):
    cl, t = w_c.shape[0] // batch, w_l.shape[0] // batch
    kvw = SWA_KV_HEADS * HEAD_DIM
    wc_spec, wl_spec = _seg_blocks(cl, t, 2 * GROUP_W)
    oc_spec, ol_spec = _seg_blocks(cl, t, GROUP_W)
    return pl.pallas_call(
        functools.partial(_swa_body, need_ctx=need_ctx),
        grid=(batch,),
        in_specs=[wc_spec, wl_spec, _resident((1, GROUP_W)), _resident((1, kvw)), _resident((HEADS, LANES)),
                  _resident((t, LANES)), _resident((t, LANES))],
        out_specs=[oc_spec, ol_spec],
        out_shape=[jax.ShapeDtypeStruct((batch * cl, GROUP_W), bf16),
                   jax.ShapeDtypeStruct((batch * t, GROUP_W), bf16)],
        scratch_shapes=[pltpu.VMEM((t, GROUP_W), bf16), pltpu.VMEM((cl, GROUP_W), bf16),
                        pltpu.VMEM((t + 2 * SWA_BLK, kvw), bf16), pltpu.VMEM((t + 2 * SWA_BLK, kvw), bf16),
                        pltpu.VMEM((cl, kvw), bf16), pltpu.VMEM((cl, kvw), bf16)],
        compiler_params=_params(("parallel",)),
        name="swa",
    )(w_c, w_l, jnp.tile(q_gain, HEADS).reshape(1, GROUP_W), jnp.tile(k_gain, SWA_KV_HEADS).reshape(1, kvw),
      jnp.broadcast_to(sink.reshape(HEADS, 1), (HEADS, LANES)), cos, sin)


def _rope_tables(t, rot_dim, lane0):
    pos = jnp.arange(t)
    row, col = (pos // GRID_W).astype(f32), (pos % GRID_W).astype(f32)
    nf = rot_dim // 4
    inv = ROPE_BASE ** (-jnp.arange(nf, dtype=f32) / nf)
    ar, ac = row[:, None] * inv, col[:, None] * inv
    ang = jnp.concatenate([ar, ar, ac, ac], axis=-1)
    return jnp.cos(ang), jnp.sin(ang)


def _pack_in_weight(w_in):
    o = 0
    seg = {}
    for name, n in (("qkvo", 4 * GROUP_W), ("mg", 4 * HEADS), ("a_q", GROUP_W), ("a_kv", GROUP_W // 2),
                    ("a_kr", MLA_ROPE), ("swa", 2 * GROUP_W), ("z", GROUP_W), ("xbc", SSD_XBC), ("dt", 2 * HEADS)):
        seg[name] = w_in[:, o:o + n]
        o += n
    d = w_in.shape[0]
    zeros = lambda n: jnp.zeros((d, n), w_in.dtype)
    packed = jnp.concatenate([
        seg["qkvo"], seg["mg"], zeros(LANES - 4 * HEADS),
        seg["a_q"], seg["a_kv"], zeros(MLA_NOPE), seg["a_kr"], zeros(LANES - MLA_QK),
        _swa_head_order(seg["swa"][:, :GROUP_W], 1), seg["swa"][:, GROUP_W:],
        seg["z"], seg["xbc"], seg["dt"], zeros(LANES - 2 * HEADS)], axis=1)
    return packed.astype(bf16)


def kernel(x, c, ctx, c_ctx, w_mod, b_mod, ffn1_norm, ffn1_wi, ffn1_wo, mix_norm, w_in, w_out, mlstm_gate_b, mlstm_out_norm, mla_q_norm, mla_kv_norm, mla_wq_b, mla_wkv_b, mla_q_gain, mla_k_gain, swa_q_gain, swa_k_gain, swa_sink, ssd_conv_w, ssd_conv_b, ssd_dt_bias, ssd_a_log, ssd_d, ssd_norm, ffn2_norm, ffn2_wi, ffn2_wo):
    b, t, d = x.shape
    cl = ctx.shape[1]
    depth = w_mod.shape[0]
    dff = ffn1_wo.shape[1]
    lat_tiles = t // ROW_TILE

    cos_m, sin_m = _rope_tables(t, MLA_ROPE, MLA_NOPE)
    pad_id = lambda tab, fill: jnp.concatenate(
        [jnp.full((t, MLA_NOPE), fill, f32), tab, jnp.full((t, LANES - MLA_QK), fill, f32)], axis=1)
    cos_m, sin_m = pad_id(cos_m, 1.0), pad_id(sin_m, 0.0)
    cos_s, sin_s = _rope_tables(t, HEAD_DIM, 0)
    cos_s, sin_s = jnp.tile(cos_s, (1, LANES // HEAD_DIM)), jnp.tile(sin_s, (1, LANES // HEAD_DIM))

    h = x.reshape(b * t, d)
    hc = ctx.reshape(b * cl, d)
    cc = jnp.concatenate([c, c_ctx[None, :]], axis=0)
    for l in range(depth):
        need_ctx = l < depth - 1
        mod = _modulation(cc, w_mod[l], b_mod[l]).reshape(b + 1, N_MOD, d)
        streams = ((0, lat_tiles), (b, None))

        ffn1_w = (ffn1_norm[l].reshape(1, d), ffn1_wi[l][:, :dff].astype(bf16), ffn1_wi[l][:, dff:].astype(bf16),
                  ffn1_wo[l].astype(bf16))
        ffn2_w = (ffn2_norm[l].reshape(1, d), ffn2_wi[l][:, :dff].astype(bf16), ffn2_wi[l][:, dff:].astype(bf16),
                  ffn2_wo[l].astype(bf16))
        h = _ffn(h, mod, *ffn1_w, 0, *streams[0])
        hc = _ffn(hc, mod, *ffn1_w, 0, *streams[1])
        w_packed = _pack_in_weight(w_in[l])
        qkvo_l, mg_l, mla_l, swa_l, z_l, xbc_l, dt_l = _inproj(h, mod, mix_norm[l].reshape(1, d), w_packed, *streams[0])
        qkvo_c, mg_c, mla_c, swa_c, z_c, xbc_c, dt_c = _inproj(hc, mod, mix_norm[l].reshape(1, d), w_packed, *streams[1])
        a_c, a_l = _mlstm(qkvo_c, qkvo_l, mg_c, mg_l, mlstm_gate_b[l], mlstm_out_norm[l], b, need_ctx)
        m_c, m_l = _mla(mla_c, mla_l, mla_q_norm[l], mla_kv_norm[l], mla_wq_b[l], mla_wkv_b[l], mla_q_gain[l],
                        mla_k_gain[l], cos_m, sin_m, b, need_ctx)
        s_c, s_l = _swa(swa_c, swa_l, swa_q_gain[l], swa_k_gain[l], swa_sink[l], cos_s, sin_s, b, need_ctx)
        d_c, d_l = _ssd(z_c, z_l, xbc_c, xbc_l, dt_c, dt_l, ssd_conv_w[l], ssd_conv_b[l], ssd_dt_bias[l],
                        ssd_a_log[l], ssd_d[l], ssd_norm[l], b, need_ctx)
        wo_b = jnp.concatenate([w_out[l][:2 * GROUP_W], _swa_head_order(w_out[l][2 * GROUP_W:3 * GROUP_W], 0),
                                w_out[l][3 * GROUP_W:]], axis=0).astype(bf16)
        h = _mix_ffn(h, mod, (a_l, m_l, s_l, d_l), wo_b, *ffn2_w, *streams[0])
        if need_ctx:
            hc = _mix_ffn(hc, mod, (a_c, m_c, s_c, d_c), wo_b, *ffn2_w, *streams[1])
    return h.reshape(b, t, d)
```

```python
import functools

import jax
import jax.numpy as jnp
from jax import lax
from jax.experimental import pallas as pl
from jax.experimental.pallas import tpu as pltpu

f32 = jnp.float32
bf16 = jnp.bfloat16

RMS_EPS = 1e-6
ROPE_BASE = 10000.0
GRID_W = 64
N_MOD = 9
HEADS = 4
HEAD_DIM = 64
GROUP_W = HEADS * HEAD_DIM
MLA_NOPE = 64
MLA_ROPE = 32
MLA_QK = MLA_NOPE + MLA_ROPE
SWA_KV_HEADS = 2
SWA_WINDOW = 128
SWA_BLK = 128
SSD_STATE = 64
SSD_GROUPS = 2
SSD_CONV = 5
SSD_XBC = GROUP_W + 2 * SSD_GROUPS * SSD_STATE
SSD_NORM_GROUP = 128

LANES = 128
ROW_TILE = 512
FF_CHUNK = 256
SCAN_CHUNK = 128
SCAN_UNROLL = 2
SWA_UNROLL = 4
SCAN_BATCH = 2
MLA_QBLK = 512
NEG = -1e30
VMEM_LIMIT = 56 * 1024 * 1024

P_QKVO = 0
P_MG = 1024
P_MLA = 1152
P_SWA = 1664
P_Z = 2176
P_XBC = 2432
P_DT = 2944
P_COLS = 3072


def _dot(a, b):
    return jnp.dot(a, b, preferred_element_type=f32)


def _dot_nt(a, b):
    return lax.dot_general(a, b, (((1,), (1,)), ((), ())), preferred_element_type=f32)


def _sigmoid(x):
    return 1.0 / (1.0 + jnp.exp(-x))


def _silu(x):
    return x * _sigmoid(x)


def _softplus(x):
    return jnp.maximum(x, 0.0) + jnp.log(1.0 + jnp.exp(-jnp.abs(x)))


def _log_sigmoid(x):
    return -_softplus(-x)


def _split3(x):
    hi = x.astype(bf16)
    r = x - hi.astype(f32)
    mid = r.astype(bf16)
    lo = (r - mid.astype(f32)).astype(bf16)
    return hi, mid, lo


def _dot_tri(x, tri):
    m = x.shape[0]
    terms = jnp.concatenate([p.astype(f32) for p in _split3(x)], axis=0).astype(bf16)
    y = _dot(terms, tri)
    return y[0:m] + y[m:2 * m] + y[2 * m:3 * m]


def _dot2(x, w):
    hi = x.astype(bf16)
    mid = (x - hi.astype(f32)).astype(bf16)
    return _dot(hi, w) + _dot(mid, w)


def _group_mean_matrix(n, gsz):
    r = lax.broadcasted_iota(jnp.int32, (n, n), 0) // gsz
    c = lax.broadcasted_iota(jnp.int32, (n, n), 1) // gsz
    return jnp.where(r == c, 1.0 / gsz, 0.0).astype(bf16)


def _group_rsqrt(x, gmat):
    return lax.rsqrt(_dot2(x * x, gmat) + RMS_EPS)


def _row_rsqrt(x, n_real):
    return lax.rsqrt(jnp.sum(x * x, axis=1, keepdims=True) * (1.0 / n_real) + RMS_EPS)


def _rotate_half_matrix(quarter):
    r = lax.broadcasted_iota(jnp.int32, (LANES, LANES), 0)
    c = lax.broadcasted_iota(jnp.int32, (LANES, LANES), 1)
    first = (c % (2 * quarter)) < quarter
    return jnp.where(first & (r == c + quarter), -1.0, jnp.where(~first & (r == c - quarter), 1.0, 0.0)).astype(bf16)


def _rope(x, cos, sin, rot):
    return x * cos + _dot2(x, rot) * sin


def _resident(shape):
    nd = len(shape)
    return pl.BlockSpec(shape, lambda *_: (0,) * nd, pipeline_mode=pl.Buffered(1))


def _params(sem):
    return pltpu.CompilerParams(dimension_semantics=sem, vmem_limit_bytes=VMEM_LIMIT)


def _mod_body(c_ref, w_ref, b_ref, o_ref):
    s = _silu(c_ref[...])
    o_ref[...] = jnp.dot(s, w_ref[...], preferred_element_type=f32,
                         precision=lax.Precision.HIGHEST) + b_ref[...]


def _modulation(cc, w, b):
    m, d = cc.shape
    n = w.shape[1]
    tn = 1024
    return pl.pallas_call(
        _mod_body,
        grid=(n // tn,),
        in_specs=[pl.BlockSpec((m, d), lambda j: (0, 0)),
                  pl.BlockSpec((d, tn), lambda j: (0, j)),
                  pl.BlockSpec((1, tn), lambda j: (0, j))],
        out_specs=pl.BlockSpec((m, tn), lambda j: (0, j)),
        out_shape=jax.ShapeDtypeStruct((m, n), f32),
        compiler_params=_params(("arbitrary",)),
        name="modulation",
    )(cc, w, b.reshape(1, n))


def _norm_mod(x, gain, shift, scale):
    xn = x * _row_rsqrt(x, x.shape[1]) * gain
    return (xn * (1.0 + scale) + shift).astype(bf16)


def _swiglu_half_step(x, mod_ref, g_ref, wig_ref, wiu_ref, wo_ref, mi):
    xb = _norm_mod(x, g_ref[...], mod_ref[mi:mi + 1, :], mod_ref[mi + 1:mi + 2, :])
    acc = jnp.zeros(x.shape, f32)
    for c in range(wig_ref.shape[1] // FF_CHUNK):
        sl = slice(c * FF_CHUNK, (c + 1) * FF_CHUNK)
        g = _dot(xb, wig_ref[:, sl])
        u = _dot(xb, wiu_ref[:, sl])
        acc = acc + _dot((_silu(g) * u).astype(bf16), wo_ref[sl, :])
    return x + (0.5 * mod_ref[mi + 2:mi + 3, :]) * acc


def _ffn_body(x_ref, mod_ref, g_ref, wig_ref, wiu_ref, wo_ref, o_ref, *, mi):
    o_ref[...] = _swiglu_half_step(x_ref[...], mod_ref, g_ref, wig_ref, wiu_ref, wo_ref, mi)


def _mix_ffn_body(x_ref, mod_ref, a_ref, m_ref, w_ref, s_ref, wout_ref, g_ref, wig_ref, wiu_ref, wo_ref, o_ref):
    acc = _dot(a_ref[...], wout_ref[0:GROUP_W, :])
    acc = acc + _dot(m_ref[...], wout_ref[GROUP_W:2 * GROUP_W, :])
    acc = acc + _dot(w_ref[...], wout_ref[2 * GROUP_W:3 * GROUP_W, :])
    acc = acc + _dot(s_ref[...], wout_ref[3 * GROUP_W:4 * GROUP_W, :])
    x = x_ref[...] + mod_ref[5:6, :] * acc
    o_ref[...] = _swiglu_half_step(x, mod_ref, g_ref, wig_ref, wiu_ref, wo_ref, 6)


def _mod_spec(d, mod_base, tiles_per_mod):
    if tiles_per_mod is None:
        return pl.BlockSpec((None, N_MOD, d), lambda i: (mod_base, 0, 0))
    return pl.BlockSpec((None, N_MOD, d), lambda i: (mod_base + i // tiles_per_mod, 0, 0))


def _ffn(x, mod, gain, wig, wiu, wo, mi, mod_base, tiles_per_mod):
    rows, d = x.shape
    dff = wig.shape[1]
    return pl.pallas_call(
        functools.partial(_ffn_body, mi=mi),
        grid=(rows // ROW_TILE,),
        in_specs=[pl.BlockSpec((ROW_TILE, d), lambda i: (i, 0)),
                  _mod_spec(d, mod_base, tiles_per_mod),
                  _resident((1, d)), _resident((d, dff)), _resident((d, dff)), _resident((dff, d))],
        out_specs=pl.BlockSpec((ROW_TILE, d), lambda i: (i, 0)),
        out_shape=jax.ShapeDtypeStruct((rows, d), f32),
        compiler_params=_params(("parallel",)),
        name="ffn",
    )(x, mod, gain, wig, wiu, wo)


def _inproj_body(x_ref, mod_ref, g_ref, w_ref, qkvo_ref, mg_ref, mla_ref, swa_ref, z_ref, xbc_ref, dt_ref):
    xb = _norm_mod(x_ref[...], g_ref[...], mod_ref[3:4, :], mod_ref[4:5, :])
    qkvo_ref[...] = _dot(xb, w_ref[:, P_QKVO:P_MG]).astype(bf16)
    mg_ref[...] = _dot(xb, w_ref[:, P_MG:P_MLA])
    mla_ref[...] = _dot(xb, w_ref[:, P_MLA:P_SWA]).astype(bf16)
    swa_ref[...] = _dot(xb, w_ref[:, P_SWA:P_Z]).astype(bf16)
    z_ref[...] = _dot(xb, w_ref[:, P_Z:P_XBC]).astype(bf16)
    xbc_ref[...] = _dot(xb, w_ref[:, P_XBC:P_DT]).astype(bf16)
    dt_ref[...] = _dot(xb, w_ref[:, P_DT:P_COLS])


def _inproj(x, mod, gain, w, mod_base, tiles_per_mod):
    rows, d = x.shape
    widths = [(P_MG - P_QKVO, bf16), (P_MLA - P_MG, f32), (P_SWA - P_MLA, bf16), (P_Z - P_SWA, bf16),
              (P_XBC - P_Z, bf16), (P_DT - P_XBC, bf16), (P_COLS - P_DT, f32)]
    return pl.pallas_call(
        _inproj_body,
        grid=(rows // ROW_TILE,),
        in_specs=[pl.BlockSpec((ROW_TILE, d), lambda i: (i, 0)),
                  _mod_spec(d, mod_base, tiles_per_mod),
                  _resident((1, d)), _resident((d, P_COLS))],
        out_specs=[pl.BlockSpec((ROW_TILE, n), lambda i: (i, 0)) for n, _ in widths],
        out_shape=[jax.ShapeDtypeStruct((rows, n), dt) for n, dt in widths],
        compiler_params=_params(("parallel",)),
        name="inproj",
    )(x, mod, gain, w)


def _mix_ffn(x, mod, mixed, w_out, gain, wig, wiu, wo, mod_base, tiles_per_mod):
    rows, d = x.shape
    dff = wig.shape[1]
    return pl.pallas_call(
        _mix_ffn_body,
        grid=(rows // ROW_TILE,),
        in_specs=[pl.BlockSpec((ROW_TILE, d), lambda i: (i, 0)),
                  _mod_spec(d, mod_base, tiles_per_mod)]
                 + [pl.BlockSpec((ROW_TILE, GROUP_W), lambda i: (i, 0))] * 4
                 + [_resident(w_out.shape), _resident((1, d)), _resident((d, dff)), _resident((d, dff)),
                    _resident((dff, d))],
        out_specs=pl.BlockSpec((ROW_TILE, d), lambda i: (i, 0)),
        out_shape=jax.ShapeDtypeStruct((rows, d), f32),
        compiler_params=_params(("parallel",)),
        name="mix_ffn",
    )(x, mod, *mixed, w_out, gain, wig, wiu, wo)


def _tri_masks(n):
    r = lax.broadcasted_iota(jnp.int32, (n, n), 0)
    c = lax.broadcasted_iota(jnp.int32, (n, n), 1)
    return r >= c, r <= c


def _seg_blocks(cl, t, width):
    return (pl.BlockSpec((cl, width), lambda b: (b, 0)), pl.BlockSpec((t, width), lambda b: (b, 0)))


def _seg_blocks_t(cl, t, nrow):
    return (pl.BlockSpec((nrow, cl), lambda b: (0, b)), pl.BlockSpec((nrow, t), lambda b: (0, b)))


def _mlstm_body(qc_ref, ql_ref, grc_ref, grl_ref, gbr_ref, on_ref, oc_ref, ol_ref,
                hsc, hsl, ktc, ktl, *, need_ctx):
    L = SCAN_CHUNK
    PW = 2 * HEAD_DIM
    lower, upper = _tri_masks(L)
    tri_lo = jnp.where(lower, 1.0, 0.0).astype(bf16)
    tri_up = jnp.where(upper, 1.0, 0.0).astype(bf16)
    lane = lax.broadcasted_iota(jnp.int32, (1, PW), 1)
    half0 = lane < HEAD_DIM
    row_half0 = lax.broadcasted_iota(jnp.int32, (PW, 1), 0) < HEAD_DIM
    rr = lax.broadcasted_iota(jnp.int32, (PW, PW), 0)
    cc = lax.broadcasted_iota(jnp.int32, (PW, PW), 1)
    rr2 = lax.broadcasted_iota(jnp.int32, (PW, 2 * PW), 0)
    cc2 = lax.broadcasted_iota(jnp.int32, (PW, 2 * PW), 1) % PW
    blockdiag2 = (rr2 < HEAD_DIM) == (cc2 < HEAD_DIM)
    eye = jnp.where(rr == cc, 1.0, 0.0).astype(bf16)
    ones_bd = jnp.concatenate([jnp.where(half0, 1.0, 0.0) * jnp.ones((L, 1), f32),
                               jnp.where(half0, 0.0, 1.0) * jnp.ones((L, 1), f32)], axis=0).astype(bf16)
    ones_full = jnp.ones((L, PW), bf16)
    zero_b = jnp.zeros((), bf16)

    def transpose_keys(q_ref, kt_ref):
        for r in range(0, q_ref.shape[0], L):
            for p in range(HEADS // 2):
                k = q_ref[r:r + L, GROUP_W + p * PW:GROUP_W + (p + 1) * PW] * (HEAD_DIM ** -0.5)
                kt_ref[p * PW:(p + 1) * PW, r:r + L] = _dot_nt(eye, k).astype(bf16)

    transpose_keys(qc_ref, ktc)
    transpose_keys(ql_ref, ktl)

    def direction(q_ref, kt_ref, gr_ref, j, rev, carry, off):
        r0 = pl.multiple_of(off + j * L, L)
        base = 2 * HEADS if rev else 0
        mask = upper if rev else lower
        grow = gr_ref[:, pl.ds(r0, L)] + gbr_ref[...]
        lfr = _log_sigmoid(grow)
        brows = _dot_tri(lfr, tri_lo if rev else tri_up)
        bcols = jnp.transpose(brows)
        r_all = grow[base:base + HEADS, :] - brows[base + HEADS:base + 2 * HEADS, :]
        wide = lambda x: jnp.broadcast_to(x, (HEADS, L))
        b_end = wide(jnp.sum(lfr[base + HEADS:base + 2 * HEADS, :], axis=1, keepdims=True))
        cns, m_all = carry
        mm = jnp.maximum(m_all, wide(jnp.max(r_all, axis=1, keepdims=True)))
        a_all = jnp.exp(m_all - mm)
        w_all = jnp.exp(r_all - mm)
        twice = lambda x: jnp.concatenate([x, x], axis=1)
        new_cn, outs = [], []
        for p in range(HEADS // 2):
            qp = q_ref[pl.ds(r0, L), p * PW:(p + 1) * PW]
            kp = q_ref[pl.ds(r0, L), GROUP_W + p * PW:GROUP_W + (p + 1) * PW] * (HEAD_DIM ** -0.5)
            vp = q_ref[pl.ds(r0, L), 2 * GROUP_W + p * PW:2 * GROUP_W + (p + 1) * PW]
            ktp = kt_ref[p * PW:(p + 1) * PW, pl.ds(r0, L)]
            qk = _dot_nt(jnp.concatenate([jnp.where(half0, qp, zero_b), jnp.where(half0, zero_b, qp)], axis=0), kp)
            sb, wi, em = [], [], []
            for e in range(2):
                h = 2 * p + e
                rm = jnp.where(mask, r_all[h:h + 1, :], NEG)
                m_h = jnp.broadcast_to(m_all[h:h + 1, :], (L, L))
                c = jnp.maximum(m_h, jnp.broadcast_to(jnp.max(rm, axis=1, keepdims=True), (L, L)))
                sb.append((qk[e * L:(e + 1) * L] * jnp.exp(rm - c)).astype(bf16))
                wi.append(jnp.exp(m_h - c))
                bcol = bcols[:, base + HEADS + h:base + HEADS + h + 1]
                em.append(jnp.exp(-(jnp.broadcast_to(bcol, (L, L)) + c)))
            s2 = jnp.concatenate(sb, axis=1)
            vbd = jnp.concatenate([jnp.where(half0, vp, zero_b), jnp.where(half0, zero_b, vp)], axis=0)
            hx = twice(jnp.where(half0, wi[0], wi[1])) * _dot(qp, cns[p].astype(bf16)) \
                + _dot(s2, jnp.concatenate([vbd, ones_bd], axis=1))
            den = jnp.maximum(jnp.abs(hx[:, PW:2 * PW]), jnp.where(half0, em[0], em[1]))
            outs.append(hx[:, 0:PW] / den)
            wsel = jnp.where(row_half0, w_all[2 * p:2 * p + 1, :], w_all[2 * p + 1:2 * p + 2, :])
            asel = jnp.where(row_half0, a_all[2 * p:2 * p + 1, :], a_all[2 * p + 1:2 * p + 2, :])
            ktw = (ktp.astype(f32) * wsel).astype(bf16)
            upd = _dot(ktw, jnp.concatenate([vp, ones_full], axis=1))
            new_cn.append(twice(asel) * cns[p] + jnp.where(blockdiag2, upd, 0.0))
        return (new_cn, b_end + mm), jnp.concatenate(outs, axis=1)

    def segment(q_ref, kt_ref, gr_ref, hs_ref, carry):
        seg = q_ref.shape[0] // SCAN_BATCH
        n = seg // L

        def body(i, carry):
            out, pieces = [], []
            for g, (fw, bw) in enumerate(carry):
                fw, h_f = direction(q_ref, kt_ref, gr_ref, i, False, fw, g * seg)
                bw, h_b = direction(q_ref, kt_ref, gr_ref, n - 1 - i, True, bw, g * seg)
                out.append((fw, bw))
                pieces += [h_f, h_b]
            hs_ref[i] = jnp.concatenate(pieces, axis=1)
            return tuple(out)

        return lax.fori_loop(0, n, body, carry, unroll=SCAN_UNROLL)

    def init():
        return ([jnp.zeros((PW, 2 * PW), f32) for _ in range(HEADS // 2)], jnp.zeros((HEADS, L), f32))

    carry = segment(qc_ref, ktc, grc_ref, hsc, tuple((init(), init()) for _ in range(SCAN_BATCH)))
    segment(ql_ref, ktl, grl_ref, hsl, carry)

    gmat = _group_mean_matrix(GROUP_W, HEAD_DIM)

    def finish(q_ref, hs_ref, o_ref):
        seg = o_ref.shape[0] // SCAN_BATCH
        n = seg // L
        for g in range(SCAN_BATCH):
            for j in range(n):
                r = g * seg + j * L
                c0 = g * 2 * GROUP_W
                hh = hs_ref[j, :, c0:c0 + GROUP_W] + hs_ref[n - 1 - j, :, c0 + GROUP_W:c0 + 2 * GROUP_W]
                hn = hh * _group_rsqrt(hh, gmat) * on_ref[...]
                og = q_ref[r:r + L, 3 * GROUP_W:4 * GROUP_W].astype(f32)
                o_ref[r:r + L, :] = (_sigmoid(og) * hn).astype(o_ref.dtype)

    finish(ql_ref, hsl, ol_ref)
    if need_ctx:
        finish(qc_ref, hsc, oc_ref)
    else:
        oc_ref[...] = jnp.zeros(oc_ref.shape, oc_ref.dtype)


def _mlstm(qkvo_c, qkvo_l, mg_c, mg_l, gate_b, out_norm, batch, need_ctx):
    cl, t = qkvo_c.shape[0] // batch, qkvo_l.shape[0] // batch
    ng = 4 * HEADS
    bc, bt = SCAN_BATCH * cl, SCAN_BATCH * t
    gb_row = gate_b.reshape(ng, 1)
    qc_spec, ql_spec = _seg_blocks(bc, bt, 4 * GROUP_W)
    grc_spec, grl_spec = _seg_blocks_t(bc, bt, ng)
    oc_spec, ol_spec = _seg_blocks(bc, bt, GROUP_W)
    return pl.pallas_call(
        functools.partial(_mlstm_body, need_ctx=need_ctx),
        grid=(batch // SCAN_BATCH,),
        in_specs=[qc_spec, ql_spec, grc_spec, grl_spec, _resident((ng, 1)), _resident((1, GROUP_W))],
        out_specs=[oc_spec, ol_spec],
        out_shape=[jax.ShapeDtypeStruct((batch * cl, GROUP_W), bf16),
                   jax.ShapeDtypeStruct((batch * t, GROUP_W), bf16)],
        scratch_shapes=[pltpu.VMEM((cl // SCAN_CHUNK, SCAN_CHUNK, 2 * SCAN_BATCH * GROUP_W), f32),
                        pltpu.VMEM((t // SCAN_CHUNK, SCAN_CHUNK, 2 * SCAN_BATCH * GROUP_W), f32),
                        pltpu.VMEM((GROUP_W, bc), bf16), pltpu.VMEM((GROUP_W, bt), bf16)],
        compiler_params=_params(("parallel",)),
        name="mlstm",
    )(qkvo_c, qkvo_l, mg_c[:, :ng].T, mg_l[:, :ng].T, gb_row, out_norm.reshape(1, GROUP_W))


def _ssd_body(zc_ref, zl_ref, xc_ref, xl_ref, drc_ref, drl_ref, cw_ref, cb_ref, dbr_ref,
              alr_ref, dsk_ref, ng_ref, oc_ref, ol_ref,
              xac, xal, ysc, ysl, xpad, btc, btl, *, need_ctx):
    L = SCAN_CHUNK
    N = SSD_STATE
    lower, upper = _tri_masks(L)
    tri_lo = jnp.where(lower, 1.0, 0.0).astype(bf16)
    tri_up = jnp.where(upper, 1.0, 0.0).astype(bf16)
    a_row = -jnp.exp(alr_ref[...])
    pad = 8
    half = SSD_CONV // 2

    def conv_act(x_ref, xa_ref):
        n = x_ref.shape[0] // SCAN_BATCH
        zeros = jnp.zeros((pad, SSD_XBC), f32)
        blk = 256
        for g in range(SCAN_BATCH):
            xpad[0:pad, :] = zeros
            xpad[pad + n:2 * pad + n, :] = zeros
            for r in range(0, n, blk):
                xpad[pad + r:pad + r + blk, :] = x_ref[g * n + r:g * n + r + blk, :].astype(f32)
            for r in range(0, n, blk):
                y = jnp.zeros((blk, SSD_XBC), f32) + cb_ref[...]
                for kk in range(SSD_CONV):
                    o = pad + r + kk - half
                    y = y + cw_ref[kk:kk + 1, :] * xpad[o:o + blk, :]
                xa_ref[g * n + r:g * n + r + blk, :] = _silu(y)

    conv_act(xc_ref, xac)

    PW = 2 * HEAD_DIM
    lane = lax.broadcasted_iota(jnp.int32, (1, PW), 1)
    half0 = lane < HEAD_DIM
    row_half0 = lax.broadcasted_iota(jnp.int32, (PW, 1), 0) < N
    rr = lax.broadcasted_iota(jnp.int32, (PW, PW), 0)
    cc = lax.broadcasted_iota(jnp.int32, (PW, PW), 1)
    eye = jnp.where(rr == cc, 1.0, 0.0).astype(bf16)
    zero_b = jnp.zeros((), bf16)

    def transpose_b(xa_ref, bt_ref):
        for r in range(0, xa_ref.shape[0], L):
            bm = xa_ref[r:r + L, GROUP_W:GROUP_W + PW].astype(bf16)
            bt_ref[:, r:r + L] = _dot_nt(eye, bm).astype(bf16)

    transpose_b(xac, btc)

    def direction(xa_ref, bt_ref, dr_ref, j, rev, states, off):
        r0 = pl.multiple_of(off + j * L, L)
        base = HEADS if rev else 0
        mask = upper if rev else lower
        dtr = _softplus(dr_ref[:, pl.ds(r0, L)] + dbr_ref[...])
        ar = dtr * a_row
        cumr = _dot_tri(ar, tri_lo if rev else tri_up)
        cumc = jnp.transpose(cumr)
        cum_end = jnp.broadcast_to(jnp.sum(ar[base:base + HEADS, :], axis=1, keepdims=True), (HEADS, L))
        dec_all = jnp.exp(cum_end - cumr[base:base + HEADS, :]) * dtr[base:base + HEADS, :]
        aexp_all = jnp.exp(cum_end)
        bp = xa_ref[pl.ds(r0, L), GROUP_W:GROUP_W + PW].astype(bf16)
        cp = xa_ref[pl.ds(r0, L), GROUP_W + PW:GROUP_W + 2 * PW].astype(bf16)
        btp = bt_ref[:, pl.ds(r0, L)].astype(f32)
        new_states, outs = [], []
        for g in range(SSD_GROUPS):
            G = _dot_nt(jnp.where(half0 if g == 0 else ~half0, cp, zero_b), bp)
            xpb = xa_ref[pl.ds(r0, L), g * PW:(g + 1) * PW].astype(bf16)
            uu = _dot(jnp.concatenate([btp * dec_all[2 * g:2 * g + 1, :], btp * dec_all[2 * g + 1:2 * g + 2, :]],
                                      axis=0).astype(bf16), xpb)
            ws, ecs, us = [], [], []
            for e in range(2):
                h = 2 * g + e
                idx = base + h
                cum_cb = jnp.broadcast_to(cumc[:, idx:idx + 1], (L, L))
                seg = jnp.exp(jnp.where(mask, cum_cb - cumr[idx:idx + 1, :], NEG))
                ws.append((G * seg * dtr[idx:idx + 1, :]).astype(bf16))
                ecs.append(jnp.exp(cum_cb))
                us.append(aexp_all[h:h + 1, :] * states[g] + uu[e * PW:(e + 1) * PW])
            xbd = jnp.concatenate([jnp.where(half0, xpb, zero_b), jnp.where(half0, zero_b, xpb)], axis=0)
            y = _dot(jnp.concatenate(ws, axis=1), xbd) + jnp.where(half0, ecs[0], ecs[1]) * _dot(cp, states[g].astype(bf16))
            outs.append(y)
            new_states.append(jnp.where(row_half0 if g == 0 else ~row_half0, jnp.where(half0, us[0], us[1]), 0.0))
        return new_states, jnp.concatenate(outs, axis=1)

    def segment(xa_ref, bt_ref, dr_ref, ys_ref, carry):
        seg = xa_ref.shape[0] // SCAN_BATCH
        n = seg // L

        def body(i, carry):
            out, pieces = [], []
            for g, (fw, bw) in enumerate(carry):
                fw, y_f = direction(xa_ref, bt_ref, dr_ref, i, False, fw, g * seg)
                bw, y_b = direction(xa_ref, bt_ref, dr_ref, n - 1 - i, True, bw, g * seg)
                out.append((fw, bw))
                pieces += [y_f, y_b]
            ys_ref[i] = jnp.concatenate(pieces, axis=1)
            return tuple(out)

        return lax.fori_loop(0, n, body, carry, unroll=SCAN_UNROLL)

    init = [jnp.zeros((PW, PW), f32) for _ in range(SSD_GROUPS)]
    carry = segment(xac, btc, drc_ref, ysc, tuple((init, list(init)) for _ in range(SCAN_BATCH)))
    conv_act(xl_ref, xal)
    transpose_b(xal, btl)
    segment(xal, btl, drl_ref, ysl, carry)

    gmat = _group_mean_matrix(GROUP_W, SSD_NORM_GROUP)

    def finish(z_ref, xa_ref, ys_ref, o_ref):
        seg = o_ref.shape[0] // SCAN_BATCH
        n = seg // L
        for g in range(SCAN_BATCH):
            for j in range(n):
                r = g * seg + j * L
                c0 = g * 2 * GROUP_W
                yy = ys_ref[j, :, c0:c0 + GROUP_W] + ys_ref[n - 1 - j, :, c0 + GROUP_W:c0 + 2 * GROUP_W] \
                    + dsk_ref[...] * xa_ref[r:r + L, 0:GROUP_W]
                gt = yy * _silu(z_ref[r:r + L, :].astype(f32))
                o_ref[r:r + L, :] = (gt * _group_rsqrt(gt, gmat) * ng_ref[...]).astype(o_ref.dtype)

    finish(zl_ref, xal, ysl, ol_ref)
    if need_ctx:
        finish(zc_ref, xac, ysc, oc_ref)
    else:
        oc_ref[...] = jnp.zeros(oc_ref.shape, oc_ref.dtype)


def _ssd(z_c, z_l, xbc_c, xbc_l, dt_c, dt_l, conv_w, conv_b, dt_bias, a_log, d_skip, norm_g, batch, need_ctx):
    cl, t = z_c.shape[0] // batch, z_l.shape[0] // batch
    nd = 2 * HEADS
    bc, bt = SCAN_BATCH * cl, SCAN_BATCH * t
    cw = jnp.zeros((8, SSD_XBC), f32).at[:SSD_CONV].set(conv_w)
    zc_spec, zl_spec = _seg_blocks(bc, bt, GROUP_W)
    xc_spec, xl_spec = _seg_blocks(bc, bt, SSD_XBC)
    drc_spec, drl_spec = _seg_blocks_t(bc, bt, nd)
    oc_spec, ol_spec = _seg_blocks(bc, bt, GROUP_W)
    return pl.pallas_call(
        functools.partial(_ssd_body, need_ctx=need_ctx),
        grid=(batch // SCAN_BATCH,),
        in_specs=[zc_spec, zl_spec, xc_spec, xl_spec, drc_spec, drl_spec,
                  _resident((8, SSD_XBC)), _resident((1, SSD_XBC)), _resident((nd, 1)), _resident((nd, 1)),
                  _resident((1, GROUP_W)), _resident((1, GROUP_W))],
        out_specs=[oc_spec, ol_spec],
        out_shape=[jax.ShapeDtypeStruct((batch * cl, GROUP_W), bf16),
                   jax.ShapeDtypeStruct((batch * t, GROUP_W), bf16)],
        scratch_shapes=[pltpu.VMEM((bc, SSD_XBC), f32), pltpu.VMEM((bt, SSD_XBC), f32),
                        pltpu.VMEM((cl // SCAN_CHUNK, SCAN_CHUNK, 2 * SCAN_BATCH * GROUP_W), f32),
                        pltpu.VMEM((t // SCAN_CHUNK, SCAN_CHUNK, 2 * SCAN_BATCH * GROUP_W), f32),
                        pltpu.VMEM((t + 16, SSD_XBC), f32),
                        pltpu.VMEM((2 * SSD_STATE, bc), bf16), pltpu.VMEM((2 * SSD_STATE, bt), bf16)],
        compiler_params=_params(("parallel",)),
        name="ssd",
    )(z_c, z_l, xbc_c, xbc_l, dt_c[:, :nd].T, dt_l[:, :nd].T, cw, conv_b.reshape(1, SSD_XBC),
      dt_bias.reshape(nd, 1), a_log.reshape(nd, 1),
      jnp.repeat(d_skip, HEAD_DIM).reshape(1, GROUP_W), norm_g.reshape(1, GROUP_W))


def _mla_body(ac_ref, al_ref, qn_ref, kvn_ref, wq_ref, wkv_ref, qg_ref, kg_ref, krg_ref, cos_ref, sin_ref,
              oc_ref, ol_ref, q_s, k_s, vt_s, *, need_ctx):
    cl, t = ac_ref.shape[0], al_ref.shape[0]
    scale = MLA_QK ** -0.5
    rr = lax.broadcasted_iota(jnp.int32, (LANES, LANES), 0)
    cc = lax.broadcasted_iota(jnp.int32, (LANES, LANES), 1)
    in_nope = (rr < MLA_NOPE) & (cc < MLA_NOPE)
    in_rope = (rr >= MLA_NOPE) & (rr < MLA_QK) & (cc >= MLA_NOPE) & (cc < MLA_QK)
    head_gmat = jnp.where(in_nope, 1.0 / MLA_NOPE, jnp.where(in_rope, 1.0 / MLA_ROPE, 0.0)).astype(bf16)
    rot = _rotate_half_matrix(MLA_ROPE // 4)
    er = lax.broadcasted_iota(jnp.int32, (GROUP_W, GROUP_W), 0)
    ec = lax.broadcasted_iota(jnp.int32, (GROUP_W, GROUP_W), 1)
    eye = jnp.where(er == ec, 1.0, 0.0).astype(bf16)

    def head_norm(x):
        return x * _group_rsqrt(x, head_gmat)

    def project(a_ref, row0, n, roped):
        blk = 256
        for r in range(0, n, blk):
            a = a_ref[r:r + blk, :].astype(f32)
            aq, akv, akr = a[:, 0:GROUP_W], a[:, GROUP_W:GROUP_W + LANES], a[:, GROUP_W + LANES:GROUP_W + 2 * LANES]
            qh = _dot((aq * _row_rsqrt(aq, GROUP_W) * qn_ref[...]).astype(bf16), wq_ref[...])
            kv = _dot((akv * _row_rsqrt(akv, kvn_ref.shape[1]) * kvn_ref[...]).astype(bf16), wkv_ref[...])
            kr = akr * _row_rsqrt(akr, MLA_ROPE) * krg_ref[...]
            if roped:
                cos, sin = cos_ref[r:r + blk, :], sin_ref[r:r + blk, :]
                kr = _rope(kr, cos, sin, rot)
            for h in range(HEADS):
                qx = head_norm(qh[:, h * LANES:(h + 1) * LANES]) * qg_ref[...]
                if roped:
                    qx = _rope(qx, cos, sin, rot)
                q_s[h, row0 + r:row0 + r + blk, :] = (qx * scale).astype(bf16)
                kx = kv[:, h * LANES:(h + 1) * LANES]
                kx = kx * _row_rsqrt(kx, MLA_NOPE) * kg_ref[...]
                k_s[h, row0 + r:row0 + r + blk, :] = (kx + kr).astype(bf16)
            vt_s[:, row0 + r:row0 + r + blk] = _dot_nt(eye, kv[:, HEADS * LANES:].astype(bf16)).astype(bf16)

    project(ac_ref, 0, cl, False)
    project(al_ref, cl, t, True)

    def attend(q0, nq, nk, o_ref, o0):
        outs = []
        sts = [_dot_nt(k_s[h, 0:nk, :], q_s[h, pl.ds(q0, nq), :]) for h in range(HEADS)]
        for h in range(HEADS):
            st = sts[h]
            p = jnp.exp(st - jnp.max(st, axis=0, keepdims=True))
            den = jnp.sum(p, axis=0, keepdims=True)
            outs.append(_dot(vt_s[h * HEAD_DIM:(h + 1) * HEAD_DIM, 0:nk], p.astype(bf16)) / den)
        ot = jnp.concatenate(outs, axis=0).astype(bf16)
        for c in range(0, nq, GROUP_W):
            o_ref[pl.ds(o0 + c, GROUP_W), :] = _dot_nt(eye, ot[:, c:c + GROUP_W]).astype(o_ref.dtype)

    def body(i, _):
        o0 = pl.multiple_of(i * MLA_QBLK, MLA_QBLK)
        attend(cl + o0, MLA_QBLK, cl + t, ol_ref, o0)
        return 0

    lax.fori_loop(0, t // MLA_QBLK, body, 0)
    if need_ctx:
        attend(0, cl, cl, oc_ref, 0)
    else:
        oc_ref[...] = jnp.zeros(oc_ref.shape, oc_ref.dtype)


def _mla(a_c, a_l, q_norm, kv_norm, wq_b, wkv_b, q_gain, k_gain, cos, sin, batch, need_ctx):
    cl, t = a_c.shape[0] // batch, a_l.shape[0] // batch
    kvl = kv_norm.shape[0]
    wq = jnp.zeros((GROUP_W, HEADS, LANES), f32).at[:, :, :MLA_QK].set(wq_b.reshape(GROUP_W, HEADS, MLA_QK))
    wkv4 = wkv_b.reshape(kvl, HEADS, MLA_NOPE + HEAD_DIM)
    wk = jnp.zeros((kvl, HEADS, LANES), f32).at[:, :, :MLA_NOPE].set(wkv4[:, :, :MLA_NOPE])
    wkv = jnp.concatenate([wk.reshape(kvl, HEADS * LANES), wkv4[:, :, MLA_NOPE:].reshape(kvl, GROUP_W)], axis=1)
    slab = lambda v, off: jnp.zeros((1, LANES), f32).at[0, off:off + v.shape[0]].set(v)
    qg = slab(q_gain, 0)
    kg = slab(k_gain[:MLA_NOPE], 0)
    krg = slab(k_gain[MLA_NOPE:], MLA_NOPE)
    ac_spec, al_spec = _seg_blocks(cl, t, 2 * GROUP_W)
    oc_spec, ol_spec = _seg_blocks(cl, t, GROUP_W)
    return pl.pallas_call(
        functools.partial(_mla_body, need_ctx=need_ctx),
        grid=(batch,),
        in_specs=[ac_spec, al_spec, _resident((1, GROUP_W)), _resident((1, LANES)),
                  _resident((GROUP_W, HEADS * LANES)), _resident((kvl, HEADS * LANES + GROUP_W)),
                  _resident((1, LANES)), _resident((1, LANES)), _resident((1, LANES)),
                  _resident((t, LANES)), _resident((t, LANES))],
        out_specs=[oc_spec, ol_spec],
        out_shape=[jax.ShapeDtypeStruct((batch * cl, GROUP_W), bf16),
                   jax.ShapeDtypeStruct((batch * t, GROUP_W), bf16)],
        scratch_shapes=[pltpu.VMEM((HEADS, cl + t, LANES), bf16), pltpu.VMEM((HEADS, cl + t, LANES), bf16),
                        pltpu.VMEM((GROUP_W, cl + t), bf16)],
        compiler_params=_params(("parallel",)),
        name="mla",
    )(a_c, a_l, q_norm.reshape(1, GROUP_W), kv_norm.reshape(1, kvl),
      wq.reshape(GROUP_W, HEADS * LANES).astype(bf16), wkv.astype(bf16), qg, kg, krg, cos, sin)


def _swa_body(wc_ref, wl_ref, qg_ref, kg_ref, sink_ref, cos_ref, sin_ref, oc_ref, ol_ref,
              q_s, qc_s, k_s, v_s, kc_s, vc_s, *, need_ctx):
    cl, t = wc_ref.shape[0], wl_ref.shape[0]
    scale = HEAD_DIM ** -0.5
    kvw = SWA_KV_HEADS * HEAD_DIM
    blk = SWA_BLK
    gmat_q = _group_mean_matrix(GROUP_W, HEAD_DIM)
    gmat_k = _group_mean_matrix(kvw, HEAD_DIM)
    rot = _rotate_half_matrix(HEAD_DIM // 4)

    def project(w_ref, n, roped, qdst, kdst, vdst, row0):
        step = 256
        for r in range(0, n, step):
            w = w_ref[r:r + step, :].astype(f32)
            q, k = w[:, 0:GROUP_W], w[:, GROUP_W:GROUP_W + kvw]
            q = q * _group_rsqrt(q, gmat_q) * qg_ref[...]
            k = k * _group_rsqrt(k, gmat_k) * kg_ref[...]
            if roped:
                cos, sin = cos_ref[r:r + step, :], sin_ref[r:r + step, :]
                q = jnp.concatenate([_rope(q[:, 0:LANES], cos, sin, rot),
                                     _rope(q[:, LANES:2 * LANES], cos, sin, rot)], axis=1)
                k = _rope(k, cos, sin, rot)
            qdst[r:r + step, :] = (q * scale).astype(bf16)
            kdst[row0 + r:row0 + r + step, :] = k.astype(bf16)
            vdst[row0 + r:row0 + r + step, :] = w_ref[r:r + step, GROUP_W + kvw:GROUP_W + 2 * kvw]

    zeros = jnp.zeros((blk, kvw), bf16)
    for s in (k_s, v_s):
        s[0:blk, :] = zeros
        s[blk + t:2 * blk + t, :] = zeros
    project(wc_ref, cl, False, qc_s, kc_s, vc_s, 0)
    project(wl_ref, t, True, q_s, k_s, v_s, blk)

    rr = lax.broadcasted_iota(jnp.int32, (HEADS * blk, 3 * blk), 0) % blk
    jj = lax.broadcasted_iota(jnp.int32, (HEADS * blk, 3 * blk), 1)
    band = (jj - rr >= 0) & (jj - rr <= 2 * SWA_WINDOW)
    half0 = lax.broadcasted_iota(jnp.int32, (1, LANES), 1) < HEAD_DIM
    zero_b = jnp.zeros((), bf16)

    def wide(x, n):
        return jnp.broadcast_to(x, (x.shape[0], n))

    def stacked_queries(q_ref, r0, n):
        qa, qb = q_ref[pl.ds(r0, n), 0:LANES], q_ref[pl.ds(r0, n), LANES:2 * LANES]
        return jnp.concatenate([jnp.where(half0, qa, zero_b), jnp.where(half0, qb, zero_b),
                                jnp.where(half0, zero_b, qa), jnp.where(half0, zero_b, qb)], axis=0)

    def stacked_sink(n):
        row = lax.broadcasted_iota(jnp.int32, (HEADS * n, 1), 0)
        return jnp.where(row < n, sink_ref[0:1, :], jnp.where(row < 2 * n, sink_ref[1:2, :],
                         jnp.where(row < 3 * n, sink_ref[2:3, :], sink_ref[3:4, :])))

    def unstack(o, n):
        return jnp.concatenate([jnp.where(half0, o[0:n], o[2 * n:3 * n]),
                                jnp.where(half0, o[n:2 * n], o[3 * n:4 * n])], axis=1)

    def lat_block(n, _):
        r0 = pl.multiple_of(n * blk, blk)
        kpos = jj + (n - 1) * blk
        valid = band & (kpos >= 0) & (kpos < t)
        q4 = stacked_queries(q_s, r0, blk)
        s = _dot_nt(q4, jnp.concatenate([k_s[pl.ds(r0, 3 * blk), :], kc_s[...]], axis=0))
        s_loc, s_ctx = jnp.where(valid, s[:, 0:3 * blk], NEG), s[:, 3 * blk:]
        sink = stacked_sink(blk)
        m = jnp.maximum(wide(jnp.maximum(jnp.max(s_loc, axis=1, keepdims=True),
                                         jnp.max(s_ctx, axis=1, keepdims=True)), LANES), sink)
        p_loc = jnp.exp(s_loc - jnp.concatenate([m] * 3, axis=1))
        p_ctx = jnp.exp(s_ctx - jnp.concatenate([m] * (cl // LANES), axis=1))
        den = wide(jnp.sum(p_loc, axis=1, keepdims=True) + jnp.sum(p_ctx, axis=1, keepdims=True), LANES) \
            + jnp.exp(sink - m)
        o = _dot(jnp.concatenate([p_loc, p_ctx], axis=1).astype(bf16),
                 jnp.concatenate([v_s[pl.ds(r0, 3 * blk), :], vc_s[...]], axis=0))
        ol_ref[pl.ds(r0, blk), :] = unstack(o / den, blk).astype(ol_ref.dtype)
        return 0

    lax.fori_loop(0, t // blk, lat_block, 0, unroll=SWA_UNROLL)

    if need_ctx:
        half = cl // 2
        for r0 in (0, half):
            s = _dot_nt(stacked_queries(qc_s, r0, half), kc_s[...])
            sink = stacked_sink(half)
            m = jnp.maximum(wide(jnp.max(s, axis=1, keepdims=True), LANES), sink)
            p = jnp.exp(s - jnp.concatenate([m] * (cl // LANES), axis=1))
            den = wide(jnp.sum(p, axis=1, keepdims=True), LANES) + jnp.exp(sink - m)
            oc_ref[r0:r0 + half, :] = unstack(_dot(p.astype(bf16), vc_s[...]) / den, half).astype(oc_ref.dtype)
    else:
        oc_ref[...] = jnp.zeros(oc_ref.shape, oc_ref.dtype)


def _swa_head_order(a, axis):
    blocks = jnp.split(a, HEADS, axis=axis)
    return jnp.concatenate([blocks[0], blocks[2], blocks[1], blocks[3]], axis=axis)


def _swa(w_c, w_l, q_gain, k_gain, sink, cos, sin, batch, need_ctx):
    cl, t = w_c.shape[0] // batch, w_l.shape[0] // batch
    kvw = SWA_KV_HEADS * HEAD_DIM
    wc_spec, wl_spec = _seg_blocks(cl, t, 2 * GROUP_W)
    oc_spec, ol_spec = _seg_blocks(cl, t, GROUP_W)
    return pl.pallas_call(
        functools.partial(_swa_body, need_ctx=need_ctx),
        grid=(batch,),
        in_specs=[wc_spec, wl_spec, _resident((1, GROUP_W)), _resident((1, kvw)), _resident((HEADS, LANES)),
                  _resident((t, LANES)), _resident((t, LANES))],
        out_specs=[oc_spec, ol_spec],
        out_shape=[jax.ShapeDtypeStruct((batch * cl, GROUP_W), bf16),
                   jax.ShapeDtypeStruct((batch * t, GROUP_W), bf16)],
        scratch_shapes=[pltpu.VMEM((t, GROUP_W), bf16), pltpu.VMEM((cl, GROUP_W), bf16),
                        pltpu.VMEM((t + 2 * SWA_BLK, kvw), bf16), pltpu.VMEM((t + 2 * SWA_BLK, kvw), bf16),
                        pltpu.VMEM((cl, kvw), bf16), pltpu.VMEM((cl, kvw), bf16)],
        compiler_params=_params(("parallel",)),
        name="swa",
    )(w_c, w_l, jnp.tile(q_gain, HEADS).reshape(1, GROUP_W), jnp.tile(k_gain, SWA_KV_HEADS).reshape(1, kvw),
      jnp.broadcast_to(sink.reshape(HEADS, 1), (HEADS, LANES)), cos, sin)


def _rope_tables(t, rot_dim, lane0):
    pos = jnp.arange(t)
    row, col = (pos // GRID_W).astype(f32), (pos % GRID_W).astype(f32)
    nf = rot_dim // 4
    inv = ROPE_BASE ** (-jnp.arange(nf, dtype=f32) / nf)
    ar, ac = row[:, None] * inv, col[:, None] * inv
    ang = jnp.concatenate([ar, ar, ac, ac], axis=-1)
    return jnp.cos(ang), jnp.sin(ang)


def _pack_in_weight(w_in):
    o = 0
    seg = {}
    for name, n in (("qkvo", 4 * GROUP_W), ("mg", 4 * HEADS), ("a_q", GROUP_W), ("a_kv", GROUP_W // 2),
                    ("a_kr", MLA_ROPE), ("swa", 2 * GROUP_W), ("z", GROUP_W), ("xbc", SSD_XBC), ("dt", 2 * HEADS)):
        seg[name] = w_in[:, o:o + n]
        o += n
    d = w_in.shape[0]
    zeros = lambda n: jnp.zeros((d, n), w_in.dtype)
    packed = jnp.concatenate([
        seg["qkvo"], seg["mg"], zeros(LANES - 4 * HEADS),
        seg["a_q"], seg["a_kv"], zeros(MLA_NOPE), seg["a_kr"], zeros(LANES - MLA_QK),
        _swa_head_order(seg["swa"][:, :GROUP_W], 1), seg["swa"][:, GROUP_W:],
        seg["z"], seg["xbc"], seg["dt"], zeros(LANES - 2 * HEADS)], axis=1)
    return packed.astype(bf16)


def kernel(x, c, ctx, c_ctx, w_mod, b_mod, ffn1_norm, ffn1_wi, ffn1_wo, mix_norm, w_in, w_out, mlstm_gate_b, mlstm_out_norm, mla_q_norm, mla_kv_norm, mla_wq_b, mla_wkv_b, mla_q_gain, mla_k_gain, swa_q_gain, swa_k_gain, swa_sink, ssd_conv_w, ssd_conv_b, ssd_dt_bias, ssd_a_log, ssd_d, ssd_norm, ffn2_norm, ffn2_wi, ffn2_wo):
    b, t, d = x.shape
    cl = ctx.shape[1]
    depth = w_mod.shape[0]
    dff = ffn1_wo.shape[1]
    lat_tiles = t // ROW_TILE

    cos_m, sin_m = _rope_tables(t, MLA_ROPE, MLA_NOPE)
    pad_id = lambda tab, fill: jnp.concatenate(
        [jnp.full((t, MLA_NOPE), fill, f32), tab, jnp.full((t, LANES - MLA_QK), fill, f32)], axis=1)
    cos_m, sin_m = pad_id(cos_m, 1.0), pad_id(sin_m, 0.0)
    cos_s, sin_s = _rope_tables(t, HEAD_DIM, 0)
    cos_s, sin_s = jnp.tile(cos_s, (1, LANES // HEAD_DIM)), jnp.tile(sin_s, (1, LANES // HEAD_DIM))

    h = x.reshape(b * t, d)
    hc = ctx.reshape(b * cl, d)
    cc = jnp.concatenate([c, c_ctx[None, :]], axis=0)
    for l in range(depth):
        need_ctx = l < depth - 1
        mod = _modulation(cc, w_mod[l], b_mod[l]).reshape(b + 1, N_MOD, d)
        streams = ((0, lat_tiles), (b, None))

        ffn1_w = (ffn1_norm[l].reshape(1, d), ffn1_wi[l][:, :dff].astype(bf16), ffn1_wi[l][:, dff:].astype(bf16),
                  ffn1_wo[l].astype(bf16))
        ffn2_w = (ffn2_norm[l].reshape(1, d), ffn2_wi[l][:, :dff].astype(bf16), ffn2_wi[l][:, dff:].astype(bf16),
                  ffn2_wo[l].astype(bf16))
        h = _ffn(h, mod, *ffn1_w, 0, *streams[0])
        hc = _ffn(hc, mod, *ffn1_w, 0, *streams[1])
        w_packed = _pack_in_weight(w_in[l])
        qkvo_l, mg_l, mla_l, swa_l, z_l, xbc_l, dt_l = _inproj(h, mod, mix_norm[l].reshape(1, d), w_packed, *streams[0])
        qkvo_c, mg_c, mla_c, swa_c, z_c, xbc_c, dt_c = _inproj(hc, mod, mix_norm[l].reshape(1, d), w_packed, *streams[1])
        a_c, a_l = _mlstm(qkvo_c, qkvo_l, mg_c, mg_l, mlstm_gate_b[l], mlstm_out_norm[l], b, need_ctx)
        m_c, m_l = _mla(mla_c, mla_l, mla_q_norm[l], mla_kv_norm[l], mla_wq_b[l], mla_wkv_b[l], mla_q_gain[l],
                        mla_k_gain[l], cos_m, sin_m, b, need_ctx)
        s_c, s_l = _swa(swa_c, swa_l, swa_q_gain[l], swa_k_gain[l], swa_sink[l], cos_s, sin_s, b, need_ctx)
        d_c, d_l = _ssd(z_c, z_l, xbc_c, xbc_l, dt_c, dt_l, ssd_conv_w[l], ssd_conv_b[l], ssd_dt_bias[l],
                        ssd_a_log[l], ssd_d[l], ssd_norm[l], b, need_ctx)
        wo_b = jnp.concatenate([w_out[l][:2 * GROUP_W], _swa_head_order(w_out[l][2 * GROUP_W:3 * GROUP_W], 0),
                                w_out[l][3 * GROUP_W:]], axis=0).astype(bf16)
        h = _mix_ffn(h, mod, (a_l, m_l, s_l, d_l), wo_b, *ffn2_w, *streams[0])
        if need_ctx:
            hc = _mix_ffn(hc, mod, (a_c, m_c, s_c, d_c), wo_b, *ffn2_w, *streams[1])
    return h.reshape(b, t, d)
```

```python
import functools

import jax
import jax.numpy as jnp
from jax import lax
from jax.experimental import pallas as pl
from jax.experimental.pallas import tpu as pltpu

f32 = jnp.float32
bf16 = jnp.bfloat16

RMS_EPS = 1e-6
ROPE_BASE = 10000.0
GRID_W = 64
N_MOD = 9
HEADS = 4
HEAD_DIM = 64
GROUP_W = HEADS * HEAD_DIM
MLA_NOPE = 64
MLA_ROPE = 32
MLA_QK = MLA_NOPE + MLA_ROPE
SWA_KV_HEADS = 2
SWA_WINDOW = 128
SWA_BLK = 128
SSD_STATE = 64
SSD_GROUPS = 2
SSD_CONV = 5
SSD_XBC = GROUP_W + 2 * SSD_GROUPS * SSD_STATE
SSD_NORM_GROUP = 128

LANES = 128
ROW_TILE = 512
FF_CHUNK = 256
SCAN_CHUNK = 128
SCAN_UNROLL = 2
SWA_UNROLL = 4
SCAN_BATCH = 2
MLA_QBLK = 512
NEG = -1e30
LOG2E = 1.4426950408889634
VMEM_LIMIT = 56 * 1024 * 1024

P_QKVO = 0
P_MLA = 1024
P_SWA = 1536
P_Z = 2048
P_XBC = 2304
P_COLS = 2816
GT_MG = 0
GT_DT = 4 * HEADS
GT_ROWS = 32


def _dot(a, b):
    return jnp.dot(a, b, preferred_element_type=f32)


def _dot_nt(a, b):
    return lax.dot_general(a, b, (((1,), (1,)), ((), ())), preferred_element_type=f32)


def _sigmoid(x):
    return 1.0 / (1.0 + jnp.exp(-x))


def _silu(x):
    return x * _sigmoid(x)


def _softplus(x):
    return jnp.maximum(x, 0.0) + jnp.log(1.0 + jnp.exp(-jnp.abs(x)))


def _log_sigmoid(x):
    return -_softplus(-x)


def _split3(x):
    hi = x.astype(bf16)
    r = x - hi.astype(f32)
    mid = r.astype(bf16)
    lo = (r - mid.astype(f32)).astype(bf16)
    return hi, mid, lo


def _dot_tri(x, tri):
    m = x.shape[0]
    terms = jnp.concatenate([p.astype(f32) for p in _split3(x)], axis=0).astype(bf16)
    y = _dot(terms, tri)
    return y[0:m] + y[m:2 * m] + y[2 * m:3 * m]


def _dot2(x, w):
    hi = x.astype(bf16)
    mid = (x - hi.astype(f32)).astype(bf16)
    return _dot(hi, w) + _dot(mid, w)


def _group_mean_matrix(n, gsz):
    r = lax.broadcasted_iota(jnp.int32, (n, n), 0) // gsz
    c = lax.broadcasted_iota(jnp.int32, (n, n), 1) // gsz
    return jnp.where(r == c, 1.0 / gsz, 0.0).astype(bf16)


def _group_rsqrt(x, gmat):
    return lax.rsqrt(_dot2(x * x, gmat) + RMS_EPS)


def _row_rsqrt(x, n_real):
    return lax.rsqrt(jnp.sum(x * x, axis=1, keepdims=True) * (1.0 / n_real) + RMS_EPS)


def _rotate_half_matrix(quarter):
    r = lax.broadcasted_iota(jnp.int32, (LANES, LANES), 0)
    c = lax.broadcasted_iota(jnp.int32, (LANES, LANES), 1)
    first = (c % (2 * quarter)) < quarter
    return jnp.where(first & (r == c + quarter), -1.0, jnp.where(~first & (r == c - quarter), 1.0, 0.0)).astype(bf16)


def _rope(x, cos, sin, rot):
    return x * cos + _dot2(x, rot) * sin


def _resident(shape):
    nd = len(shape)
    return pl.BlockSpec(shape, lambda *_: (0,) * nd, pipeline_mode=pl.Buffered(1))


def _params(sem):
    return pltpu.CompilerParams(dimension_semantics=sem, vmem_limit_bytes=VMEM_LIMIT)


def _mod_body(c_ref, w_ref, b_ref, o_ref):
    s = _silu(c_ref[...])
    o_ref[...] = jnp.dot(s, w_ref[...], preferred_element_type=f32,
                         precision=lax.Precision.HIGHEST) + b_ref[...]


def _modulation(cc, w_all, b_all, layer):
    m, d = cc.shape
    depth, _, n = w_all.shape
    tn = 1024
    return pl.pallas_call(
        _mod_body,
        grid=(n // tn,),
        in_specs=[pl.BlockSpec((m, d), lambda j: (0, 0)),
                  pl.BlockSpec((None, d, tn), lambda j: (layer, 0, j)),
                  pl.BlockSpec((None, 1, tn), lambda j: (layer, 0, j))],
        out_specs=pl.BlockSpec((m, tn), lambda j: (0, j)),
        out_shape=jax.ShapeDtypeStruct((m, n), f32),
        compiler_params=_params(("arbitrary",)),
        name="modulation",
    )(cc, w_all, b_all.reshape(depth, 1, n))


def _norm_mod(x, gain, shift, scale):
    xn = x * _row_rsqrt(x, x.shape[1]) * gain
    return (xn * (1.0 + scale) + shift).astype(bf16)


def _swiglu_half_step(x, mod_ref, g_ref, wig_ref, wiu_ref, wo_ref, mi):
    xb = _norm_mod(x, g_ref[...], mod_ref[mi:mi + 1, :], mod_ref[mi + 1:mi + 2, :])
    acc = jnp.zeros(x.shape, f32)
    for c in range(wig_ref.shape[1] // FF_CHUNK):
        sl = slice(c * FF_CHUNK, (c + 1) * FF_CHUNK)
        g = _dot(xb, wig_ref[:, sl])
        u = _dot(xb, wiu_ref[:, sl])
        acc = acc + _dot((_silu(g) * u).astype(bf16), wo_ref[sl, :])
    return x + (0.5 * mod_ref[mi + 2:mi + 3, :]) * acc


def _ffn_body(x_ref, mod_ref, g_ref, wig_ref, wiu_ref, wo_ref, o_ref, *, mi):
    o_ref[...] = _swiglu_half_step(x_ref[...], mod_ref, g_ref, wig_ref, wiu_ref, wo_ref, mi)


def _mix_ffn_body(x_ref, mod_ref, a_ref, m_ref, w_ref, s_ref, wout_ref, g_ref, wig_ref, wiu_ref, wo_ref, o_ref):
    acc = _dot(a_ref[...], wout_ref[0:GROUP_W, :])
    acc = acc + _dot(m_ref[...], wout_ref[GROUP_W:2 * GROUP_W, :])
    acc = acc + _dot(w_ref[...], wout_ref[2 * GROUP_W:3 * GROUP_W, :])
    acc = acc + _dot(s_ref[...], wout_ref[3 * GROUP_W:4 * GROUP_W, :])
    x = x_ref[...] + mod_ref[5:6, :] * acc
    o_ref[...] = _swiglu_half_step(x, mod_ref, g_ref, wig_ref, wiu_ref, wo_ref, 6)


def _mod_spec(d, mod_base, tiles_per_mod):
    if tiles_per_mod is None:
        return pl.BlockSpec((None, N_MOD, d), lambda i: (mod_base, 0, 0))
    return pl.BlockSpec((None, N_MOD, d), lambda i: (mod_base + i // tiles_per_mod, 0, 0))


def _ffn(x, mod, gain, wig, wiu, wo, mi, mod_base, tiles_per_mod):
    rows, d = x.shape
    dff = wig.shape[1]
    return pl.pallas_call(
        functools.partial(_ffn_body, mi=mi),
        grid=(rows // ROW_TILE,),
        in_specs=[pl.BlockSpec((ROW_TILE, d), lambda i: (i, 0)),
                  _mod_spec(d, mod_base, tiles_per_mod),
                  _resident((1, d)), _resident((d, dff)), _resident((d, dff)), _resident((dff, d))],
        out_specs=pl.BlockSpec((ROW_TILE, d), lambda i: (i, 0)),
        out_shape=jax.ShapeDtypeStruct((rows, d), f32),
        compiler_params=_params(("parallel",)),
        name="ffn",
    )(x, mod, gain, wig, wiu, wo)


def _inproj_body(x_ref, mod_ref, g_ref, w_ref, wt_ref, qkvo_ref, mla_ref, swa_ref, z_ref, xbc_ref, gt_ref):
    xb = _norm_mod(x_ref[...], g_ref[...], mod_ref[3:4, :], mod_ref[4:5, :])
    qkvo_ref[...] = _dot(xb, w_ref[:, P_QKVO:P_MLA]).astype(bf16)
    mla_ref[...] = _dot(xb, w_ref[:, P_MLA:P_SWA]).astype(bf16)
    swa_ref[...] = _dot(xb, w_ref[:, P_SWA:P_Z]).astype(bf16)
    z_ref[...] = _dot(xb, w_ref[:, P_Z:P_XBC]).astype(bf16)
    xbc_ref[...] = _dot(xb, w_ref[:, P_XBC:P_COLS]).astype(bf16)
    gt_ref[...] = _dot_nt(wt_ref[...], xb)


def _inproj(x, mod, gain, w, wt, mod_base, tiles_per_mod):
    rows, d = x.shape
    widths = [P_MLA - P_QKVO, P_SWA - P_MLA, P_Z - P_SWA, P_XBC - P_Z, P_COLS - P_XBC]
    return pl.pallas_call(
        _inproj_body,
        grid=(rows // ROW_TILE,),
        in_specs=[pl.BlockSpec((ROW_TILE, d), lambda i: (i, 0)),
                  _mod_spec(d, mod_base, tiles_per_mod),
                  _resident((1, d)), _resident((d, P_COLS)), _resident((GT_ROWS, d))],
        out_specs=[pl.BlockSpec((ROW_TILE, n), lambda i: (i, 0)) for n in widths]
                  + [pl.BlockSpec((GT_ROWS, ROW_TILE), lambda i: (0, i))],
        out_shape=[jax.ShapeDtypeStruct((rows, n), bf16) for n in widths]
                  + [jax.ShapeDtypeStruct((GT_ROWS, rows), f32)],
        compiler_params=_params(("parallel",)),
        name="inproj",
    )(x, mod, gain, w, wt)


def _mix_ffn(x, mod, mixed, w_out, gain, wig, wiu, wo, mod_base, tiles_per_mod):
    rows, d = x.shape
    dff = wig.shape[1]
    return pl.pallas_call(
        _mix_ffn_body,
        grid=(rows // ROW_TILE,),
        in_specs=[pl.BlockSpec((ROW_TILE, d), lambda i: (i, 0)),
                  _mod_spec(d, mod_base, tiles_per_mod)]
                 + [pl.BlockSpec((ROW_TILE, GROUP_W), lambda i: (i, 0))] * 4
                 + [_resident(w_out.shape), _resident((1, d)), _resident((d, dff)), _resident((d, dff)),
                    _resident((dff, d))],
        out_specs=pl.BlockSpec((ROW_TILE, d), lambda i: (i, 0)),
        out_shape=jax.ShapeDtypeStruct((rows, d), f32),
        compiler_params=_params(("parallel",)),
        name="mix_ffn",
    )(x, mod, *mixed, w_out, gain, wig, wiu, wo)


def _tri_masks(n):
    r = lax.broadcasted_iota(jnp.int32, (n, n), 0)
    c = lax.broadcasted_iota(jnp.int32, (n, n), 1)
    return r >= c, r <= c


def _seg_blocks(cl, t, width):
    return (pl.BlockSpec((cl, width), lambda b: (b, 0)), pl.BlockSpec((t, width), lambda b: (b, 0)))


def _seg_blocks_t(cl, t, nrow, row0):
    blk = row0 // nrow
    return (pl.BlockSpec((nrow, cl), lambda b: (blk, b)), pl.BlockSpec((nrow, t), lambda b: (blk, b)))


def _mlstm_body(qc_ref, ql_ref, grc_ref, grl_ref, gbr_ref, on_ref, oc_ref, ol_ref,
                hsc, hsl, ktc, ktl, *, need_ctx):
    L = SCAN_CHUNK
    PW = 2 * HEAD_DIM
    lower, upper = _tri_masks(L)
    tri_lo = jnp.where(lower, 1.0, 0.0).astype(bf16)
    tri_up = jnp.where(upper, 1.0, 0.0).astype(bf16)
    lane = lax.broadcasted_iota(jnp.int32, (1, PW), 1)
    half0 = lane < HEAD_DIM
    row_half0 = lax.broadcasted_iota(jnp.int32, (PW, 1), 0) < HEAD_DIM
    rr = lax.broadcasted_iota(jnp.int32, (PW, PW), 0)
    cc = lax.broadcasted_iota(jnp.int32, (PW, PW), 1)
    rr2 = lax.broadcasted_iota(jnp.int32, (PW, 2 * PW), 0)
    cc2 = lax.broadcasted_iota(jnp.int32, (PW, 2 * PW), 1) % PW
    blockdiag2 = (rr2 < HEAD_DIM) == (cc2 < HEAD_DIM)
    eye = jnp.where(rr == cc, 1.0, 0.0).astype(bf16)
    ones_bd = jnp.concatenate([jnp.where(half0, 1.0, 0.0) * jnp.ones((L, 1), f32),
                               jnp.where(half0, 0.0, 1.0) * jnp.ones((L, 1), f32)], axis=0).astype(bf16)
    ones_full = jnp.ones((L, PW), bf16)
    zero_b = jnp.zeros((), bf16)

    def transpose_keys(q_ref, kt_ref):
        for r in range(0, q_ref.shape[0], L):
            for p in range(HEADS // 2):
                k = q_ref[r:r + L, GROUP_W + p * PW:GROUP_W + (p + 1) * PW] * (HEAD_DIM ** -0.5)
                kt_ref[p * PW:(p + 1) * PW, r:r + L] = _dot_nt(eye, k).astype(bf16)

    transpose_keys(qc_ref, ktc)
    transpose_keys(ql_ref, ktl)

    def direction(q_ref, kt_ref, gr_ref, j, rev, carry, off):
        r0 = pl.multiple_of(off + j * L, L)
        base = 2 * HEADS if rev else 0
        mask = upper if rev else lower
        grow = gr_ref[:, pl.ds(r0, L)] + gbr_ref[...]
        lfr = _log_sigmoid(grow)
        brows = _dot_tri(lfr, tri_lo if rev else tri_up)
        bcols = jnp.transpose(brows)
        r_all = grow[base:base + HEADS, :] - brows[base + HEADS:base + 2 * HEADS, :]
        wide = lambda x: jnp.broadcast_to(x, (HEADS, L))
        b_end = wide(jnp.sum(lfr[base + HEADS:base + 2 * HEADS, :], axis=1, keepdims=True))
        cns, m_all = carry
        mm = jnp.maximum(m_all, wide(jnp.max(r_all, axis=1, keepdims=True)))
        a_all = jnp.exp(m_all - mm)
        w_all = jnp.exp(r_all - mm)
        twice = lambda x: jnp.concatenate([x, x], axis=1)
        new_cn, outs = [], []
        for p in range(HEADS // 2):
            qp = q_ref[pl.ds(r0, L), p * PW:(p + 1) * PW]
            kp = q_ref[pl.ds(r0, L), GROUP_W + p * PW:GROUP_W + (p + 1) * PW] * (HEAD_DIM ** -0.5)
            vp = q_ref[pl.ds(r0, L), 2 * GROUP_W + p * PW:2 * GROUP_W + (p + 1) * PW]
            ktp = kt_ref[p * PW:(p + 1) * PW, pl.ds(r0, L)]
            qk = _dot_nt(jnp.concatenate([jnp.where(half0, qp, zero_b), jnp.where(half0, zero_b, qp)], axis=0), kp)
            sb, wi, em = [], [], []
            for e in range(2):
                h = 2 * p + e
                rm = jnp.where(mask, r_all[h:h + 1, :], NEG)
                m_h = jnp.broadcast_to(m_all[h:h + 1, :], (L, L))
                c = jnp.maximum(m_h, jnp.broadcast_to(jnp.max(rm, axis=1, keepdims=True), (L, L)))
                sb.append((qk[e * L:(e + 1) * L] * jnp.exp(rm - c)).astype(bf16))
                wi.append(jnp.exp(m_h - c))
                bcol = bcols[:, base + HEADS + h:base + HEADS + h + 1]
                em.append(jnp.exp(-(jnp.broadcast_to(bcol, (L, L)) + c)))
            s2 = jnp.concatenate(sb, axis=1)
            vbd = jnp.concatenate([jnp.where(half0, vp, zero_b), jnp.where(half0, zero_b, vp)], axis=0)
            hx = twice(jnp.where(half0, wi[0], wi[1])) * _dot(qp, cns[p].astype(bf16)) \
                + _dot(s2, jnp.concatenate([vbd, ones_bd], axis=1))
            den = jnp.maximum(jnp.abs(hx[:, PW:2 * PW]), jnp.where(half0, em[0], em[1]))
            outs.append(hx[:, 0:PW] / den)
            wsel = jnp.where(row_half0, w_all[2 * p:2 * p + 1, :], w_all[2 * p + 1:2 * p + 2, :])
            asel = jnp.where(row_half0, a_all[2 * p:2 * p + 1, :], a_all[2 * p + 1:2 * p + 2, :])
            ktw = (ktp.astype(f32) * wsel).astype(bf16)
            upd = _dot(ktw, jnp.concatenate([vp, ones_full], axis=1))
            new_cn.append(twice(asel) * cns[p] + jnp.where(blockdiag2, upd, 0.0))
        return (new_cn, b_end + mm), jnp.concatenate(outs, axis=1)

    def segment(q_ref, kt_ref, gr_ref, hs_ref, carry):
        seg = q_ref.shape[0] // SCAN_BATCH
        n = seg // L

        def body(i, carry):
            out, pieces = [], []
            for g, (fw, bw) in enumerate(carry):
                fw, h_f = direction(q_ref, kt_ref, gr_ref, i, False, fw, g * seg)
                bw, h_b = direction(q_ref, kt_ref, gr_ref, n - 1 - i, True, bw, g * seg)
                out.append((fw, bw))
                pieces += [h_f, h_b]
            hs_ref[i] = jnp.concatenate(pieces, axis=1)
            return tuple(out)

        return lax.fori_loop(0, n, body, carry, unroll=SCAN_UNROLL)

    def init():
        return ([jnp.zeros((PW, 2 * PW), f32) for _ in range(HEADS // 2)], jnp.zeros((HEADS, L), f32))

    carry = segment(qc_ref, ktc, grc_ref, hsc, tuple((init(), init()) for _ in range(SCAN_BATCH)))
    segment(ql_ref, ktl, grl_ref, hsl, carry)

    gmat = _group_mean_matrix(GROUP_W, HEAD_DIM)

    def finish(q_ref, hs_ref, o_ref):
        seg = o_ref.shape[0] // SCAN_BATCH
        n = seg // L
        for g in range(SCAN_BATCH):
            for j in range(n):
                r = g * seg + j * L
                c0 = g * 2 * GROUP_W
                hh = hs_ref[j, :, c0:c0 + GROUP_W] + hs_ref[n - 1 - j, :, c0 + GROUP_W:c0 + 2 * GROUP_W]
                hn = hh * _group_rsqrt(hh, gmat) * on_ref[...]
                og = q_ref[r:r + L, 3 * GROUP_W:4 * GROUP_W].astype(f32)
                o_ref[r:r + L, :] = (_sigmoid(og) * hn).astype(o_ref.dtype)

    finish(ql_ref, hsl, ol_ref)
    if need_ctx:
        finish(qc_ref, hsc, oc_ref)
    else:
        oc_ref[...] = jnp.zeros(oc_ref.shape, oc_ref.dtype)


def _mlstm(qkvo_c, qkvo_l, gt_c, gt_l, gate_b, out_norm, batch, need_ctx):
    cl, t = qkvo_c.shape[0] // batch, qkvo_l.shape[0] // batch
    ng = 4 * HEADS
    bc, bt = SCAN_BATCH * cl, SCAN_BATCH * t
    gb_row = gate_b.reshape(ng, 1)
    qc_spec, ql_spec = _seg_blocks(bc, bt, 4 * GROUP_W)
    grc_spec, grl_spec = _seg_blocks_t(bc, bt, ng, GT_MG)
    oc_spec, ol_spec = _seg_blocks(bc, bt, GROUP_W)
    return pl.pallas_call(
        functools.partial(_mlstm_body, need_ctx=need_ctx),
        grid=(batch // SCAN_BATCH,),
        in_specs=[qc_spec, ql_spec, grc_spec, grl_spec, _resident((ng, 1)), _resident((1, GROUP_W))],
        out_specs=[oc_spec, ol_spec],
        out_shape=[jax.ShapeDtypeStruct((batch * cl, GROUP_W), bf16),
                   jax.ShapeDtypeStruct((batch * t, GROUP_W), bf16)],
        scratch_shapes=[pltpu.VMEM((cl // SCAN_CHUNK, SCAN_CHUNK, 2 * SCAN_BATCH * GROUP_W), f32),
                        pltpu.VMEM((t // SCAN_CHUNK, SCAN_CHUNK, 2 * SCAN_BATCH * GROUP_W), f32),
                        pltpu.VMEM((GROUP_W, bc), bf16), pltpu.VMEM((GROUP_W, bt), bf16)],
        compiler_params=_params(("parallel",)),
        name="mlstm",
    )(qkvo_c, qkvo_l, gt_c, gt_l, gb_row, out_norm.reshape(1, GROUP_W))


def _ssd_body(zc_ref, zl_ref, xc_ref, xl_ref, drc_ref, drl_ref, cw_ref, cb_ref, dbr_ref,
              alr_ref, dsk_ref, ng_ref, oc_ref, ol_ref,
              xac, xal, ysc, ysl, xpad, btc, btl, *, need_ctx):
    L = SCAN_CHUNK
    N = SSD_STATE
    lower, upper = _tri_masks(L)
    tri_lo = jnp.where(lower, 1.0, 0.0).astype(bf16)
    tri_up = jnp.where(upper, 1.0, 0.0).astype(bf16)
    a_row = -jnp.exp(alr_ref[...])
    pad = 8
    half = SSD_CONV // 2

    def conv_act(x_ref, xa_ref):
        n = x_ref.shape[0] // SCAN_BATCH
        zeros = jnp.zeros((pad, SSD_XBC), f32)
        blk = 256
        for g in range(SCAN_BATCH):
            xpad[0:pad, :] = zeros
            xpad[pad + n:2 * pad + n, :] = zeros
            for r in range(0, n, blk):
                xpad[pad + r:pad + r + blk, :] = x_ref[g * n + r:g * n + r + blk, :].astype(f32)
            for r in range(0, n, blk):
                y = jnp.zeros((blk, SSD_XBC), f32) + cb_ref[...]
                for kk in range(SSD_CONV):
                    o = pad + r + kk - half
                    y = y + cw_ref[kk:kk + 1, :] * xpad[o:o + blk, :]
                xa_ref[g * n + r:g * n + r + blk, :] = _silu(y)

    conv_act(xc_ref, xac)

    PW = 2 * HEAD_DIM
    lane = lax.broadcasted_iota(jnp.int32, (1, PW), 1)
    half0 = lane < HEAD_DIM
    row_half0 = lax.broadcasted_iota(jnp.int32, (PW, 1), 0) < N
    rr = lax.broadcasted_iota(jnp.int32, (PW, PW), 0)
    cc = lax.broadcasted_iota(jnp.int32, (PW, PW), 1)
    eye = jnp.where(rr == cc, 1.0, 0.0).astype(bf16)
    zero_b = jnp.zeros((), bf16)

    def transpose_b(xa_ref, bt_ref):
        for r in range(0, xa_ref.shape[0], L):
            bm = xa_ref[r:r + L, GROUP_W:GROUP_W + PW].astype(bf16)
            bt_ref[:, r:r + L] = _dot_nt(eye, bm).astype(bf16)

    transpose_b(xac, btc)

    def direction(xa_ref, bt_ref, dr_ref, j, rev, states, off):
        r0 = pl.multiple_of(off + j * L, L)
        base = HEADS if rev else 0
        mask = upper if rev else lower
        dtr = _softplus(dr_ref[:, pl.ds(r0, L)] + dbr_ref[...])
        ar = dtr * a_row
        cumr = _dot_tri(ar, tri_lo if rev else tri_up)
        cumc = jnp.transpose(cumr)
        cum_end = jnp.broadcast_to(jnp.sum(ar[base:base + HEADS, :], axis=1, keepdims=True), (HEADS, L))
        dec_all = jnp.exp(cum_end - cumr[base:base + HEADS, :]) * dtr[base:base + HEADS, :]
        aexp_all = jnp.exp(cum_end)
        bp = xa_ref[pl.ds(r0, L), GROUP_W:GROUP_W + PW].astype(bf16)
        cp = xa_ref[pl.ds(r0, L), GROUP_W + PW:GROUP_W + 2 * PW].astype(bf16)
        btp = bt_ref[:, pl.ds(r0, L)].astype(f32)
        new_states, outs = [], []
        for g in range(SSD_GROUPS):
            G = _dot_nt(jnp.where(half0 if g == 0 else ~half0, cp, zero_b), bp)
            xpb = xa_ref[pl.ds(r0, L), g * PW:(g + 1) * PW].astype(bf16)
            uu = _dot(jnp.concatenate([btp * dec_all[2 * g:2 * g + 1, :], btp * dec_all[2 * g + 1:2 * g + 2, :]],
                                      axis=0).astype(bf16), xpb)
            ws, ecs, us = [], [], []
            for e in range(2):
                h = 2 * g + e
                idx = base + h
                cum_cb = jnp.broadcast_to(cumc[:, idx:idx + 1], (L, L))
                seg = jnp.exp(jnp.where(mask, cum_cb - cumr[idx:idx + 1, :], NEG))
                ws.append((G * seg * dtr[idx:idx + 1, :]).astype(bf16))
                ecs.append(jnp.exp(cum_cb))
                us.append(aexp_all[h:h + 1, :] * states[g] + uu[e * PW:(e + 1) * PW])
            xbd = jnp.concatenate([jnp.where(half0, xpb, zero_b), jnp.where(half0, zero_b, xpb)], axis=0)
            y = _dot(jnp.concatenate(ws, axis=1), xbd) + jnp.where(half0, ecs[0], ecs[1]) * _dot(cp, states[g].astype(bf16))
            outs.append(y)
            new_states.append(jnp.where(row_half0 if g == 0 else ~row_half0, jnp.where(half0, us[0], us[1]), 0.0))
        return new_states, jnp.concatenate(outs, axis=1)

    def segment(xa_ref, bt_ref, dr_ref, ys_ref, carry):
        seg = xa_ref.shape[0] // SCAN_BATCH
        n = seg // L

        def body(i, carry):
            out, pieces = [], []
            for g, (fw, bw) in enumerate(carry):
                fw, y_f = direction(xa_ref, bt_ref, dr_ref, i, False, fw, g * seg)
                bw, y_b = direction(xa_ref, bt_ref, dr_ref, n - 1 - i, True, bw, g * seg)
                out.append((fw, bw))
                pieces += [y_f, y_b]
            ys_ref[i] = jnp.concatenate(pieces, axis=1)
            return tuple(out)

        return lax.fori_loop(0, n, body, carry, unroll=SCAN_UNROLL)

    init = [jnp.zeros((PW, PW), f32) for _ in range(SSD_GROUPS)]
    carry = segment(xac, btc, drc_ref, ysc, tuple((init, list(init)) for _ in range(SCAN_BATCH)))
    conv_act(xl_ref, xal)
    transpose_b(xal, btl)
    segment(xal, btl, drl_ref, ysl, carry)

    gmat = _group_mean_matrix(GROUP_W, SSD_NORM_GROUP)

    def finish(z_ref, xa_ref, ys_ref, o_ref):
        seg = o_ref.shape[0] // SCAN_BATCH
        n = seg // L
        for g in range(SCAN_BATCH):
            for j in range(n):
                r = g * seg + j * L
                c0 = g * 2 * GROUP_W
                yy = ys_ref[j, :, c0:c0 + GROUP_W] + ys_ref[n - 1 - j, :, c0 + GROUP_W:c0 + 2 * GROUP_W] \
                    + dsk_ref[...] * xa_ref[r:r + L, 0:GROUP_W]
                gt = yy * _silu(z_ref[r:r + L, :].astype(f32))
                o_ref[r:r + L, :] = (gt * _group_rsqrt(gt, gmat) * ng_ref[...]).astype(o_ref.dtype)

    finish(zl_ref, xal, ysl, ol_ref)
    if need_ctx:
        finish(zc_ref, xac, ysc, oc_ref)
    else:
        oc_ref[...] = jnp.zeros(oc_ref.shape, oc_ref.dtype)


def _ssd(z_c, z_l, xbc_c, xbc_l, gt_c, gt_l, conv_w, conv_b, dt_bias, a_log, d_skip, norm_g, batch, need_ctx):
    cl, t = z_c.shape[0] // batch, z_l.shape[0] // batch
    nd = 2 * HEADS
    bc, bt = SCAN_BATCH * cl, SCAN_BATCH * t
    cw = jnp.zeros((8, SSD_XBC), f32).at[:SSD_CONV].set(conv_w)
    zc_spec, zl_spec = _seg_blocks(bc, bt, GROUP_W)
    xc_spec, xl_spec = _seg_blocks(bc, bt, SSD_XBC)
    drc_spec, drl_spec = _seg_blocks_t(bc, bt, nd, GT_DT)
    oc_spec, ol_spec = _seg_blocks(bc, bt, GROUP_W)
    return pl.pallas_call(
        functools.partial(_ssd_body, need_ctx=need_ctx),
        grid=(batch // SCAN_BATCH,),
        in_specs=[zc_spec, zl_spec, xc_spec, xl_spec, drc_spec, drl_spec,
                  _resident((8, SSD_XBC)), _resident((1, SSD_XBC)), _resident((nd, 1)), _resident((nd, 1)),
                  _resident((1, GROUP_W)), _resident((1, GROUP_W))],
        out_specs=[oc_spec, ol_spec],
        out_shape=[jax.ShapeDtypeStruct((batch * cl, GROUP_W), bf16),
                   jax.ShapeDtypeStruct((batch * t, GROUP_W), bf16)],
        scratch_shapes=[pltpu.VMEM((bc, SSD_XBC), f32), pltpu.VMEM((bt, SSD_XBC), f32),
                        pltpu.VMEM((cl // SCAN_CHUNK, SCAN_CHUNK, 2 * SCAN_BATCH * GROUP_W), f32),
                        pltpu.VMEM((t // SCAN_CHUNK, SCAN_CHUNK, 2 * SCAN_BATCH * GROUP_W), f32),
                        pltpu.VMEM((t + 16, SSD_XBC), f32),
                        pltpu.VMEM((2 * SSD_STATE, bc), bf16), pltpu.VMEM((2 * SSD_STATE, bt), bf16)],
        compiler_params=_params(("parallel",)),
        name="ssd",
    )(z_c, z_l, xbc_c, xbc_l, gt_c, gt_l, cw, conv_b.reshape(1, SSD_XBC),
      dt_bias.reshape(nd, 1), a_log.reshape(nd, 1),
      jnp.repeat(d_skip, HEAD_DIM).reshape(1, GROUP_W), norm_g.reshape(1, GROUP_W))


def _mla_body(ac_ref, al_ref, qn_ref, kvn_ref, wq_ref, wkv_ref, qg_ref, kg_ref, krg_ref, cos_ref, sin_ref,
              oc_ref, ol_ref, q_s, k_s, vt_s, *, need_ctx):
    cl, t = ac_ref.shape[0], al_ref.shape[0]
    scale = MLA_QK ** -0.5 * LOG2E
    rr = lax.broadcasted_iota(jnp.int32, (LANES, LANES), 0)
    cc = lax.broadcasted_iota(jnp.int32, (LANES, LANES), 1)
    in_nope = (rr < MLA_NOPE) & (cc < MLA_NOPE)
    in_rope = (rr >= MLA_NOPE) & (rr < MLA_QK) & (cc >= MLA_NOPE) & (cc < MLA_QK)
    head_gmat = jnp.where(in_nope, 1.0 / MLA_NOPE, jnp.where(in_rope, 1.0 / MLA_ROPE, 0.0)).astype(bf16)
    rot = _rotate_half_matrix(MLA_ROPE // 4)
    er = lax.broadcasted_iota(jnp.int32, (GROUP_W, GROUP_W), 0)
    ec = lax.broadcasted_iota(jnp.int32, (GROUP_W, GROUP_W), 1)
    eye = jnp.where(er == ec, 1.0, 0.0).astype(bf16)

    def head_norm(x):
        return x * _group_rsqrt(x, head_gmat)

    def project(a_ref, row0, n, roped):
        blk = 256
        for r in range(0, n, blk):
            a = a_ref[r:r + blk, :].astype(f32)
            aq, akv, akr = a[:, 0:GROUP_W], a[:, GROUP_W:GROUP_W + LANES], a[:, GROUP_W + LANES:GROUP_W + 2 * LANES]
            qh = _dot((aq * _row_rsqrt(aq, GROUP_W) * qn_ref[...]).astype(bf16), wq_ref[...])
            kv = _dot((akv * _row_rsqrt(akv, kvn_ref.shape[1]) * kvn_ref[...]).astype(bf16), wkv_ref[...])
            kr = akr * _row_rsqrt(akr, MLA_ROPE) * krg_ref[...]
            if roped:
                cos, sin = cos_ref[r:r + blk, :], sin_ref[r:r + blk, :]
                kr = _rope(kr, cos, sin, rot)
            for h in range(HEADS):
                qx = head_norm(qh[:, h * LANES:(h + 1) * LANES]) * qg_ref[...]
                if roped:
                    qx = _rope(qx, cos, sin, rot)
                q_s[h, row0 + r:row0 + r + blk, :] = (qx * scale).astype(bf16)
                kx = kv[:, h * LANES:(h + 1) * LANES]
                kx = kx * _row_rsqrt(kx, MLA_NOPE) * kg_ref[...]
                k_s[h, row0 + r:row0 + r + blk, :] = (kx + kr).astype(bf16)
            vt_s[:, row0 + r:row0 + r + blk] = _dot_nt(eye, kv[:, HEADS * LANES:].astype(bf16)).astype(bf16)

    project(ac_ref, 0, cl, False)
    project(al_ref, cl, t, True)

    def attend(q0, nq, nk, o_ref, o0):
        outs = []
        sts = [_dot_nt(k_s[h, 0:nk, :], q_s[h, pl.ds(q0, nq), :]) for h in range(HEADS)]
        for h in range(HEADS):
            st = sts[h]
            p = jnp.exp2(st - jnp.max(st, axis=0, keepdims=True))
            den = jnp.sum(p, axis=0, keepdims=True)
            outs.append(_dot(vt_s[h * HEAD_DIM:(h + 1) * HEAD_DIM, 0:nk], p.astype(bf16)) / den)
        ot = jnp.concatenate(outs, axis=0).astype(bf16)
        for c in range(0, nq, GROUP_W):
            o_ref[pl.ds(o0 + c, GROUP_W), :] = _dot_nt(eye, ot[:, c:c + GROUP_W]).astype(o_ref.dtype)

    def body(i, _):
        o0 = pl.multiple_of(i * MLA_QBLK, MLA_QBLK)
        attend(cl + o0, MLA_QBLK, cl + t, ol_ref, o0)
        return 0

    lax.fori_loop(0, t // MLA_QBLK, body, 0)
    if need_ctx:
        attend(0, cl, cl, oc_ref, 0)
    else:
        oc_ref[...] = jnp.zeros(oc_ref.shape, oc_ref.dtype)


def _mla(a_c, a_l, q_norm, kv_norm, wq_b, wkv_b, q_gain, k_gain, cos, sin, batch, need_ctx):
    cl, t = a_c.shape[0] // batch, a_l.shape[0] // batch
    kvl = kv_norm.shape[0]
    wq = jnp.zeros((GROUP_W, HEADS, LANES), f32).at[:, :, :MLA_QK].set(wq_b.reshape(GROUP_W, HEADS, MLA_QK))
    wkv4 = wkv_b.reshape(kvl, HEADS, MLA_NOPE + HEAD_DIM)
    wk = jnp.zeros((kvl, HEADS, LANES), f32).at[:, :, :MLA_NOPE].set(wkv4[:, :, :MLA_NOPE])
    wkv = jnp.concatenate([wk.reshape(kvl, HEADS * LANES), wkv4[:, :, MLA_NOPE:].reshape(kvl, GROUP_W)], axis=1)
    slab = lambda v, off: jnp.zeros((1, LANES), f32).at[0, off:off + v.shape[0]].set(v)
    qg = slab(q_gain, 0)
    kg = slab(k_gain[:MLA_NOPE], 0)
    krg = slab(k_gain[MLA_NOPE:], MLA_NOPE)
    ac_spec, al_spec = _seg_blocks(cl, t, 2 * GROUP_W)
    oc_spec, ol_spec = _seg_blocks(cl, t, GROUP_W)
    return pl.pallas_call(
        functools.partial(_mla_body, need_ctx=need_ctx),
        grid=(batch,),
        in_specs=[ac_spec, al_spec, _resident((1, GROUP_W)), _resident((1, LANES)),
                  _resident((GROUP_W, HEADS * LANES)), _resident((kvl, HEADS * LANES + GROUP_W)),
                  _resident((1, LANES)), _resident((1, LANES)), _resident((1, LANES)),
                  _resident((t, LANES)), _resident((t, LANES))],
        out_specs=[oc_spec, ol_spec],
        out_shape=[jax.ShapeDtypeStruct((batch * cl, GROUP_W), bf16),
                   jax.ShapeDtypeStruct((batch * t, GROUP_W), bf16)],
        scratch_shapes=[pltpu.VMEM((HEADS, cl + t, LANES), bf16), pltpu.VMEM((HEADS, cl + t, LANES), bf16),
                        pltpu.VMEM((GROUP_W, cl + t), bf16)],
        compiler_params=_params(("parallel",)),
        name="mla",
    )(a_c, a_l, q_norm.reshape(1, GROUP_W), kv_norm.reshape(1, kvl),
      wq.reshape(GROUP_W, HEADS * LANES).astype(bf16), wkv.astype(bf16), qg, kg, krg, cos, sin)


def _swa_body(wc_ref, wl_ref, qg_ref, kg_ref, sink_ref, cos_ref, sin_ref, oc_ref, ol_ref,
              q_s, qc_s, k_s, v_s, kc_s, vc_s, *, need_ctx):
    cl, t = wc_ref.shape[0], wl_ref.shape[0]
    scale = HEAD_DIM ** -0.5 * LOG2E
    kvw = SWA_KV_HEADS * HEAD_DIM
    blk = SWA_BLK
    gmat_q = _group_mean_matrix(GROUP_W, HEAD_DIM)
    gmat_k = _group_mean_matrix(kvw, HEAD_DIM)
    rot = _rotate_half_matrix(HEAD_DIM // 4)

    def project(w_ref, n, roped, qdst, kdst, vdst, row0):
        step = 256
        for r in range(0, n, step):
            w = w_ref[r:r + step, :].astype(f32)
            q, k = w[:, 0:GROUP_W], w[:, GROUP_W:GROUP_W + kvw]
            q = q * _group_rsqrt(q, gmat_q) * qg_ref[...]
            k = k * _group_rsqrt(k, gmat_k) * kg_ref[...]
            if roped:
                cos, sin = cos_ref[r:r + step, :], sin_ref[r:r + step, :]
                q = jnp.concatenate([_rope(q[:, 0:LANES], cos, sin, rot),
                                     _rope(q[:, LANES:2 * LANES], cos, sin, rot)], axis=1)
                k = _rope(k, cos, sin, rot)
            qdst[r:r + step, :] = (q * scale).astype(bf16)
            kdst[row0 + r:row0 + r + step, :] = k.astype(bf16)
            vdst[row0 + r:row0 + r + step, :] = w_ref[r:r + step, GROUP_W + kvw:GROUP_W + 2 * kvw]

    zeros = jnp.zeros((blk, kvw), bf16)
    for s in (k_s, v_s):
        s[0:blk, :] = zeros
        s[blk + t:2 * blk + t, :] = zeros
    project(wc_ref, cl, False, qc_s, kc_s, vc_s, 0)
    project(wl_ref, t, True, q_s, k_s, v_s, blk)

    rr = lax.broadcasted_iota(jnp.int32, (HEADS * blk, 3 * blk), 0) % blk
    jj = lax.broadcasted_iota(jnp.int32, (HEADS * blk, 3 * blk), 1)
    band = (jj - rr >= 0) & (jj - rr <= 2 * SWA_WINDOW)
    half0 = lax.broadcasted_iota(jnp.int32, (1, LANES), 1) < HEAD_DIM
    zero_b = jnp.zeros((), bf16)

    def wide(x, n):
        return jnp.broadcast_to(x, (x.shape[0], n))

    def stacked_queries(q_ref, r0, n):
        qa, qb = q_ref[pl.ds(r0, n), 0:LANES], q_ref[pl.ds(r0, n), LANES:2 * LANES]
        return jnp.concatenate([jnp.where(half0, qa, zero_b), jnp.where(half0, qb, zero_b),
                                jnp.where(half0, zero_b, qa), jnp.where(half0, zero_b, qb)], axis=0)

    def stacked_sink(n):
        row = lax.broadcasted_iota(jnp.int32, (HEADS * n, 1), 0)
        return LOG2E * jnp.where(row < n, sink_ref[0:1, :], jnp.where(row < 2 * n, sink_ref[1:2, :],
                                 jnp.where(row < 3 * n, sink_ref[2:3, :], sink_ref[3:4, :])))

    def unstack(o, n):
        return jnp.concatenate([jnp.where(half0, o[0:n], o[2 * n:3 * n]),
                                jnp.where(half0, o[n:2 * n], o[3 * n:4 * n])], axis=1)

    def lat_block(n, _):
        r0 = pl.multiple_of(n * blk, blk)
        kpos = jj + (n - 1) * blk
        valid = band & (kpos >= 0) & (kpos < t)
        q4 = stacked_queries(q_s, r0, blk)
        s = _dot_nt(q4, jnp.concatenate([k_s[pl.ds(r0, 3 * blk), :], kc_s[...]], axis=0))
        s_loc, s_ctx = jnp.where(valid, s[:, 0:3 * blk], NEG), s[:, 3 * blk:]
        sink = stacked_sink(blk)
        m = jnp.maximum(wide(jnp.maximum(jnp.max(s_loc, axis=1, keepdims=True),
                                         jnp.max(s_ctx, axis=1, keepdims=True)), LANES), sink)
        p_loc = jnp.exp2(s_loc - jnp.concatenate([m] * 3, axis=1))
        p_ctx = jnp.exp2(s_ctx - jnp.concatenate([m] * (cl // LANES), axis=1))
        den = wide(jnp.sum(p_loc, axis=1, keepdims=True) + jnp.sum(p_ctx, axis=1, keepdims=True), LANES) \
            + jnp.exp2(sink - m)
        o = _dot(jnp.concatenate([p_loc, p_ctx], axis=1).astype(bf16),
                 jnp.concatenate([v_s[pl.ds(r0, 3 * blk), :], vc_s[...]], axis=0))
        ol_ref[pl.ds(r0, blk), :] = unstack(o / den, blk).astype(ol_ref.dtype)
        return 0

    lax.fori_loop(0, t // blk, lat_block, 0, unroll=SWA_UNROLL)

    if need_ctx:
        half = cl // 2
        for r0 in (0, half):
            s = _dot_nt(stacked_queries(qc_s, r0, half), kc_s[...])
            sink = stacked_sink(half)
            m = jnp.maximum(wide(jnp.max(s, axis=1, keepdims=True), LANES), sink)
            p = jnp.exp2(s - jnp.concatenate([m] * (cl // LANES), axis=1))
            den = wide(jnp.sum(p, axis=1, keepdims=True), LANES) + jnp.exp2(sink - m)
            oc_ref[r0:r0 + half, :] = unstack(_dot(p.astype(bf16), vc_s[...]) / den, half).astype(oc_ref.dtype)
    else:
        oc_ref[...] = jnp.zeros(oc_ref.shape, oc_ref.dtype)


def _swa_head_order(a, axis):
    blocks = jnp.split(a, HEADS, axis=axis)
    return jnp.concatenate([blocks[0], blocks[2], blocks[1], blocks[3]], axis=axis)


def _swa(w_c, w_l, q_gain, k_gain, sink, cos, sin, batch, need_ctx):
    cl, t = w_c.shape[0] // batch, w_l.shape[0] // batch
    kvw = SWA_KV_HEADS * HEAD_DIM
    wc_spec, wl_spec = _seg_blocks(cl, t, 2 * GROUP_W)
    oc_spec, ol_spec = _seg_blocks(cl, t, GROUP_W)
    return pl.pallas_call(
        functools.partial(_swa_body, need_ctx=need_ctx),
        grid=(batch,),
        in_specs=[wc_spec, wl_spec, _resident((1, GROUP_W)), _resident((1, kvw)), _resident((HEADS, LANES)),
                  _resident((t, LANES)), _resident((t, LANES))],
        out_specs=[oc_spec, ol_spec],
        out_shape=[jax.ShapeDtypeStruct((batch * cl, GROUP_W), bf16),
                   jax.ShapeDtypeStruct((batch * t, GROUP_W), bf16)],
        scratch_shapes=[pltpu.VMEM((t, GROUP_W), bf16), pltpu.VMEM((cl, GROUP_W), bf16),
                        pltpu.VMEM((t + 2 * SWA_BLK, kvw), bf16), pltpu.VMEM((t + 2 * SWA_BLK, kvw), bf16),
                        pltpu.VMEM((cl, kvw), bf16), pltpu.VMEM((cl, kvw), bf16)],
        compiler_params=_params(("parallel",)),
        name="swa",
    )(w_c, w_l, jnp.tile(q_gain, HEADS).reshape(1, GROUP_W), jnp.tile(k_gain, SWA_KV_HEADS).reshape(1, kvw),
      jnp.broadcast_to(sink.reshape(HEADS, 1), (HEADS, LANES)), cos, sin)


def _rope_tables(t, rot_dim, lane0):
    pos = jnp.arange(t)
    row, col = (pos // GRID_W).astype(f32), (pos % GRID_W).astype(f32)
    nf = rot_dim // 4
    inv = ROPE_BASE ** (-jnp.arange(nf, dtype=f32) / nf)
    ar, ac = row[:, None] * inv, col[:, None] * inv
    ang = jnp.concatenate([ar, ar, ac, ac], axis=-1)
    return jnp.cos(ang), jnp.sin(ang)


def _pack_in_weight(w_in):
    o = 0
    seg = {}
    for name, n in (("qkvo", 4 * GROUP_W), ("mg", 4 * HEADS), ("a_q", GROUP_W), ("a_kv", GROUP_W // 2),
                    ("a_kr", MLA_ROPE), ("swa", 2 * GROUP_W), ("z", GROUP_W), ("xbc", SSD_XBC), ("dt", 2 * HEADS)):
        seg[name] = w_in[:, o:o + n]
        o += n
    d = w_in.shape[0]
    zeros = lambda n: jnp.zeros((d, n), w_in.dtype)
    packed = jnp.concatenate([
        seg["qkvo"],
        seg["a_q"], seg["a_kv"], zeros(MLA_NOPE), seg["a_kr"], zeros(LANES - MLA_QK),
        _swa_head_order(seg["swa"][:, :GROUP_W], 1), seg["swa"][:, GROUP_W:],
        seg["z"], seg["xbc"]], axis=1)
    gate_t = jnp.concatenate([seg["mg"], seg["dt"], zeros(GT_ROWS - 4 * HEADS - 2 * HEADS)], axis=1).T
    return packed.astype(bf16), gate_t.astype(bf16)


def kernel(x, c, ctx, c_ctx, w_mod, b_mod, ffn1_norm, ffn1_wi, ffn1_wo, mix_norm, w_in, w_out, mlstm_gate_b, mlstm_out_norm, mla_q_norm, mla_kv_norm, mla_wq_b, mla_wkv_b, mla_q_gain, mla_k_gain, swa_q_gain, swa_k_gain, swa_sink, ssd_conv_w, ssd_conv_b, ssd_dt_bias, ssd_a_log, ssd_d, ssd_norm, ffn2_norm, ffn2_wi, ffn2_wo):
    b, t, d = x.shape
    cl = ctx.shape[1]
    depth = w_mod.shape[0]
    dff = ffn1_wo.shape[1]
    lat_tiles = t // ROW_TILE

    cos_m, sin_m = _rope_tables(t, MLA_ROPE, MLA_NOPE)
    pad_id = lambda tab, fill: jnp.concatenate(
        [jnp.full((t, MLA_NOPE), fill, f32), tab, jnp.full((t, LANES - MLA_QK), fill, f32)], axis=1)
    cos_m, sin_m = pad_id(cos_m, 1.0), pad_id(sin_m, 0.0)
    cos_s, sin_s = _rope_tables(t, HEAD_DIM, 0)
    cos_s, sin_s = jnp.tile(cos_s, (1, LANES // HEAD_DIM)), jnp.tile(sin_s, (1, LANES // HEAD_DIM))

    h = x.reshape(b * t, d)
    hc = ctx.reshape(b * cl, d)
    cc = jnp.concatenate([c, c_ctx[None, :]], axis=0)
    for l in range(depth):
        need_ctx = l < depth - 1
        mod = _modulation(cc, w_mod, b_mod, l).reshape(b + 1, N_MOD, d)
        streams = ((0, lat_tiles), (b, None))

        ffn1_w = (ffn1_norm[l].reshape(1, d), ffn1_wi[l][:, :dff].astype(bf16), ffn1_wi[l][:, dff:].astype(bf16),
                  ffn1_wo[l].astype(bf16))
        ffn2_w = (ffn2_norm[l].reshape(1, d), ffn2_wi[l][:, :dff].astype(bf16), ffn2_wi[l][:, dff:].astype(bf16),
                  ffn2_wo[l].astype(bf16))
        h = _ffn(h, mod, *ffn1_w, 0, *streams[0])
        hc = _ffn(hc, mod, *ffn1_w, 0, *streams[1])
        w_packed, w_gate_t = _pack_in_weight(w_in[l])
        in_w = (mix_norm[l].reshape(1, d), w_packed, w_gate_t)
        qkvo_l, mla_l, swa_l, z_l, xbc_l, gt_l = _inproj(h, mod, *in_w, *streams[0])
        qkvo_c, mla_c, swa_c, z_c, xbc_c, gt_c = _inproj(hc, mod, *in_w, *streams[1])
        a_c, a_l = _mlstm(qkvo_c, qkvo_l, gt_c, gt_l, mlstm_gate_b[l], mlstm_out_norm[l], b, need_ctx)
        m_c, m_l = _mla(mla_c, mla_l, mla_q_norm[l], mla_kv_norm[l], mla_wq_b[l], mla_wkv_b[l], mla_q_gain[l],
                        mla_k_gain[l], cos_m, sin_m, b, need_ctx)
        s_c, s_l = _swa(swa_c, swa_l, swa_q_gain[l], swa_k_gain[l], swa_sink[l], cos_s, sin_s, b, need_ctx)
        d_c, d_l = _ssd(z_c, z_l, xbc_c, xbc_l, gt_c, gt_l, ssd_conv_w[l], ssd_conv_b[l], ssd_dt_bias[l],
                        ssd_a_log[l], ssd_d[l], ssd_norm[l], b, need_ctx)
        wo_b = jnp.concatenate([w_out[l][:2 * GROUP_W], _swa_head_order(w_out[l][2 * GROUP_W:3 * GROUP_W], 0),
                                w_out[l][3 * GROUP_W:]], axis=0).astype(bf16)
        h = _mix_ffn(h, mod, (a_l, m_l, s_l, d_l), wo_b, *ffn2_w, *streams[0])
        if need_ctx:
            hc = _mix_ffn(hc, mod, (a_c, m_c, s_c, d_c), wo_b, *ffn2_w, *streams[1])
    return h.reshape(b, t, d)
```

```python
import functools

import jax
import jax.numpy as jnp
from jax import lax
from jax.experimental import pallas as pl
from jax.experimental.pallas import tpu as pltpu

f32 = jnp.float32
bf16 = jnp.bfloat16

RMS_EPS = 1e-6
ROPE_BASE = 10000.0
GRID_W = 64
N_MOD = 9
HEADS = 4
HEAD_DIM = 64
GROUP_W = HEADS * HEAD_DIM
MLA_NOPE = 64
MLA_ROPE = 32
MLA_QK = MLA_NOPE + MLA_ROPE
SWA_KV_HEADS = 2
SWA_WINDOW = 128
SWA_BLK = 128
SSD_STATE = 64
SSD_GROUPS = 2
SSD_CONV = 5
SSD_XBC = GROUP_W + 2 * SSD_GROUPS * SSD_STATE
SSD_NORM_GROUP = 128

LANES = 128
ROW_TILE = 512
FF_CHUNK = 256
SCAN_CHUNK = 128
SCAN_UNROLL = 4
SWA_UNROLL = 4
SCAN_BATCH = 2
MLA_QBLK = 512
NEG = -1e30
LOG2E = 1.4426950408889634
VMEM_LIMIT = 56 * 1024 * 1024

P_QKVO = 0
P_MLA = 1024
P_SWA = 1536
P_Z = 2048
P_XBC = 2304
P_COLS = 2816
GT_MG = 0
GT_DT = 4 * HEADS
GT_ROWS = 32


def _dot(a, b):
    return jnp.dot(a, b, preferred_element_type=f32)


def _dot_nt(a, b):
    return lax.dot_general(a, b, (((1,), (1,)), ((), ())), preferred_element_type=f32)


def _sigmoid(x):
    return 1.0 / (1.0 + jnp.exp(-x))


def _silu(x):
    return x * _sigmoid(x)


def _softplus(x):
    return jnp.maximum(x, 0.0) + jnp.log(1.0 + jnp.exp(-jnp.abs(x)))


def _log_sigmoid(x):
    return -_softplus(-x)


def _split3(x):
    hi = x.astype(bf16)
    r = x - hi.astype(f32)
    mid = r.astype(bf16)
    lo = (r - mid.astype(f32)).astype(bf16)
    return hi, mid, lo


def _dot_tri(x, tri):
    m = x.shape[0]
    terms = jnp.concatenate([p.astype(f32) for p in _split3(x)], axis=0).astype(bf16)
    y = _dot(terms, tri)
    return y[0:m] + y[m:2 * m] + y[2 * m:3 * m]


def _dot2(x, w):
    hi = x.astype(bf16)
    mid = (x - hi.astype(f32)).astype(bf16)
    return _dot(hi, w) + _dot(mid, w)


def _group_mean_matrix(n, gsz):
    r = lax.broadcasted_iota(jnp.int32, (n, n), 0) // gsz
    c = lax.broadcasted_iota(jnp.int32, (n, n), 1) // gsz
    return jnp.where(r == c, 1.0 / gsz, 0.0).astype(bf16)


def _group_rsqrt(x, gmat):
    return lax.rsqrt(_dot2(x * x, gmat) + RMS_EPS)


def _row_rsqrt(x, n_real):
    return lax.rsqrt(jnp.sum(x * x, axis=1, keepdims=True) * (1.0 / n_real) + RMS_EPS)


def _rotate_half_matrix(quarter):
    r = lax.broadcasted_iota(jnp.int32, (LANES, LANES), 0)
    c = lax.broadcasted_iota(jnp.int32, (LANES, LANES), 1)
    first = (c % (2 * quarter)) < quarter
    return jnp.where(first & (r == c + quarter), -1.0, jnp.where(~first & (r == c - quarter), 1.0, 0.0)).astype(bf16)


def _rope(x, cos, sin, rot):
    return x * cos + _dot2(x, rot) * sin


def _resident(shape):
    nd = len(shape)
    return pl.BlockSpec(shape, lambda *_: (0,) * nd, pipeline_mode=pl.Buffered(1))


def _params(sem):
    return pltpu.CompilerParams(dimension_semantics=sem, vmem_limit_bytes=VMEM_LIMIT)


def _mod_body(c_ref, w_ref, b_ref, o_ref):
    s = _silu(c_ref[...])
    o_ref[...] = jnp.dot(s, w_ref[...], preferred_element_type=f32,
                         precision=lax.Precision.HIGHEST) + b_ref[...]


def _modulation(cc, w_all, b_all, layer):
    m, d = cc.shape
    depth, _, n = w_all.shape
    tn = 1024
    return pl.pallas_call(
        _mod_body,
        grid=(n // tn,),
        in_specs=[pl.BlockSpec((m, d), lambda j: (0, 0)),
                  pl.BlockSpec((None, d, tn), lambda j: (layer, 0, j)),
                  pl.BlockSpec((None, 1, tn), lambda j: (layer, 0, j))],
        out_specs=pl.BlockSpec((m, tn), lambda j: (0, j)),
        out_shape=jax.ShapeDtypeStruct((m, n), f32),
        compiler_params=_params(("arbitrary",)),
        name="modulation",
    )(cc, w_all, b_all.reshape(depth, 1, n))


def _norm_mod(x, gain, shift, scale):
    xn = x * _row_rsqrt(x, x.shape[1]) * gain
    return (xn * (1.0 + scale) + shift).astype(bf16)


def _swiglu_half_step(x, mod_ref, g_ref, wig_ref, wiu_ref, wo_ref, mi):
    xb = _norm_mod(x, g_ref[...], mod_ref[mi:mi + 1, :], mod_ref[mi + 1:mi + 2, :])
    acc = jnp.zeros(x.shape, f32)
    for c in range(wig_ref.shape[1] // FF_CHUNK):
        sl = slice(c * FF_CHUNK, (c + 1) * FF_CHUNK)
        g = _dot(xb, wig_ref[:, sl])
        u = _dot(xb, wiu_ref[:, sl])
        acc = acc + _dot((_silu(g) * u).astype(bf16), wo_ref[sl, :])
    return x + (0.5 * mod_ref[mi + 2:mi + 3, :]) * acc


def _ffn_body(x_ref, mod_ref, g_ref, wig_ref, wiu_ref, wo_ref, o_ref, *, mi):
    o_ref[...] = _swiglu_half_step(x_ref[...], mod_ref, g_ref, wig_ref, wiu_ref, wo_ref, mi)


def _mix_ffn_body(x_ref, mod_ref, a_ref, m_ref, w_ref, s_ref, wout_ref, g_ref, wig_ref, wiu_ref, wo_ref, o_ref):
    acc = _dot(a_ref[...], wout_ref[0:GROUP_W, :])
    acc = acc + _dot(m_ref[...], wout_ref[GROUP_W:2 * GROUP_W, :])
    acc = acc + _dot(w_ref[...], wout_ref[2 * GROUP_W:3 * GROUP_W, :])
    acc = acc + _dot(s_ref[...], wout_ref[3 * GROUP_W:4 * GROUP_W, :])
    x = x_ref[...] + mod_ref[5:6, :] * acc
    o_ref[...] = _swiglu_half_step(x, mod_ref, g_ref, wig_ref, wiu_ref, wo_ref, 6)


def _mod_spec(d, mod_base, tiles_per_mod):
    if tiles_per_mod is None:
        return pl.BlockSpec((None, N_MOD, d), lambda i: (mod_base, 0, 0))
    return pl.BlockSpec((None, N_MOD, d), lambda i: (mod_base + i // tiles_per_mod, 0, 0))


def _ffn(x, mod, gain, wig, wiu, wo, mi, mod_base, tiles_per_mod):
    rows, d = x.shape
    dff = wig.shape[1]
    return pl.pallas_call(
        functools.partial(_ffn_body, mi=mi),
        grid=(rows // ROW_TILE,),
        in_specs=[pl.BlockSpec((ROW_TILE, d), lambda i: (i, 0)),
                  _mod_spec(d, mod_base, tiles_per_mod),
                  _resident((1, d)), _resident((d, dff)), _resident((d, dff)), _resident((dff, d))],
        out_specs=pl.BlockSpec((ROW_TILE, d), lambda i: (i, 0)),
        out_shape=jax.ShapeDtypeStruct((rows, d), f32),
        compiler_params=_params(("parallel",)),
        name="ffn",
    )(x, mod, gain, wig, wiu, wo)


def _inproj_body(x_ref, mod_ref, g_ref, w_ref, wt_ref, qkvo_ref, mla_ref, swa_ref, z_ref, xbc_ref, gt_ref):
    xb = _norm_mod(x_ref[...], g_ref[...], mod_ref[3:4, :], mod_ref[4:5, :])
    qkvo_ref[...] = _dot(xb, w_ref[:, P_QKVO:P_MLA]).astype(bf16)
    mla_ref[...] = _dot(xb, w_ref[:, P_MLA:P_SWA]).astype(bf16)
    swa_ref[...] = _dot(xb, w_ref[:, P_SWA:P_Z]).astype(bf16)
    z_ref[...] = _dot(xb, w_ref[:, P_Z:P_XBC]).astype(bf16)
    xbc_ref[...] = _dot(xb, w_ref[:, P_XBC:P_COLS]).astype(bf16)
    gt_ref[...] = _dot_nt(wt_ref[...], xb)


def _inproj(x, mod, gain, w, wt, mod_base, tiles_per_mod):
    rows, d = x.shape
    widths = [P_MLA - P_QKVO, P_SWA - P_MLA, P_Z - P_SWA, P_XBC - P_Z, P_COLS - P_XBC]
    return pl.pallas_call(
        _inproj_body,
        grid=(rows // ROW_TILE,),
        in_specs=[pl.BlockSpec((ROW_TILE, d), lambda i: (i, 0)),
                  _mod_spec(d, mod_base, tiles_per_mod),
                  _resident((1, d)), _resident((d, P_COLS)), _resident((GT_ROWS, d))],
        out_specs=[pl.BlockSpec((ROW_TILE, n), lambda i: (i, 0)) for n in widths]
                  + [pl.BlockSpec((GT_ROWS, ROW_TILE), lambda i: (0, i))],
        out_shape=[jax.ShapeDtypeStruct((rows, n), bf16) for n in widths]
                  + [jax.ShapeDtypeStruct((GT_ROWS, rows), f32)],
        compiler_params=_params(("parallel",)),
        name="inproj",
    )(x, mod, gain, w, wt)


def _mix_ffn(x, mod, mixed, w_out, gain, wig, wiu, wo, mod_base, tiles_per_mod):
    rows, d = x.shape
    dff = wig.shape[1]
    return pl.pallas_call(
        _mix_ffn_body,
        grid=(rows // ROW_TILE,),
        in_specs=[pl.BlockSpec((ROW_TILE, d), lambda i: (i, 0)),
                  _mod_spec(d, mod_base, tiles_per_mod)]
                 + [pl.BlockSpec((ROW_TILE, GROUP_W), lambda i: (i, 0))] * 4
                 + [_resident(w_out.shape), _resident((1, d)), _resident((d, dff)), _resident((d, dff)),
                    _resident((dff, d))],
        out_specs=pl.BlockSpec((ROW_TILE, d), lambda i: (i, 0)),
        out_shape=jax.ShapeDtypeStruct((rows, d), f32),
        compiler_params=_params(("parallel",)),
        name="mix_ffn",
    )(x, mod, *mixed, w_out, gain, wig, wiu, wo)


def _tri_masks(n):
    r = lax.broadcasted_iota(jnp.int32, (n, n), 0)
    c = lax.broadcasted_iota(jnp.int32, (n, n), 1)
    return r >= c, r <= c


def _seg_blocks(cl, t, width):
    return (pl.BlockSpec((cl, width), lambda b: (b, 0)), pl.BlockSpec((t, width), lambda b: (b, 0)))


def _seg_blocks_t(cl, t, nrow, row0):
    blk = row0 // nrow
    return (pl.BlockSpec((nrow, cl), lambda b: (blk, b)), pl.BlockSpec((nrow, t), lambda b: (blk, b)))


def _mlstm_body(qc_ref, ql_ref, grc_ref, grl_ref, gbr_ref, on_ref, oc_ref, ol_ref,
                hsc, hsl, ktc, ktl, *, need_ctx):
    L = SCAN_CHUNK
    PW = 2 * HEAD_DIM
    lower, upper = _tri_masks(L)
    tri_lo = jnp.where(lower, 1.0, 0.0).astype(bf16)
    tri_up = jnp.where(upper, 1.0, 0.0).astype(bf16)
    lane = lax.broadcasted_iota(jnp.int32, (1, PW), 1)
    half0 = lane < HEAD_DIM
    rr = lax.broadcasted_iota(jnp.int32, (PW, PW), 0)
    cc = lax.broadcasted_iota(jnp.int32, (PW, PW), 1)
    eye = jnp.where(rr == cc, 1.0, 0.0).astype(bf16)
    zero_b = jnp.zeros((), bf16)
    lane_head = lax.broadcasted_iota(jnp.int32, (1, GROUP_W), 1) // HEAD_DIM
    row_head = lax.broadcasted_iota(jnp.int32, (GROUP_W, 1), 0) // HEAD_DIM
    rr4 = lax.broadcasted_iota(jnp.int32, (GROUP_W, 2 * GROUP_W), 0) // HEAD_DIM
    cc4 = (lax.broadcasted_iota(jnp.int32, (GROUP_W, 2 * GROUP_W), 1) % GROUP_W) // HEAD_DIM
    blockdiag4 = rr4 == cc4
    ones_head = [jnp.broadcast_to(jnp.where(lane_head == h, 1.0, 0.0), (L, GROUP_W)).astype(bf16)
                 for h in range(HEADS)]
    ones_all = jnp.ones((L, GROUP_W), bf16)

    def transpose_keys(q_ref, kt_ref):
        for r in range(0, q_ref.shape[0], L):
            for p in range(HEADS // 2):
                k = q_ref[r:r + L, GROUP_W + p * PW:GROUP_W + (p + 1) * PW] * (HEAD_DIM ** -0.5)
                kt_ref[p * PW:(p + 1) * PW, r:r + L] = _dot_nt(eye, k).astype(bf16)

    transpose_keys(qc_ref, ktc)
    transpose_keys(ql_ref, ktl)

    def direction(q_ref, kt_ref, gr_ref, j, rev, carry, off):
        r0 = pl.multiple_of(off + j * L, L)
        base = 2 * HEADS if rev else 0
        mask = upper if rev else lower
        grow = gr_ref[:, pl.ds(r0, L)] + gbr_ref[...]
        lfr = _log_sigmoid(grow)
        brows = _dot_tri(lfr, tri_lo if rev else tri_up)
        bcols = jnp.transpose(brows)
        r_all = grow[base:base + HEADS, :] - brows[base + HEADS:base + 2 * HEADS, :]
        wide = lambda x: jnp.broadcast_to(x, (HEADS, L))
        b_end = wide(jnp.sum(lfr[base + HEADS:base + 2 * HEADS, :], axis=1, keepdims=True))
        cn, m_all = carry
        mm = jnp.maximum(m_all, wide(jnp.max(r_all, axis=1, keepdims=True)))
        a_all = jnp.exp(m_all - mm)
        w_all = jnp.exp(r_all - mm)
        twice = lambda x: jnp.concatenate([x, x], axis=1)
        q_all = q_ref[pl.ds(r0, L), 0:GROUP_W]
        k_all = q_ref[pl.ds(r0, L), GROUP_W:2 * GROUP_W] * (HEAD_DIM ** -0.5)
        v_all = q_ref[pl.ds(r0, L), 2 * GROUP_W:3 * GROUP_W]
        kt_all = kt_ref[:, pl.ds(r0, L)]
        qk = _dot_nt(jnp.concatenate([jnp.where(lane_head == h, q_all, zero_b) for h in range(HEADS)], axis=0), k_all)
        sb, wi, em = [], [], []
        for h in range(HEADS):
            rm = jnp.where(mask, r_all[h:h + 1, :], NEG)
            m_h = jnp.broadcast_to(m_all[h:h + 1, :], (L, L))
            c = jnp.maximum(m_h, jnp.broadcast_to(jnp.max(rm, axis=1, keepdims=True), (L, L)))
            sb.append((qk[h * L:(h + 1) * L] * jnp.exp(rm - c)).astype(bf16))
            wi.append(jnp.exp(m_h - c))
            bcol = bcols[:, base + HEADS + h:base + HEADS + h + 1]
            em.append(jnp.exp(-(jnp.broadcast_to(bcol, (L, L)) + c)))
        per_head = lambda xs: jnp.concatenate([jnp.where(half0, xs[0], xs[1]), jnp.where(half0, xs[2], xs[3])], axis=1)
        vo = jnp.concatenate([jnp.concatenate([jnp.where(lane_head == h, v_all, zero_b),
                                               ones_head[h]], axis=1)
                              for h in range(HEADS)], axis=0)
        hx = twice(per_head(wi)) * _dot(q_all, cn.astype(bf16)) + _dot(jnp.concatenate(sb, axis=1), vo)
        den = jnp.maximum(jnp.abs(hx[:, GROUP_W:2 * GROUP_W]), per_head(em))
        out = hx[:, 0:GROUP_W] / den
        wsel = jnp.where(row_head == 0, w_all[0:1, :], jnp.where(row_head == 1, w_all[1:2, :],
                         jnp.where(row_head == 2, w_all[2:3, :], w_all[3:4, :])))
        asel = jnp.where(row_head == 0, a_all[0:1, :], jnp.where(row_head == 1, a_all[1:2, :],
                         jnp.where(row_head == 2, a_all[2:3, :], a_all[3:4, :])))
        ktw = (kt_all.astype(f32) * wsel).astype(bf16)
        upd = _dot(ktw, jnp.concatenate([v_all, ones_all], axis=1))
        asel4 = jnp.concatenate([asel] * (2 * GROUP_W // L), axis=1)
        return (asel4 * cn + jnp.where(blockdiag4, upd, 0.0), b_end + mm), out

    def segment(q_ref, kt_ref, gr_ref, hs_ref, carry):
        seg = q_ref.shape[0] // SCAN_BATCH
        n = seg // L

        def body(i, carry):
            out, pieces = [], []
            for g, (fw, bw) in enumerate(carry):
                fw, h_f = direction(q_ref, kt_ref, gr_ref, i, False, fw, g * seg)
                bw, h_b = direction(q_ref, kt_ref, gr_ref, n - 1 - i, True, bw, g * seg)
                out.append((fw, bw))
                pieces += [h_f, h_b]
            hs_ref[i] = jnp.concatenate(pieces, axis=1)
            return tuple(out)

        return lax.fori_loop(0, n, body, carry, unroll=SCAN_UNROLL)

    def init():
        return jnp.zeros((GROUP_W, 2 * GROUP_W), f32), jnp.zeros((HEADS, L), f32)

    carry = segment(qc_ref, ktc, grc_ref, hsc, tuple((init(), init()) for _ in range(SCAN_BATCH)))
    segment(ql_ref, ktl, grl_ref, hsl, carry)

    gmat = _group_mean_matrix(GROUP_W, HEAD_DIM)

    def finish(q_ref, hs_ref, o_ref):
        seg = o_ref.shape[0] // SCAN_BATCH
        n = seg // L
        for g in range(SCAN_BATCH):
            for j in range(n):
                r = g * seg + j * L
                c0 = g * 2 * GROUP_W
                hh = hs_ref[j, :, c0:c0 + GROUP_W] + hs_ref[n - 1 - j, :, c0 + GROUP_W:c0 + 2 * GROUP_W]
                hn = hh * _group_rsqrt(hh, gmat) * on_ref[...]
                og = q_ref[r:r + L, 3 * GROUP_W:4 * GROUP_W].astype(f32)
                o_ref[r:r + L, :] = (_sigmoid(og) * hn).astype(o_ref.dtype)

    finish(ql_ref, hsl, ol_ref)
    if need_ctx:
        finish(qc_ref, hsc, oc_ref)
    else:
        oc_ref[...] = jnp.zeros(oc_ref.shape, oc_ref.dtype)


def _mlstm(qkvo_c, qkvo_l, gt_c, gt_l, gate_b, out_norm, batch, need_ctx):
    cl, t = qkvo_c.shape[0] // batch, qkvo_l.shape[0] // batch
    ng = 4 * HEADS
    bc, bt = SCAN_BATCH * cl, SCAN_BATCH * t
    gb_row = gate_b.reshape(ng, 1)
    qc_spec, ql_spec = _seg_blocks(bc, bt, 4 * GROUP_W)
    grc_spec, grl_spec = _seg_blocks_t(bc, bt, ng, GT_MG)
    oc_spec, ol_spec = _seg_blocks(bc, bt, GROUP_W)
    return pl.pallas_call(
        functools.partial(_mlstm_body, need_ctx=need_ctx),
        grid=(batch // SCAN_BATCH,),
        in_specs=[qc_spec, ql_spec, grc_spec, grl_spec, _resident((ng, 1)), _resident((1, GROUP_W))],
        out_specs=[oc_spec, ol_spec],
        out_shape=[jax.ShapeDtypeStruct((batch * cl, GROUP_W), bf16),
                   jax.ShapeDtypeStruct((batch * t, GROUP_W), bf16)],
        scratch_shapes=[pltpu.VMEM((cl // SCAN_CHUNK, SCAN_CHUNK, 2 * SCAN_BATCH * GROUP_W), f32),
                        pltpu.VMEM((t // SCAN_CHUNK, SCAN_CHUNK, 2 * SCAN_BATCH * GROUP_W), f32),
                        pltpu.VMEM((GROUP_W, bc), bf16), pltpu.VMEM((GROUP_W, bt), bf16)],
        compiler_params=_params(("parallel",)),
        name="mlstm",
    )(qkvo_c, qkvo_l, gt_c, gt_l, gb_row, out_norm.reshape(1, GROUP_W))


def _ssd_body(zc_ref, zl_ref, xc_ref, xl_ref, drc_ref, drl_ref, cw_ref, cb_ref, dbr_ref,
              alr_ref, dsk_ref, ng_ref, oc_ref, ol_ref,
              xac, xal, ysc, ysl, xpad, btc, btl, *, need_ctx):
    L = SCAN_CHUNK
    N = SSD_STATE
    lower, upper = _tri_masks(L)
    tri_lo = jnp.where(lower, 1.0, 0.0).astype(bf16)
    tri_up = jnp.where(upper, 1.0, 0.0).astype(bf16)
    a_row = -jnp.exp(alr_ref[...])
    pad = 8
    half = SSD_CONV // 2

    def conv_act(x_ref, xa_ref):
        n = x_ref.shape[0] // SCAN_BATCH
        zeros = jnp.zeros((pad, SSD_XBC), f32)
        blk = 256
        for g in range(SCAN_BATCH):
            xpad[0:pad, :] = zeros
            xpad[pad + n:2 * pad + n, :] = zeros
            for r in range(0, n, blk):
                xpad[pad + r:pad + r + blk, :] = x_ref[g * n + r:g * n + r + blk, :].astype(f32)
            for r in range(0, n, blk):
                y = jnp.zeros((blk, SSD_XBC), f32) + cb_ref[...]
                for kk in range(SSD_CONV):
                    o = pad + r + kk - half
                    y = y + cw_ref[kk:kk + 1, :] * xpad[o:o + blk, :]
                xa_ref[g * n + r:g * n + r + blk, :] = _silu(y)

    conv_act(xc_ref, xac)

    PW = 2 * HEAD_DIM
    lane = lax.broadcasted_iota(jnp.int32, (1, PW), 1)
    half0 = lane < HEAD_DIM
    row_half0 = lax.broadcasted_iota(jnp.int32, (PW, 1), 0) < N
    rr = lax.broadcasted_iota(jnp.int32, (PW, PW), 0)
    cc = lax.broadcasted_iota(jnp.int32, (PW, PW), 1)
    eye = jnp.where(rr == cc, 1.0, 0.0).astype(bf16)
    zero_b = jnp.zeros((), bf16)

    def transpose_b(xa_ref, bt_ref):
        for r in range(0, xa_ref.shape[0], L):
            bm = xa_ref[r:r + L, GROUP_W:GROUP_W + PW].astype(bf16)
            bt_ref[:, r:r + L] = _dot_nt(eye, bm).astype(bf16)

    transpose_b(xac, btc)

    def direction(xa_ref, bt_ref, dr_ref, j, rev, states, off):
        r0 = pl.multiple_of(off + j * L, L)
        base = HEADS if rev else 0
        mask = upper if rev else lower
        dtr = _softplus(dr_ref[:, pl.ds(r0, L)] + dbr_ref[...])
        ar = dtr * a_row
        cumr = _dot_tri(ar, tri_lo if rev else tri_up)
        cumc = jnp.transpose(cumr)
        cum_end = jnp.broadcast_to(jnp.sum(ar[base:base + HEADS, :], axis=1, keepdims=True), (HEADS, L))
        dec_all = jnp.exp(cum_end - cumr[base:base + HEADS, :]) * dtr[base:base + HEADS, :]
        aexp_all = jnp.exp(cum_end)
        bp = xa_ref[pl.ds(r0, L), GROUP_W:GROUP_W + PW].astype(bf16)
        cp = xa_ref[pl.ds(r0, L), GROUP_W + PW:GROUP_W + 2 * PW].astype(bf16)
        btp = bt_ref[:, pl.ds(r0, L)].astype(f32)
        new_states, outs = [], []
        for g in range(SSD_GROUPS):
            G = _dot_nt(jnp.where(half0 if g == 0 else ~half0, cp, zero_b), bp)
            xpb = xa_ref[pl.ds(r0, L), g * PW:(g + 1) * PW].astype(bf16)
            uu = _dot(jnp.concatenate([btp * dec_all[2 * g:2 * g + 1, :], btp * dec_all[2 * g + 1:2 * g + 2, :]],
                                      axis=0).astype(bf16), xpb)
            ws, ecs, us = [], [], []
            for e in range(2):
                h = 2 * g + e
                idx = base + h
                cum_cb = jnp.broadcast_to(cumc[:, idx:idx + 1], (L, L))
                seg = jnp.exp(jnp.where(mask, cum_cb - cumr[idx:idx + 1, :], NEG))
                ws.append((G * seg * dtr[idx:idx + 1, :]).astype(bf16))
                ecs.append(jnp.exp(cum_cb))
                us.append(aexp_all[h:h + 1, :] * states[g] + uu[e * PW:(e + 1) * PW])
            xbd = jnp.concatenate([jnp.where(half0, xpb, zero_b), jnp.where(half0, zero_b, xpb)], axis=0)
            y = _dot(jnp.concatenate(ws, axis=1), xbd) + jnp.where(half0, ecs[0], ecs[1]) * _dot(cp, states[g].astype(bf16))
            outs.append(y)
            new_states.append(jnp.where(row_half0 if g == 0 else ~row_half0, jnp.where(half0, us[0], us[1]), 0.0))
        return new_states, jnp.concatenate(outs, axis=1)

    def segment(xa_ref, bt_ref, dr_ref, ys_ref, carry):
        seg = xa_ref.shape[0] // SCAN_BATCH
        n = seg // L

        def body(i, carry):
            out, pieces = [], []
            for g, (fw, bw) in enumerate(carry):
                fw, y_f = direction(xa_ref, bt_ref, dr_ref, i, False, fw, g * seg)
                bw, y_b = direction(xa_ref, bt_ref, dr_ref, n - 1 - i, True, bw, g * seg)
                out.append((fw, bw))
                pieces += [y_f, y_b]
            ys_ref[i] = jnp.concatenate(pieces, axis=1)
            return tuple(out)

        return lax.fori_loop(0, n, body, carry, unroll=SCAN_UNROLL)

    init = [jnp.zeros((PW, PW), f32) for _ in range(SSD_GROUPS)]
    carry = segment(xac, btc, drc_ref, ysc, tuple((init, list(init)) for _ in range(SCAN_BATCH)))
    conv_act(xl_ref, xal)
    transpose_b(xal, btl)
    segment(xal, btl, drl_ref, ysl, carry)

    gmat = _group_mean_matrix(GROUP_W, SSD_NORM_GROUP)

    def finish(z_ref, xa_ref, ys_ref, o_ref):
        seg = o_ref.shape[0] // SCAN_BATCH
        n = seg // L
        for g in range(SCAN_BATCH):
            for j in range(n):
                r = g * seg + j * L
                c0 = g * 2 * GROUP_W
                yy = ys_ref[j, :, c0:c0 + GROUP_W] + ys_ref[n - 1 - j, :, c0 + GROUP_W:c0 + 2 * GROUP_W] \
                    + dsk_ref[...] * xa_ref[r:r + L, 0:GROUP_W]
                gt = yy * _silu(z_ref[r:r + L, :].astype(f32))
                o_ref[r:r + L, :] = (gt * _group_rsqrt(gt, gmat) * ng_ref[...]).astype(o_ref.dtype)

    finish(zl_ref, xal, ysl, ol_ref)
    if need_ctx:
        finish(zc_ref, xac, ysc, oc_ref)
    else:
        oc_ref[...] = jnp.zeros(oc_ref.shape, oc_ref.dtype)


def _ssd(z_c, z_l, xbc_c, xbc_l, gt_c, gt_l, conv_w, conv_b, dt_bias, a_log, d_skip, norm_g, batch, need_ctx):
    cl, t = z_c.shape[0] // batch, z_l.shape[0] // batch
    nd = 2 * HEADS
    bc, bt = SCAN_BATCH * cl, SCAN_BATCH * t
    cw = jnp.zeros((8, SSD_XBC), f32).at[:SSD_CONV].set(conv_w)
    zc_spec, zl_spec = _seg_blocks(bc, bt, GROUP_W)
    xc_spec, xl_spec = _seg_blocks(bc, bt, SSD_XBC)
    drc_spec, drl_spec = _seg_blocks_t(bc, bt, nd, GT_DT)
    oc_spec, ol_spec = _seg_blocks(bc, bt, GROUP_W)
    return pl.pallas_call(
        functools.partial(_ssd_body, need_ctx=need_ctx),
        grid=(batch // SCAN_BATCH,),
        in_specs=[zc_spec, zl_spec, xc_spec, xl_spec, drc_spec, drl_spec,
                  _resident((8, SSD_XBC)), _resident((1, SSD_XBC)), _resident((nd, 1)), _resident((nd, 1)),
                  _resident((1, GROUP_W)), _resident((1, GROUP_W))],
        out_specs=[oc_spec, ol_spec],
        out_shape=[jax.ShapeDtypeStruct((batch * cl, GROUP_W), bf16),
                   jax.ShapeDtypeStruct((batch * t, GROUP_W), bf16)],
        scratch_shapes=[pltpu.VMEM((bc, SSD_XBC), f32), pltpu.VMEM((bt, SSD_XBC), f32),
                        pltpu.VMEM((cl // SCAN_CHUNK, SCAN_CHUNK, 2 * SCAN_BATCH * GROUP_W), f32),
                        pltpu.VMEM((t // SCAN_CHUNK, SCAN_CHUNK, 2 * SCAN_BATCH * GROUP_W), f32),
                        pltpu.VMEM((t + 16, SSD_XBC), f32),
                        pltpu.VMEM((2 * SSD_STATE, bc), bf16), pltpu.VMEM((2 * SSD_STATE, bt), bf16)],
        compiler_params=_params(("parallel",)),
        name="ssd",
    )(z_c, z_l, xbc_c, xbc_l, gt_c, gt_l, cw, conv_b.reshape(1, SSD_XBC),
      dt_bias.reshape(nd, 1), a_log.reshape(nd, 1),
      jnp.repeat(d_skip, HEAD_DIM).reshape(1, GROUP_W), norm_g.reshape(1, GROUP_W))


def _mla_body(ac_ref, al_ref, qn_ref, kvn_ref, wq_ref, wkv_ref, qg_ref, kg_ref, krg_ref, cos_ref, sin_ref,
              oc_ref, ol_ref, q_s, k_s, vt_s, *, need_ctx):
    cl, t = ac_ref.shape[0], al_ref.shape[0]
    scale = MLA_QK ** -0.5 * LOG2E
    rr = lax.broadcasted_iota(jnp.int32, (LANES, LANES), 0)
    cc = lax.broadcasted_iota(jnp.int32, (LANES, LANES), 1)
    in_nope = (rr < MLA_NOPE) & (cc < MLA_NOPE)
    in_rope = (rr >= MLA_NOPE) & (rr < MLA_QK) & (cc >= MLA_NOPE) & (cc < MLA_QK)
    head_gmat = jnp.where(in_nope, 1.0 / MLA_NOPE, jnp.where(in_rope, 1.0 / MLA_ROPE, 0.0)).astype(bf16)
    rot = _rotate_half_matrix(MLA_ROPE // 4)
    er = lax.broadcasted_iota(jnp.int32, (GROUP_W, GROUP_W), 0)
    ec = lax.broadcasted_iota(jnp.int32, (GROUP_W, GROUP_W), 1)
    eye = jnp.where(er == ec, 1.0, 0.0).astype(bf16)

    def head_norm(x):
        return x * _group_rsqrt(x, head_gmat)

    def project(a_ref, row0, n, roped):
        blk = 256
        for r in range(0, n, blk):
            a = a_ref[r:r + blk, :].astype(f32)
            aq, akv, akr = a[:, 0:GROUP_W], a[:, GROUP_W:GROUP_W + LANES], a[:, GROUP_W + LANES:GROUP_W + 2 * LANES]
            qh = _dot((aq * _row_rsqrt(aq, GROUP_W) * qn_ref[...]).astype(bf16), wq_ref[...])
            kv = _dot((akv * _row_rsqrt(akv, kvn_ref.shape[1]) * kvn_ref[...]).astype(bf16), wkv_ref[...])
            kr = akr * _row_rsqrt(akr, MLA_ROPE) * krg_ref[...]
            if roped:
                cos, sin = cos_ref[r:r + blk, :], sin_ref[r:r + blk, :]
                kr = _rope(kr, cos, sin, rot)
            for h in range(HEADS):
                qx = head_norm(qh[:, h * LANES:(h + 1) * LANES]) * qg_ref[...]
                if roped:
                    qx = _rope(qx, cos, sin, rot)
                q_s[h, row0 + r:row0 + r + blk, :] = (qx * scale).astype(bf16)
                kx = kv[:, h * LANES:(h + 1) * LANES]
                kx = kx * _row_rsqrt(kx, MLA_NOPE) * kg_ref[...]
                k_s[h, row0 + r:row0 + r + blk, :] = (kx + kr).astype(bf16)
            vt_s[:, row0 + r:row0 + r + blk] = _dot_nt(eye, kv[:, HEADS * LANES:].astype(bf16)).astype(bf16)

    project(ac_ref, 0, cl, False)
    project(al_ref, cl, t, True)

    def attend(q0, nq, nk, o_ref, o0):
        outs = []
        sts = [_dot_nt(k_s[h, 0:nk, :], q_s[h, pl.ds(q0, nq), :]) for h in range(HEADS)]
        for h in range(HEADS):
            st = sts[h]
            p = jnp.exp2(st - jnp.max(st, axis=0, keepdims=True))
            den = jnp.sum(p, axis=0, keepdims=True)
            outs.append(_dot(vt_s[h * HEAD_DIM:(h + 1) * HEAD_DIM, 0:nk], p.astype(bf16)) / den)
        ot = jnp.concatenate(outs, axis=0).astype(bf16)
        for c in range(0, nq, GROUP_W):
            o_ref[pl.ds(o0 + c, GROUP_W), :] = _dot_nt(eye, ot[:, c:c + GROUP_W]).astype(o_ref.dtype)

    def body(i, _):
        o0 = pl.multiple_of(i * MLA_QBLK, MLA_QBLK)
        attend(cl + o0, MLA_QBLK, cl + t, ol_ref, o0)
        return 0

    lax.fori_loop(0, t // MLA_QBLK, body, 0)
    if need_ctx:
        attend(0, cl, cl, oc_ref, 0)
    else:
        oc_ref[...] = jnp.zeros(oc_ref.shape, oc_ref.dtype)


def _mla(a_c, a_l, q_norm, kv_norm, wq_b, wkv_b, q_gain, k_gain, cos, sin, batch, need_ctx):
    cl, t = a_c.shape[0] // batch, a_l.shape[0] // batch
    kvl = kv_norm.shape[0]
    wq = jnp.zeros((GROUP_W, HEADS, LANES), f32).at[:, :, :MLA_QK].set(wq_b.reshape(GROUP_W, HEADS, MLA_QK))
    wkv4 = wkv_b.reshape(kvl, HEADS, MLA_NOPE + HEAD_DIM)
    wk = jnp.zeros((kvl, HEADS, LANES), f32).at[:, :, :MLA_NOPE].set(wkv4[:, :, :MLA_NOPE])
    wkv = jnp.concatenate([wk.reshape(kvl, HEADS * LANES), wkv4[:, :, MLA_NOPE:].reshape(kvl, GROUP_W)], axis=1)
    slab = lambda v, off: jnp.zeros((1, LANES), f32).at[0, off:off + v.shape[0]].set(v)
    qg = slab(q_gain, 0)
    kg = slab(k_gain[:MLA_NOPE], 0)
    krg = slab(k_gain[MLA_NOPE:], MLA_NOPE)
    ac_spec, al_spec = _seg_blocks(cl, t, 2 * GROUP_W)
    oc_spec, ol_spec = _seg_blocks(cl, t, GROUP_W)
    return pl.pallas_call(
        functools.partial(_mla_body, need_ctx=need_ctx),
        grid=(batch,),
        in_specs=[ac_spec, al_spec, _resident((1, GROUP_W)), _resident((1, LANES)),
                  _resident((GROUP_W, HEADS * LANES)), _resident((kvl, HEADS * LANES + GROUP_W)),
                  _resident((1, LANES)), _resident((1, LANES)), _resident((1, LANES)),
                  _resident((t, LANES)), _resident((t, LANES))],
        out_specs=[oc_spec, ol_spec],
        out_shape=[jax.ShapeDtypeStruct((batch * cl, GROUP_W), bf16),
                   jax.ShapeDtypeStruct((batch * t, GROUP_W), bf16)],
        scratch_shapes=[pltpu.VMEM((HEADS, cl + t, LANES), bf16), pltpu.VMEM((HEADS, cl + t, LANES), bf16),
                        pltpu.VMEM((GROUP_W, cl + t), bf16)],
        compiler_params=_params(("parallel",)),
        name="mla",
    )(a_c, a_l, q_norm.reshape(1, GROUP_W), kv_norm.reshape(1, kvl),
      wq.reshape(GROUP_W, HEADS * LANES).astype(bf16), wkv.astype(bf16), qg, kg, krg, cos, sin)


def _swa_body(wc_ref, wl_ref, qg_ref, kg_ref, sink_ref, cos_ref, sin_ref, oc_ref, ol_ref,
              q_s, qc_s, k_s, v_s, kc_s, vc_s, *, need_ctx):
    cl, t = wc_ref.shape[0], wl_ref.shape[0]
    scale = HEAD_DIM ** -0.5 * LOG2E
    kvw = SWA_KV_HEADS * HEAD_DIM
    blk = SWA_BLK
    gmat_q = _group_mean_matrix(GROUP_W, HEAD_DIM)
    gmat_k = _group_mean_matrix(kvw, HEAD_DIM)
    rot = _rotate_half_matrix(HEAD_DIM // 4)

    def project(w_ref, n, roped, qdst, kdst, vdst, row0):
        step = 256
        for r in range(0, n, step):
            w = w_ref[r:r + step, :].astype(f32)
            q, k = w[:, 0:GROUP_W], w[:, GROUP_W:GROUP_W + kvw]
            q = q * _group_rsqrt(q, gmat_q) * qg_ref[...]
            k = k * _group_rsqrt(k, gmat_k) * kg_ref[...]
            if roped:
                cos, sin = cos_ref[r:r + step, :], sin_ref[r:r + step, :]
                q = jnp.concatenate([_rope(q[:, 0:LANES], cos, sin, rot),
                                     _rope(q[:, LANES:2 * LANES], cos, sin, rot)], axis=1)
                k = _rope(k, cos, sin, rot)
            qdst[r:r + step, :] = (q * scale).astype(bf16)
            kdst[row0 + r:row0 + r + step, :] = k.astype(bf16)
            vdst[row0 + r:row0 + r + step, :] = w_ref[r:r + step, GROUP_W + kvw:GROUP_W + 2 * kvw]

    zeros = jnp.zeros((blk, kvw), bf16)
    for s in (k_s, v_s):
        s[0:blk, :] = zeros
        s[blk + t:2 * blk + t, :] = zeros
    project(wc_ref, cl, False, qc_s, kc_s, vc_s, 0)
    project(wl_ref, t, True, q_s, k_s, v_s, blk)

    rr = lax.broadcasted_iota(jnp.int32, (HEADS * blk, 3 * blk), 0) % blk
    jj = lax.broadcasted_iota(jnp.int32, (HEADS * blk, 3 * blk), 1)
    band = (jj - rr >= 0) & (jj - rr <= 2 * SWA_WINDOW)
    half0 = lax.broadcasted_iota(jnp.int32, (1, LANES), 1) < HEAD_DIM
    zero_b = jnp.zeros((), bf16)

    def wide(x, n):
        return jnp.broadcast_to(x, (x.shape[0], n))

    def stacked_queries(q_ref, r0, n):
        qa, qb = q_ref[pl.ds(r0, n), 0:LANES], q_ref[pl.ds(r0, n), LANES:2 * LANES]
        return jnp.concatenate([jnp.where(half0, qa, zero_b), jnp.where(half0, qb, zero_b),
                                jnp.where(half0, zero_b, qa), jnp.where(half0, zero_b, qb)], axis=0)

    def stacked_sink(n):
        row = lax.broadcasted_iota(jnp.int32, (HEADS * n, 1), 0)
        return LOG2E * jnp.where(row < n, sink_ref[0:1, :], jnp.where(row < 2 * n, sink_ref[1:2, :],
                                 jnp.where(row < 3 * n, sink_ref[2:3, :], sink_ref[3:4, :])))

    def unstack(o, n):
        return jnp.concatenate([jnp.where(half0, o[0:n], o[2 * n:3 * n]),
                                jnp.where(half0, o[n:2 * n], o[3 * n:4 * n])], axis=1)

    def lat_block(n, _):
        r0 = pl.multiple_of(n * blk, blk)
        kpos = jj + (n - 1) * blk
        valid = band & (kpos >= 0) & (kpos < t)
        q4 = stacked_queries(q_s, r0, blk)
        s = _dot_nt(q4, jnp.concatenate([k_s[pl.ds(r0, 3 * blk), :], kc_s[...]], axis=0))
        s_loc, s_ctx = jnp.where(valid, s[:, 0:3 * blk], NEG), s[:, 3 * blk:]
        sink = stacked_sink(blk)
        m = jnp.maximum(wide(jnp.maximum(jnp.max(s_loc, axis=1, keepdims=True),
                                         jnp.max(s_ctx, axis=1, keepdims=True)), LANES), sink)
        p_loc = jnp.exp2(s_loc - jnp.concatenate([m] * 3, axis=1))
        p_ctx = jnp.exp2(s_ctx - jnp.concatenate([m] * (cl // LANES), axis=1))
        den = wide(jnp.sum(p_loc, axis=1, keepdims=True) + jnp.sum(p_ctx, axis=1, keepdims=True), LANES) \
            + jnp.exp2(sink - m)
        o = _dot(jnp.concatenate([p_loc, p_ctx], axis=1).astype(bf16),
                 jnp.concatenate([v_s[pl.ds(r0, 3 * blk), :], vc_s[...]], axis=0))
        ol_ref[pl.ds(r0, blk), :] = unstack(o / den, blk).astype(ol_ref.dtype)
        return 0

    lax.fori_loop(0, t // blk, lat_block, 0, unroll=SWA_UNROLL)

    if need_ctx:
        half = cl // 2
        for r0 in (0, half):
            s = _dot_nt(stacked_queries(qc_s, r0, half), kc_s[...])
            sink = stacked_sink(half)
            m = jnp.maximum(wide(jnp.max(s, axis=1, keepdims=True), LANES), sink)
            p = jnp.exp2(s - jnp.concatenate([m] * (cl // LANES), axis=1))
            den = wide(jnp.sum(p, axis=1, keepdims=True), LANES) + jnp.exp2(sink - m)
            oc_ref[r0:r0 + half, :] = unstack(_dot(p.astype(bf16), vc_s[...]) / den, half).astype(oc_ref.dtype)
    else:
        oc_ref[...] = jnp.zeros(oc_ref.shape, oc_ref.dtype)


def _swa_head_order(a, axis):
    blocks = jnp.split(a, HEADS, axis=axis)
    return jnp.concatenate([blocks[0], blocks[2], blocks[1], blocks[3]], axis=axis)


def _swa(w_c, w_l, q_gain, k_gain, sink, cos, sin, batch, need_ctx):
    cl, t = w_c.shape[0] // batch, w_l.shape[0] // batch
    kvw = SWA_KV_HEADS * HEAD_DIM
    wc_spec, wl_spec = _seg_blocks(cl, t, 2 * GROUP_W)
    oc_spec, ol_spec = _seg_blocks(cl, t, GROUP_W)
    return pl.pallas_call(
        functools.partial(_swa_body, need_ctx=need_ctx),
        grid=(batch,),
        in_specs=[wc_spec, wl_spec, _resident((1, GROUP_W)), _resident((1, kvw)), _resident((HEADS, LANES)),
                  _resident((t, LANES)), _resident((t, LANES))],
        out_specs=[oc_spec, ol_spec],
        out_shape=[jax.ShapeDtypeStruct((batch * cl, GROUP_W), bf16),
                   jax.ShapeDtypeStruct((batch * t, GROUP_W), bf16)],
        scratch_shapes=[pltpu.VMEM((t, GROUP_W), bf16), pltpu.VMEM((cl, GROUP_W), bf16),
                        pltpu.VMEM((t + 2 * SWA_BLK, kvw), bf16), pltpu.VMEM((t + 2 * SWA_BLK, kvw), bf16),
                        pltpu.VMEM((cl, kvw), bf16), pltpu.VMEM((cl, kvw), bf16)],
        compiler_params=_params(("parallel",)),
        name="swa",
    )(w_c, w_l, jnp.tile(q_gain, HEADS).reshape(1, GROUP_W), jnp.tile(k_gain, SWA_KV_HEADS).reshape(1, kvw),
      jnp.broadcast_to(sink.reshape(HEADS, 1), (HEADS, LANES)), cos, sin)


def _rope_tables(t, rot_dim, lane0):
    pos = jnp.arange(t)
    row, col = (pos // GRID_W).astype(f32), (pos % GRID_W).astype(f32)
    nf = rot_dim // 4
    inv = ROPE_BASE ** (-jnp.arange(nf, dtype=f32) / nf)
    ar, ac = row[:, None] * inv, col[:, None] * inv
    ang = jnp.concatenate([ar, ar, ac, ac], axis=-1)
    return jnp.cos(ang), jnp.sin(ang)


def _pack_in_weight(w_in):
    o = 0
    seg = {}
    for name, n in (("qkvo", 4 * GROUP_W), ("mg", 4 * HEADS), ("a_q", GROUP_W), ("a_kv", GROUP_W // 2),
                    ("a_kr", MLA_ROPE), ("swa", 2 * GROUP_W), ("z", GROUP_W), ("xbc", SSD_XBC), ("dt", 2 * HEADS)):
        seg[name] = w_in[:, o:o + n]
        o += n
    d = w_in.shape[0]
    zeros = lambda n: jnp.zeros((d, n), w_in.dtype)
    packed = jnp.concatenate([
        seg["qkvo"],
        seg["a_q"], seg["a_kv"], zeros(MLA_NOPE), seg["a_kr"], zeros(LANES - MLA_QK),
        _swa_head_order(seg["swa"][:, :GROUP_W], 1), seg["swa"][:, GROUP_W:],
        seg["z"], seg["xbc"]], axis=1)
    gate_t = jnp.concatenate([seg["mg"], seg["dt"], zeros(GT_ROWS - 4 * HEADS - 2 * HEADS)], axis=1).T
    return packed.astype(bf16), gate_t.astype(bf16)


def kernel(x, c, ctx, c_ctx, w_mod, b_mod, ffn1_norm, ffn1_wi, ffn1_wo, mix_norm, w_in, w_out, mlstm_gate_b, mlstm_out_norm, mla_q_norm, mla_kv_norm, mla_wq_b, mla_wkv_b, mla_q_gain, mla_k_gain, swa_q_gain, swa_k_gain, swa_sink, ssd_conv_w, ssd_conv_b, ssd_dt_bias, ssd_a_log, ssd_d, ssd_norm, ffn2_norm, ffn2_wi, ffn2_wo):
    b, t, d = x.shape
    cl = ctx.shape[1]
    depth = w_mod.shape[0]
    dff = ffn1_wo.shape[1]
    lat_tiles = t // ROW_TILE

    cos_m, sin_m = _rope_tables(t, MLA_ROPE, MLA_NOPE)
    pad_id = lambda tab, fill: jnp.concatenate(
        [jnp.full((t, MLA_NOPE), fill, f32), tab, jnp.full((t, LANES - MLA_QK), fill, f32)], axis=1)
    cos_m, sin_m = pad_id(cos_m, 1.0), pad_id(sin_m, 0.0)
    cos_s, sin_s = _rope_tables(t, HEAD_DIM, 0)
    cos_s, sin_s = jnp.tile(cos_s, (1, LANES // HEAD_DIM)), jnp.tile(sin_s, (1, LANES // HEAD_DIM))

    h = x.reshape(b * t, d)
    hc = ctx.reshape(b * cl, d)
    cc = jnp.concatenate([c, c_ctx[None, :]], axis=0)
    for l in range(depth):
        need_ctx = l < depth - 1
        mod = _modulation(cc, w_mod, b_mod, l).reshape(b + 1, N_MOD, d)
        streams = ((0, lat_tiles), (b, None))

        ffn1_w = (ffn1_norm[l].reshape(1, d), ffn1_wi[l][:, :dff].astype(bf16), ffn1_wi[l][:, dff:].astype(bf16),
                  ffn1_wo[l].astype(bf16))
        ffn2_w = (ffn2_norm[l].reshape(1, d), ffn2_wi[l][:, :dff].astype(bf16), ffn2_wi[l][:, dff:].astype(bf16),
                  ffn2_wo[l].astype(bf16))
        h = _ffn(h, mod, *ffn1_w, 0, *streams[0])
        hc = _ffn(hc, mod, *ffn1_w, 0, *streams[1])
        w_packed, w_gate_t = _pack_in_weight(w_in[l])
        in_w = (mix_norm[l].reshape(1, d), w_packed, w_gate_t)
        qkvo_l, mla_l, swa_l, z_l, xbc_l, gt_l = _inproj(h, mod, *in_w, *streams[0])
        qkvo_c, mla_c, swa_c, z_c, xbc_c, gt_c = _inproj(hc, mod, *in_w, *streams[1])
        a_c, a_l = _mlstm(qkvo_c, qkvo_l, gt_c, gt_l, mlstm_gate_b[l], mlstm_out_norm[l], b, need_ctx)
        m_c, m_l = _mla(mla_c, mla_l, mla_q_norm[l], mla_kv_norm[l], mla_wq_b[l], mla_wkv_b[l], mla_q_gain[l],
                        mla_k_gain[l], cos_m, sin_m, b, need_ctx)
        s_c, s_l = _swa(swa_c, swa_l, swa_q_gain[l], swa_k_gain[l], swa_sink[l], cos_s, sin_s, b, need_ctx)
        d_c, d_l = _ssd(z_c, z_l, xbc_c, xbc_l, gt_c, gt_l, ssd_conv_w[l], ssd_conv_b[l], ssd_dt_bias[l],
                        ssd_a_log[l], ssd_d[l], ssd_norm[l], b, need_ctx)
        wo_b = jnp.concatenate([w_out[l][:2 * GROUP_W], _swa_head_order(w_out[l][2 * GROUP_W:3 * GROUP_W], 0),
                                w_out[l][3 * GROUP_W:]], axis=0).astype(bf16)
        h = _mix_ffn(h, mod, (a_l, m_l, s_l, d_l), wo_b, *ffn2_w, *streams[0])
        if need_ctx:
            hc = _mix_ffn(hc, mod, (a_c, m_c, s_c, d_c), wo_b, *ffn2_w, *streams[1])
    return h.reshape(b, t, d)
```

```python
import functools

import jax
import jax.numpy as jnp
from jax import lax
from jax.experimental import pallas as pl
from jax.experimental.pallas import tpu as pltpu

f32 = jnp.float32
bf16 = jnp.bfloat16

RMS_EPS = 1e-6
ROPE_BASE = 10000.0
GRID_W = 64
N_MOD = 9
HEADS = 4
HEAD_DIM = 64
GROUP_W = HEADS * HEAD_DIM
MLA_NOPE = 64
MLA_ROPE = 32
MLA_QK = MLA_NOPE + MLA_ROPE
SWA_KV_HEADS = 2
SWA_WINDOW = 128
SWA_BLK = 128
SSD_STATE = 64
SSD_GROUPS = 2
SSD_CONV = 5
SSD_XBC = GROUP_W + 2 * SSD_GROUPS * SSD_STATE
SSD_NORM_GROUP = 128

LANES = 128
ROW_TILE = 512
WIDE_ROW_TILE = 1024
FF_CHUNK = 256
SCAN_CHUNK = 128
SCAN_UNROLL = 4
SWA_UNROLL = 4
SCAN_BATCH = 2
MLA_QBLK = 512
NEG = -1e30
LOG2E = 1.4426950408889634
VMEM_LIMIT = 56 * 1024 * 1024

P_QKVO = 0
P_MLA = 1024
P_SWA = 1536
P_Z = 2048
P_XBC = 2304
P_COLS = 2816
GT_MG = 0
GT_DT = 4 * HEADS
GT_ROWS = 32


def _dot(a, b):
    return jnp.dot(a, b, preferred_element_type=f32)


def _dot_nt(a, b):
    return lax.dot_general(a, b, (((1,), (1,)), ((), ())), preferred_element_type=f32)


def _sigmoid(x):
    return 1.0 / (1.0 + jnp.exp(-x))


def _silu(x):
    return x * _sigmoid(x)


def _softplus(x):
    return jnp.maximum(x, 0.0) + jnp.log(1.0 + jnp.exp(-jnp.abs(x)))


def _log_sigmoid(x):
    return -_softplus(-x)


def _split3(x):
    hi = x.astype(bf16)
    r = x - hi.astype(f32)
    mid = r.astype(bf16)
    lo = (r - mid.astype(f32)).astype(bf16)
    return hi, mid, lo


def _dot_tri(x, tri):
    m = x.shape[0]
    terms = jnp.concatenate([p.astype(f32) for p in _split3(x)], axis=0).astype(bf16)
    y = _dot(terms, tri)
    return y[0:m] + y[m:2 * m] + y[2 * m:3 * m]


def _dot2(x, w):
    hi = x.astype(bf16)
    mid = (x - hi.astype(f32)).astype(bf16)
    return _dot(hi, w) + _dot(mid, w)


def _group_mean_matrix(n, gsz):
    r = lax.broadcasted_iota(jnp.int32, (n, n), 0) // gsz
    c = lax.broadcasted_iota(jnp.int32, (n, n), 1) // gsz
    return jnp.where(r == c, 1.0 / gsz, 0.0).astype(bf16)


def _group_rsqrt(x, gmat):
    return lax.rsqrt(_dot2(x * x, gmat) + RMS_EPS)


def _row_rsqrt(x, n_real):
    return lax.rsqrt(jnp.sum(x * x, axis=1, keepdims=True) * (1.0 / n_real) + RMS_EPS)


def _rotate_half_matrix(quarter):
    r = lax.broadcasted_iota(jnp.int32, (LANES, LANES), 0)
    c = lax.broadcasted_iota(jnp.int32, (LANES, LANES), 1)
    first = (c % (2 * quarter)) < quarter
    return jnp.where(first & (r == c + quarter), -1.0, jnp.where(~first & (r == c - quarter), 1.0, 0.0)).astype(bf16)


def _rope(x, cos, sin, rot):
    return x * cos + _dot2(x, rot) * sin


def _resident(shape):
    nd = len(shape)
    return pl.BlockSpec(shape, lambda *_: (0,) * nd, pipeline_mode=pl.Buffered(1))


def _params(sem):
    return pltpu.CompilerParams(dimension_semantics=sem, vmem_limit_bytes=VMEM_LIMIT)


def _mod_body(c_ref, w_ref, b_ref, o_ref):
    s = _silu(c_ref[...])
    o_ref[...] = jnp.dot(s, w_ref[...], preferred_element_type=f32,
                         precision=lax.Precision.HIGHEST) + b_ref[...]


def _modulation(cc, w_all, b_all, layer):
    m, d = cc.shape
    depth, _, n = w_all.shape
    tn = 1024
    return pl.pallas_call(
        _mod_body,
        grid=(n // tn,),
        in_specs=[pl.BlockSpec((m, d), lambda j: (0, 0)),
                  pl.BlockSpec((None, d, tn), lambda j: (layer, 0, j)),
                  pl.BlockSpec((None, 1, tn), lambda j: (layer, 0, j))],
        out_specs=pl.BlockSpec((m, tn), lambda j: (0, j)),
        out_shape=jax.ShapeDtypeStruct((m, n), f32),
        compiler_params=_params(("arbitrary",)),
        name="modulation",
    )(cc, w_all, b_all.reshape(depth, 1, n))


def _norm_mod(x, gain, shift, scale):
    xn = x * _row_rsqrt(x, x.shape[1]) * gain
    return (xn * (1.0 + scale) + shift).astype(bf16)


def _swiglu_half_step(x, mod_ref, g_ref, wig_ref, wiu_ref, wo_ref, mi):
    xb = _norm_mod(x, g_ref[...], mod_ref[mi:mi + 1, :], mod_ref[mi + 1:mi + 2, :])
    acc = jnp.zeros(x.shape, f32)
    for c in range(wig_ref.shape[1] // FF_CHUNK):
        sl = slice(c * FF_CHUNK, (c + 1) * FF_CHUNK)
        g = _dot(xb, wig_ref[:, sl])
        u = _dot(xb, wiu_ref[:, sl])
        acc = acc + _dot((_silu(g) * u).astype(bf16), wo_ref[sl, :])
    return x + (0.5 * mod_ref[mi + 2:mi + 3, :]) * acc


def _ffn_body(x_ref, mod_ref, g_ref, wig_ref, wiu_ref, wo_ref, o_ref, *, mi):
    o_ref[...] = _swiglu_half_step(x_ref[...], mod_ref, g_ref, wig_ref, wiu_ref, wo_ref, mi)


def _mix_ffn_body(x_ref, mod_ref, a_ref, m_ref, w_ref, s_ref, wout_ref, g_ref, wig_ref, wiu_ref, wo_ref, o_ref):
    acc = _dot(a_ref[...], wout_ref[0:GROUP_W, :])
    acc = acc + _dot(m_ref[...], wout_ref[GROUP_W:2 * GROUP_W, :])
    acc = acc + _dot(w_ref[...], wout_ref[2 * GROUP_W:3 * GROUP_W, :])
    acc = acc + _dot(s_ref[...], wout_ref[3 * GROUP_W:4 * GROUP_W, :])
    x = x_ref[...] + mod_ref[5:6, :] * acc
    o_ref[...] = _swiglu_half_step(x, mod_ref, g_ref, wig_ref, wiu_ref, wo_ref, 6)


def _mod_spec(d, mod_base, rows_per_mod, tile):
    if rows_per_mod is None:
        return pl.BlockSpec((None, N_MOD, d), lambda i: (mod_base, 0, 0))
    tiles_per_mod = rows_per_mod // tile
    return pl.BlockSpec((None, N_MOD, d), lambda i: (mod_base + i // tiles_per_mod, 0, 0))


def _ffn(x, mod, gain, wig, wiu, wo, mi, mod_base, rows_per_mod):
    rows, d = x.shape
    dff = wig.shape[1]
    return pl.pallas_call(
        functools.partial(_ffn_body, mi=mi),
        grid=(rows // WIDE_ROW_TILE,),
        in_specs=[pl.BlockSpec((WIDE_ROW_TILE, d), lambda i: (i, 0)),
                  _mod_spec(d, mod_base, rows_per_mod, WIDE_ROW_TILE),
                  _resident((1, d)), _resident((d, dff)), _resident((d, dff)), _resident((dff, d))],
        out_specs=pl.BlockSpec((WIDE_ROW_TILE, d), lambda i: (i, 0)),
        out_shape=jax.ShapeDtypeStruct((rows, d), f32),
        compiler_params=_params(("parallel",)),
        name="ffn",
    )(x, mod, gain, wig, wiu, wo)


def _inproj_body(x_ref, mod_ref, g_ref, w_ref, wt_ref, qkvo_ref, mla_ref, swa_ref, z_ref, xbc_ref, gt_ref):
    xb = _norm_mod(x_ref[...], g_ref[...], mod_ref[3:4, :], mod_ref[4:5, :])
    qkvo_ref[...] = _dot(xb, w_ref[:, P_QKVO:P_MLA]).astype(bf16)
    mla_ref[...] = _dot(xb, w_ref[:, P_MLA:P_SWA]).astype(bf16)
    swa_ref[...] = _dot(xb, w_ref[:, P_SWA:P_Z]).astype(bf16)
    z_ref[...] = _dot(xb, w_ref[:, P_Z:P_XBC]).astype(bf16)
    xbc_ref[...] = _dot(xb, w_ref[:, P_XBC:P_COLS]).astype(bf16)
    gt_ref[...] = _dot_nt(wt_ref[...], xb)


def _inproj(x, mod, gain, w, wt, mod_base, rows_per_mod):
    rows, d = x.shape
    widths = [P_MLA - P_QKVO, P_SWA - P_MLA, P_Z - P_SWA, P_XBC - P_Z, P_COLS - P_XBC]
    return pl.pallas_call(
        _inproj_body,
        grid=(rows // WIDE_ROW_TILE,),
        in_specs=[pl.BlockSpec((WIDE_ROW_TILE, d), lambda i: (i, 0)),
                  _mod_spec(d, mod_base, rows_per_mod, WIDE_ROW_TILE),
                  _resident((1, d)), _resident((d, P_COLS)), _resident((GT_ROWS, d))],
        out_specs=[pl.BlockSpec((WIDE_ROW_TILE, n), lambda i: (i, 0)) for n in widths]
                  + [pl.BlockSpec((GT_ROWS, WIDE_ROW_TILE), lambda i: (0, i))],
        out_shape=[jax.ShapeDtypeStruct((rows, n), bf16) for n in widths]
                  + [jax.ShapeDtypeStruct((GT_ROWS, rows), f32)],
        compiler_params=_params(("parallel",)),
        name="inproj",
    )(x, mod, gain, w, wt)


def _mix_ffn(x, mod, mixed, w_out, gain, wig, wiu, wo, mod_base, rows_per_mod):
    rows, d = x.shape
    dff = wig.shape[1]
    return pl.pallas_call(
        _mix_ffn_body,
        grid=(rows // ROW_TILE,),
        in_specs=[pl.BlockSpec((ROW_TILE, d), lambda i: (i, 0)),
                  _mod_spec(d, mod_base, rows_per_mod, ROW_TILE)]
                 + [pl.BlockSpec((ROW_TILE, GROUP_W), lambda i: (i, 0))] * 4
                 + [_resident(w_out.shape), _resident((1, d)), _resident((d, dff)), _resident((d, dff)),
                    _resident((dff, d))],
        out_specs=pl.BlockSpec((ROW_TILE, d), lambda i: (i, 0)),
        out_shape=jax.ShapeDtypeStruct((rows, d), f32),
        compiler_params=_params(("parallel",)),
        name="mix_ffn",
    )(x, mod, *mixed, w_out, gain, wig, wiu, wo)


def _tri_masks(n):
    r = lax.broadcasted_iota(jnp.int32, (n, n), 0)
    c = lax.broadcasted_iota(jnp.int32, (n, n), 1)
    return r >= c, r <= c


def _seg_blocks(cl, t, width):
    return (pl.BlockSpec((cl, width), lambda b: (b, 0)), pl.BlockSpec((t, width), lambda b: (b, 0)))


def _seg_blocks_t(cl, t, nrow, row0):
    blk = row0 // nrow
    return (pl.BlockSpec((nrow, cl), lambda b: (blk, b)), pl.BlockSpec((nrow, t), lambda b: (blk, b)))


def _mlstm_body(qc_ref, ql_ref, grc_ref, grl_ref, gbr_ref, on_ref, oc_ref, ol_ref,
                hsc, hsl, ktc, ktl, *, need_ctx):
    L = SCAN_CHUNK
    PW = 2 * HEAD_DIM
    lower, upper = _tri_masks(L)
    tri_lo = jnp.where(lower, 1.0, 0.0).astype(bf16)
    tri_up = jnp.where(upper, 1.0, 0.0).astype(bf16)
    lane = lax.broadcasted_iota(jnp.int32, (1, PW), 1)
    half0 = lane < HEAD_DIM
    rr = lax.broadcasted_iota(jnp.int32, (PW, PW), 0)
    cc = lax.broadcasted_iota(jnp.int32, (PW, PW), 1)
    eye = jnp.where(rr == cc, 1.0, 0.0).astype(bf16)
    zero_b = jnp.zeros((), bf16)
    lane_head = lax.broadcasted_iota(jnp.int32, (1, GROUP_W), 1) // HEAD_DIM
    row_head = lax.broadcasted_iota(jnp.int32, (GROUP_W, 1), 0) // HEAD_DIM
    rr4 = lax.broadcasted_iota(jnp.int32, (GROUP_W, 2 * GROUP_W), 0) // HEAD_DIM
    cc4 = (lax.broadcasted_iota(jnp.int32, (GROUP_W, 2 * GROUP_W), 1) % GROUP_W) // HEAD_DIM
    blockdiag4 = rr4 == cc4
    ones_head = [jnp.broadcast_to(jnp.where(lane_head == h, 1.0, 0.0), (L, GROUP_W)).astype(bf16)
                 for h in range(HEADS)]
    ones_all = jnp.ones((L, GROUP_W), bf16)

    def transpose_keys(q_ref, kt_ref):
        for r in range(0, q_ref.shape[0], L):
            for p in range(HEADS // 2):
                k = q_ref[r:r + L, GROUP_W + p * PW:GROUP_W + (p + 1) * PW] * (HEAD_DIM ** -0.5)
                kt_ref[p * PW:(p + 1) * PW, r:r + L] = _dot_nt(eye, k).astype(bf16)

    transpose_keys(qc_ref, ktc)
    transpose_keys(ql_ref, ktl)

    def direction(q_ref, kt_ref, gr_ref, j, rev, carry, off):
        r0 = pl.multiple_of(off + j * L, L)
        base = 2 * HEADS if rev else 0
        mask = upper if rev else lower
        grow = gr_ref[:, pl.ds(r0, L)] + gbr_ref[...]
        lfr = _log_sigmoid(grow)
        brows = _dot_tri(lfr, tri_lo if rev else tri_up)
        bcols = jnp.transpose(brows)
        r_all = grow[base:base + HEADS, :] - brows[base + HEADS:base + 2 * HEADS, :]
        wide = lambda x: jnp.broadcast_to(x, (HEADS, L))
        b_end = wide(jnp.sum(lfr[base + HEADS:base + 2 * HEADS, :], axis=1, keepdims=True))
        cn, m_all = carry
        mm = jnp.maximum(m_all, wide(jnp.max(r_all, axis=1, keepdims=True)))
        a_all = jnp.exp(m_all - mm)
        w_all = jnp.exp(r_all - mm)
        twice = lambda x: jnp.concatenate([x, x], axis=1)
        q_all = q_ref[pl.ds(r0, L), 0:GROUP_W]
        k_all = q_ref[pl.ds(r0, L), GROUP_W:2 * GROUP_W] * (HEAD_DIM ** -0.5)
        v_all = q_ref[pl.ds(r0, L), 2 * GROUP_W:3 * GROUP_W]
        kt_all = kt_ref[:, pl.ds(r0, L)]
        qk = _dot_nt(jnp.concatenate([jnp.where(lane_head == h, q_all, zero_b) for h in range(HEADS)], axis=0), k_all)
        sb, wi, em = [], [], []
        for h in range(HEADS):
            rm = jnp.where(mask, r_all[h:h + 1, :], NEG)
            m_h = jnp.broadcast_to(m_all[h:h + 1, :], (L, L))
            c = jnp.maximum(m_h, jnp.broadcast_to(jnp.max(rm, axis=1, keepdims=True), (L, L)))
            sb.append((qk[h * L:(h + 1) * L] * jnp.exp(rm - c)).astype(bf16))
            wi.append(jnp.exp(m_h - c))
            bcol = bcols[:, base + HEADS + h:base + HEADS + h + 1]
            em.append(jnp.exp(-(jnp.broadcast_to(bcol, (L, L)) + c)))
        per_head = lambda xs: jnp.concatenate([jnp.where(half0, xs[0], xs[1]), jnp.where(half0, xs[2], xs[3])], axis=1)
        vo = jnp.concatenate([jnp.concatenate([jnp.where(lane_head == h, v_all, zero_b),
                                               ones_head[h]], axis=1)
                              for h in range(HEADS)], axis=0)
        hx = twice(per_head(wi)) * _dot(q_all, cn.astype(bf16)) + _dot(jnp.concatenate(sb, axis=1), vo)
        den = jnp.maximum(jnp.abs(hx[:, GROUP_W:2 * GROUP_W]), per_head(em))
        out = hx[:, 0:GROUP_W] / den
        wsel = jnp.where(row_head == 0, w_all[0:1, :], jnp.where(row_head == 1, w_all[1:2, :],
                         jnp.where(row_head == 2, w_all[2:3, :], w_all[3:4, :])))
        asel = jnp.where(row_head == 0, a_all[0:1, :], jnp.where(row_head == 1, a_all[1:2, :],
                         jnp.where(row_head == 2, a_all[2:3, :], a_all[3:4, :])))
        ktw = (kt_all.astype(f32) * wsel).astype(bf16)
        upd = _dot(ktw, jnp.concatenate([v_all, ones_all], axis=1))
        asel4 = jnp.concatenate([asel] * (2 * GROUP_W // L), axis=1)
        return (asel4 * cn + jnp.where(blockdiag4, upd, 0.0), b_end + mm), out

    def segment(q_ref, kt_ref, gr_ref, hs_ref, carry):
        seg = q_ref.shape[0] // SCAN_BATCH
        n = seg // L

        def body(i, carry):
            out, pieces = [], []
            for g, (fw, bw) in enumerate(carry):
                fw, h_f = direction(q_ref, kt_ref, gr_ref, i, False, fw, g * seg)
                bw, h_b = direction(q_ref, kt_ref, gr_ref, n - 1 - i, True, bw, g * seg)
                out.append((fw, bw))
                pieces += [h_f, h_b]
            hs_ref[i] = jnp.concatenate(pieces, axis=1)
            return tuple(out)

        return lax.fori_loop(0, n, body, carry, unroll=SCAN_UNROLL)

    def init():
        return jnp.zeros((GROUP_W, 2 * GROUP_W), f32), jnp.zeros((HEADS, L), f32)

    carry = segment(qc_ref, ktc, grc_ref, hsc, tuple((init(), init()) for _ in range(SCAN_BATCH)))
    segment(ql_ref, ktl, grl_ref, hsl, carry)

    gmat = _group_mean_matrix(GROUP_W, HEAD_DIM)

    def finish(q_ref, hs_ref, o_ref):
        seg = o_ref.shape[0] // SCAN_BATCH
        n = seg // L
        for g in range(SCAN_BATCH):
            for j in range(n):
                r = g * seg + j * L
                c0 = g * 2 * GROUP_W
                hh = hs_ref[j, :, c0:c0 + GROUP_W] + hs_ref[n - 1 - j, :, c0 + GROUP_W:c0 + 2 * GROUP_W]
                hn = hh * _group_rsqrt(hh, gmat) * on_ref[...]
                og = q_ref[r:r + L, 3 * GROUP_W:4 * GROUP_W].astype(f32)
                o_ref[r:r + L, :] = (_sigmoid(og) * hn).astype(o_ref.dtype)

    finish(ql_ref, hsl, ol_ref)
    if need_ctx:
        finish(qc_ref, hsc, oc_ref)
    else:
        oc_ref[...] = jnp.zeros(oc_ref.shape, oc_ref.dtype)


def _mlstm(qkvo_c, qkvo_l, gt_c, gt_l, gate_b, out_norm, batch, need_ctx):
    cl, t = qkvo_c.shape[0] // batch, qkvo_l.shape[0] // batch
    ng = 4 * HEADS
    bc, bt = SCAN_BATCH * cl, SCAN_BATCH * t
    gb_row = gate_b.reshape(ng, 1)
    qc_spec, ql_spec = _seg_blocks(bc, bt, 4 * GROUP_W)
    grc_spec, grl_spec = _seg_blocks_t(bc, bt, ng, GT_MG)
    oc_spec, ol_spec = _seg_blocks(bc, bt, GROUP_W)
    return pl.pallas_call(
        functools.partial(_mlstm_body, need_ctx=need_ctx),
        grid=(batch // SCAN_BATCH,),
        in_specs=[qc_spec, ql_spec, grc_spec, grl_spec, _resident((ng, 1)), _resident((1, GROUP_W))],
        out_specs=[oc_spec, ol_spec],
        out_shape=[jax.ShapeDtypeStruct((batch * cl, GROUP_W), bf16),
                   jax.ShapeDtypeStruct((batch * t, GROUP_W), bf16)],
        scratch_shapes=[pltpu.VMEM((cl // SCAN_CHUNK, SCAN_CHUNK, 2 * SCAN_BATCH * GROUP_W), f32),
                        pltpu.VMEM((t // SCAN_CHUNK, SCAN_CHUNK, 2 * SCAN_BATCH * GROUP_W), f32),
                        pltpu.VMEM((GROUP_W, bc), bf16), pltpu.VMEM((GROUP_W, bt), bf16)],
        compiler_params=_params(("parallel",)),
        name="mlstm",
    )(qkvo_c, qkvo_l, gt_c, gt_l, gb_row, out_norm.reshape(1, GROUP_W))


def _ssd_body(zc_ref, zl_ref, xc_ref, xl_ref, drc_ref, drl_ref, cw_ref, cb_ref, dbr_ref,
              alr_ref, dsk_ref, ng_ref, oc_ref, ol_ref,
              xac, xal, ysc, ysl, xpad, btc, btl, *, need_ctx):
    L = SCAN_CHUNK
    N = SSD_STATE
    lower, upper = _tri_masks(L)
    tri_lo = jnp.where(lower, 1.0, 0.0).astype(bf16)
    tri_up = jnp.where(upper, 1.0, 0.0).astype(bf16)
    a_row = -jnp.exp(alr_ref[...])
    pad = 8
    half = SSD_CONV // 2

    def conv_act(x_ref, xa_ref):
        n = x_ref.shape[0] // SCAN_BATCH
        zeros = jnp.zeros((pad, SSD_XBC), f32)
        blk = 256
        for g in range(SCAN_BATCH):
            xpad[0:pad, :] = zeros
            xpad[pad + n:2 * pad + n, :] = zeros
            for r in range(0, n, blk):
                xpad[pad + r:pad + r + blk, :] = x_ref[g * n + r:g * n + r + blk, :].astype(f32)
            for r in range(0, n, blk):
                slab = xpad[r:r + blk + 2 * pad, :]
                y = jnp.zeros((blk, SSD_XBC), f32) + cb_ref[...]
                for kk in range(SSD_CONV):
                    s = kk - half
                    shifted = slab if s == 0 else pltpu.roll(slab, (-s) % slab.shape[0], 0)
                    y = y + cw_ref[kk:kk + 1, :] * shifted[pad:pad + blk, :]
                xa_ref[g * n + r:g * n + r + blk, :] = _silu(y)

    conv_act(xc_ref, xac)

    PW = 2 * HEAD_DIM
    lane = lax.broadcasted_iota(jnp.int32, (1, PW), 1)
    half0 = lane < HEAD_DIM
    row_half0 = lax.broadcasted_iota(jnp.int32, (PW, 1), 0) < N
    rr = lax.broadcasted_iota(jnp.int32, (PW, PW), 0)
    cc = lax.broadcasted_iota(jnp.int32, (PW, PW), 1)
    eye = jnp.where(rr == cc, 1.0, 0.0).astype(bf16)
    zero_b = jnp.zeros((), bf16)

    def transpose_b(xa_ref, bt_ref):
        for r in range(0, xa_ref.shape[0], L):
            bm = xa_ref[r:r + L, GROUP_W:GROUP_W + PW].astype(bf16)
            bt_ref[:, r:r + L] = _dot_nt(eye, bm).astype(bf16)

    transpose_b(xac, btc)

    def direction(xa_ref, bt_ref, dr_ref, j, rev, states, off):
        r0 = pl.multiple_of(off + j * L, L)
        base = HEADS if rev else 0
        mask = upper if rev else lower
        dtr = _softplus(dr_ref[:, pl.ds(r0, L)] + dbr_ref[...])
        ar = dtr * a_row
        cumr = _dot_tri(ar, tri_lo if rev else tri_up)
        cumc = jnp.transpose(cumr)
        cum_end = jnp.broadcast_to(jnp.sum(ar[base:base + HEADS, :], axis=1, keepdims=True), (HEADS, L))
        dec_all = jnp.exp(cum_end - cumr[base:base + HEADS, :]) * dtr[base:base + HEADS, :]
        aexp_all = jnp.exp(cum_end)
        bp = xa_ref[pl.ds(r0, L), GROUP_W:GROUP_W + PW].astype(bf16)
        cp = xa_ref[pl.ds(r0, L), GROUP_W + PW:GROUP_W + 2 * PW].astype(bf16)
        btp = bt_ref[:, pl.ds(r0, L)].astype(f32)
        new_states, outs = [], []
        for g in range(SSD_GROUPS):
            G = _dot_nt(jnp.where(half0 if g == 0 else ~half0, cp, zero_b), bp)
            xpb = xa_ref[pl.ds(r0, L), g * PW:(g + 1) * PW].astype(bf16)
            uu = _dot(jnp.concatenate([btp * dec_all[2 * g:2 * g + 1, :], btp * dec_all[2 * g + 1:2 * g + 2, :]],
                                      axis=0).astype(bf16), xpb)
            ws, ecs, us = [], [], []
            for e in range(2):
                h = 2 * g + e
                idx = base + h
                cum_cb = jnp.broadcast_to(cumc[:, idx:idx + 1], (L, L))
                seg = jnp.exp(jnp.where(mask, cum_cb - cumr[idx:idx + 1, :], NEG))
                ws.append((G * seg * dtr[idx:idx + 1, :]).astype(bf16))
                ecs.append(jnp.exp(cum_cb))
                us.append(aexp_all[h:h + 1, :] * states[g] + uu[e * PW:(e + 1) * PW])
            xbd = jnp.concatenate([jnp.where(half0, xpb, zero_b), jnp.where(half0, zero_b, xpb)], axis=0)
            y = _dot(jnp.concatenate(ws, axis=1), xbd) + jnp.where(half0, ecs[0], ecs[1]) * _dot(cp, states[g].astype(bf16))
            outs.append(y)
            new_states.append(jnp.where(row_half0 if g == 0 else ~row_half0, jnp.where(half0, us[0], us[1]), 0.0))
        return new_states, jnp.concatenate(outs, axis=1)

    def segment(xa_ref, bt_ref, dr_ref, ys_ref, carry):
        seg = xa_ref.shape[0] // SCAN_BATCH
        n = seg // L

        def body(i, carry):
            out, pieces = [], []
            for g, (fw, bw) in enumerate(carry):
                fw, y_f = direction(xa_ref, bt_ref, dr_ref, i, False, fw, g * seg)
                bw, y_b = direction(xa_ref, bt_ref, dr_ref, n - 1 - i, True, bw, g * seg)
                out.append((fw, bw))
                pieces += [y_f, y_b]
            ys_ref[i] = jnp.concatenate(pieces, axis=1)
            return tuple(out)

        return lax.fori_loop(0, n, body, carry, unroll=SCAN_UNROLL)

    init = [jnp.zeros((PW, PW), f32) for _ in range(SSD_GROUPS)]
    carry = segment(xac, btc, drc_ref, ysc, tuple((init, list(init)) for _ in range(SCAN_BATCH)))
    conv_act(xl_ref, xal)
    transpose_b(xal, btl)
    segment(xal, btl, drl_ref, ysl, carry)

    gmat = _group_mean_matrix(GROUP_W, SSD_NORM_GROUP)

    def finish(z_ref, xa_ref, ys_ref, o_ref):
        seg = o_ref.shape[0] // SCAN_BATCH
        n = seg // L
        for g in range(SCAN_BATCH):
            for j in range(n):
                r = g * seg + j * L
                c0 = g * 2 * GROUP_W
                yy = ys_ref[j, :, c0:c0 + GROUP_W] + ys_ref[n - 1 - j, :, c0 + GROUP_W:c0 + 2 * GROUP_W] \
                    + dsk_ref[...] * xa_ref[r:r + L, 0:GROUP_W]
                gt = yy * _silu(z_ref[r:r + L, :].astype(f32))
                o_ref[r:r + L, :] = (gt * _group_rsqrt(gt, gmat) * ng_ref[...]).astype(o_ref.dtype)

    finish(zl_ref, xal, ysl, ol_ref)
    if need_ctx:
        finish(zc_ref, xac, ysc, oc_ref)
    else:
        oc_ref[...] = jnp.zeros(oc_ref.shape, oc_ref.dtype)


def _ssd(z_c, z_l, xbc_c, xbc_l, gt_c, gt_l, conv_w, conv_b, dt_bias, a_log, d_skip, norm_g, batch, need_ctx):
    cl, t = z_c.shape[0] // batch, z_l.shape[0] // batch
    nd = 2 * HEADS
    bc, bt = SCAN_BATCH * cl, SCAN_BATCH * t
    cw = jnp.zeros((8, SSD_XBC), f32).at[:SSD_CONV].set(conv_w)
    zc_spec, zl_spec = _seg_blocks(bc, bt, GROUP_W)
    xc_spec, xl_spec = _seg_blocks(bc, bt, SSD_XBC)
    drc_spec, drl_spec = _seg_blocks_t(bc, bt, nd, GT_DT)
    oc_spec, ol_spec = _seg_blocks(bc, bt, GROUP_W)
    return pl.pallas_call(
        functools.partial(_ssd_body, need_ctx=need_ctx),
        grid=(batch // SCAN_BATCH,),
        in_specs=[zc_spec, zl_spec, xc_spec, xl_spec, drc_spec, drl_spec,
                  _resident((8, SSD_XBC)), _resident((1, SSD_XBC)), _resident((nd, 1)), _resident((nd, 1)),
                  _resident((1, GROUP_W)), _resident((1, GROUP_W))],
        out_specs=[oc_spec, ol_spec],
        out_shape=[jax.ShapeDtypeStruct((batch * cl, GROUP_W), bf16),
                   jax.ShapeDtypeStruct((batch * t, GROUP_W), bf16)],
        scratch_shapes=[pltpu.VMEM((bc, SSD_XBC), f32), pltpu.VMEM((bt, SSD_XBC), f32),
                        pltpu.VMEM((cl // SCAN_CHUNK, SCAN_CHUNK, 2 * SCAN_BATCH * GROUP_W), f32),
                        pltpu.VMEM((t // SCAN_CHUNK, SCAN_CHUNK, 2 * SCAN_BATCH * GROUP_W), f32),
                        pltpu.VMEM((t + 16, SSD_XBC), f32),
                        pltpu.VMEM((2 * SSD_STATE, bc), bf16), pltpu.VMEM((2 * SSD_STATE, bt), bf16)],
        compiler_params=_params(("parallel",)),
        name="ssd",
    )(z_c, z_l, xbc_c, xbc_l, gt_c, gt_l, cw, conv_b.reshape(1, SSD_XBC),
      dt_bias.reshape(nd, 1), a_log.reshape(nd, 1),
      jnp.repeat(d_skip, HEAD_DIM).reshape(1, GROUP_W), norm_g.reshape(1, GROUP_W))


def _mla_body(ac_ref, al_ref, qn_ref, kvn_ref, wq_ref, wkv_ref, qg_ref, kg_ref, krg_ref, cos_ref, sin_ref,
              oc_ref, ol_ref, q_s, k_s, vt_s, *, need_ctx):
    cl, t = ac_ref.shape[0], al_ref.shape[0]
    scale = MLA_QK ** -0.5 * LOG2E
    rr = lax.broadcasted_iota(jnp.int32, (LANES, LANES), 0)
    cc = lax.broadcasted_iota(jnp.int32, (LANES, LANES), 1)
    in_nope = (rr < MLA_NOPE) & (cc < MLA_NOPE)
    in_rope = (rr >= MLA_NOPE) & (rr < MLA_QK) & (cc >= MLA_NOPE) & (cc < MLA_QK)
    head_gmat = jnp.where(in_nope, 1.0 / MLA_NOPE, jnp.where(in_rope, 1.0 / MLA_ROPE, 0.0)).astype(bf16)
    rot = _rotate_half_matrix(MLA_ROPE // 4)
    er = lax.broadcasted_iota(jnp.int32, (GROUP_W, GROUP_W), 0)
    ec = lax.broadcasted_iota(jnp.int32, (GROUP_W, GROUP_W), 1)
    eye = jnp.where(er == ec, 1.0, 0.0).astype(bf16)

    def head_norm(x):
        return x * _group_rsqrt(x, head_gmat)

    def project(a_ref, row0, n, roped):
        blk = 256
        for r in range(0, n, blk):
            a = a_ref[r:r + blk, :].astype(f32)
            aq, akv, akr = a[:, 0:GROUP_W], a[:, GROUP_W:GROUP_W + LANES], a[:, GROUP_W + LANES:GROUP_W + 2 * LANES]
            qh = _dot((aq * _row_rsqrt(aq, GROUP_W) * qn_ref[...]).astype(bf16), wq_ref[...])
            kv = _dot((akv * _row_rsqrt(akv, kvn_ref.shape[1]) * kvn_ref[...]).astype(bf16), wkv_ref[...])
            kr = akr * _row_rsqrt(akr, MLA_ROPE) * krg_ref[...]
            if roped:
                cos, sin = cos_ref[r:r + blk, :], sin_ref[r:r + blk, :]
                kr = _rope(kr, cos, sin, rot)
            for h in range(HEADS):
                qx = head_norm(qh[:, h * LANES:(h + 1) * LANES]) * qg_ref[...]
                if roped:
                    qx = _rope(qx, cos, sin, rot)
                q_s[h, row0 + r:row0 + r + blk, :] = (qx * scale).astype(bf16)
                kx = kv[:, h * LANES:(h + 1) * LANES]
                kx = kx * _row_rsqrt(kx, MLA_NOPE) * kg_ref[...]
                k_s[h, row0 + r:row0 + r + blk, :] = (kx + kr).astype(bf16)
            vt_s[:, row0 + r:row0 + r + blk] = _dot_nt(eye, kv[:, HEADS * LANES:].astype(bf16)).astype(bf16)

    project(ac_ref, 0, cl, False)
    project(al_ref, cl, t, True)

    def attend(q0, nq, nk, o_ref, o0):
        outs = []
        sts = [_dot_nt(k_s[h, 0:nk, :], q_s[h, pl.ds(q0, nq), :]) for h in range(HEADS)]
        for h in range(HEADS):
            st = sts[h]
            p = jnp.exp2(st - jnp.max(st, axis=0, keepdims=True))
            den = jnp.sum(p, axis=0, keepdims=True)
            outs.append(_dot(vt_s[h * HEAD_DIM:(h + 1) * HEAD_DIM, 0:nk], p.astype(bf16)) / den)
        ot = jnp.concatenate(outs, axis=0).astype(bf16)
        for c in range(0, nq, GROUP_W):
            o_ref[pl.ds(o0 + c, GROUP_W), :] = _dot_nt(eye, ot[:, c:c + GROUP_W]).astype(o_ref.dtype)

    def body(i, _):
        o0 = pl.multiple_of(i * MLA_QBLK, MLA_QBLK)
        attend(cl + o0, MLA_QBLK, cl + t, ol_ref, o0)
        return 0

    lax.fori_loop(0, t // MLA_QBLK, body, 0)
    if need_ctx:
        attend(0, cl, cl, oc_ref, 0)
    else:
        oc_ref[...] = jnp.zeros(oc_ref.shape, oc_ref.dtype)


def _mla(a_c, a_l, q_norm, kv_norm, wq_b, wkv_b, q_gain, k_gain, cos, sin, batch, need_ctx):
    cl, t = a_c.shape[0] // batch, a_l.shape[0] // batch
    kvl = kv_norm.shape[0]
    wq = jnp.zeros((GROUP_W, HEADS, LANES), f32).at[:, :, :MLA_QK].set(wq_b.reshape(GROUP_W, HEADS, MLA_QK))
    wkv4 = wkv_b.reshape(kvl, HEADS, MLA_NOPE + HEAD_DIM)
    wk = jnp.zeros((kvl, HEADS, LANES), f32).at[:, :, :MLA_NOPE].set(wkv4[:, :, :MLA_NOPE])
    wkv = jnp.concatenate([wk.reshape(kvl, HEADS * LANES), wkv4[:, :, MLA_NOPE:].reshape(kvl, GROUP_W)], axis=1)
    slab = lambda v, off: jnp.zeros((1, LANES), f32).at[0, off:off + v.shape[0]].set(v)
    qg = slab(q_gain, 0)
    kg = slab(k_gain[:MLA_NOPE], 0)
    krg = slab(k_gain[MLA_NOPE:], MLA_NOPE)
    ac_spec, al_spec = _seg_blocks(cl, t, 2 * GROUP_W)
    oc_spec, ol_spec = _seg_blocks(cl, t, GROUP_W)
    return pl.pallas_call(
        functools.partial(_mla_body, need_ctx=need_ctx),
        grid=(batch,),
        in_specs=[ac_spec, al_spec, _resident((1, GROUP_W)), _resident((1, LANES)),
                  _resident((GROUP_W, HEADS * LANES)), _resident((kvl, HEADS * LANES + GROUP_W)),
                  _resident((1, LANES)), _resident((1, LANES)), _resident((1, LANES)),
                  _resident((t, LANES)), _resident((t, LANES))],
        out_specs=[oc_spec, ol_spec],
        out_shape=[jax.ShapeDtypeStruct((batch * cl, GROUP_W), bf16),
                   jax.ShapeDtypeStruct((batch * t, GROUP_W), bf16)],
        scratch_shapes=[pltpu.VMEM((HEADS, cl + t, LANES), bf16), pltpu.VMEM((HEADS, cl + t, LANES), bf16),
                        pltpu.VMEM((GROUP_W, cl + t), bf16)],
        compiler_params=_params(("parallel",)),
        name="mla",
    )(a_c, a_l, q_norm.reshape(1, GROUP_W), kv_norm.reshape(1, kvl),
      wq.reshape(GROUP_W, HEADS * LANES).astype(bf16), wkv.astype(bf16), qg, kg, krg, cos, sin)


def _swa_body(wc_ref, wl_ref, qg_ref, kg_ref, sink_ref, cos_ref, sin_ref, oc_ref, ol_ref,
              q_s, qc_s, k_s, v_s, kc_s, vc_s, *, need_ctx):
    cl, t = wc_ref.shape[0], wl_ref.shape[0]
    scale = HEAD_DIM ** -0.5 * LOG2E
    kvw = SWA_KV_HEADS * HEAD_DIM
    blk = SWA_BLK
    gmat_q = _group_mean_matrix(GROUP_W, HEAD_DIM)
    gmat_k = _group_mean_matrix(kvw, HEAD_DIM)
    rot = _rotate_half_matrix(HEAD_DIM // 4)

    def project(w_ref, n, roped, qdst, kdst, vdst, row0):
        step = 256
        for r in range(0, n, step):
            w = w_ref[r:r + step, :].astype(f32)
            q, k = w[:, 0:GROUP_W], w[:, GROUP_W:GROUP_W + kvw]
            q = q * _group_rsqrt(q, gmat_q) * qg_ref[...]
            k = k * _group_rsqrt(k, gmat_k) * kg_ref[...]
            if roped:
                cos, sin = cos_ref[r:r + step, :], sin_ref[r:r + step, :]
                q = jnp.concatenate([_rope(q[:, 0:LANES], cos, sin, rot),
                                     _rope(q[:, LANES:2 * LANES], cos, sin, rot)], axis=1)
                k = _rope(k, cos, sin, rot)
            qdst[r:r + step, :] = (q * scale).astype(bf16)
            kdst[row0 + r:row0 + r + step, :] = k.astype(bf16)
            vdst[row0 + r:row0 + r + step, :] = w_ref[r:r + step, GROUP_W + kvw:GROUP_W + 2 * kvw]

    zeros = jnp.zeros((blk, kvw), bf16)
    for s in (k_s, v_s):
        s[0:blk, :] = zeros
        s[blk + t:2 * blk + t, :] = zeros
    project(wc_ref, cl, False, qc_s, kc_s, vc_s, 0)
    project(wl_ref, t, True, q_s, k_s, v_s, blk)

    rr = lax.broadcasted_iota(jnp.int32, (HEADS * blk, 3 * blk), 0) % blk
    jj = lax.broadcasted_iota(jnp.int32, (HEADS * blk, 3 * blk), 1)
    band = (jj - rr >= 0) & (jj - rr <= 2 * SWA_WINDOW)
    half0 = lax.broadcasted_iota(jnp.int32, (1, LANES), 1) < HEAD_DIM
    zero_b = jnp.zeros((), bf16)

    def wide(x, n):
        return jnp.broadcast_to(x, (x.shape[0], n))

    def stacked_queries(q_ref, r0, n):
        qa, qb = q_ref[pl.ds(r0, n), 0:LANES], q_ref[pl.ds(r0, n), LANES:2 * LANES]
        return jnp.concatenate([jnp.where(half0, qa, zero_b), jnp.where(half0, qb, zero_b),
                                jnp.where(half0, zero_b, qa), jnp.where(half0, zero_b, qb)], axis=0)

    def stacked_sink(n):
        row = lax.broadcasted_iota(jnp.int32, (HEADS * n, 1), 0)
        return LOG2E * jnp.where(row < n, sink_ref[0:1, :], jnp.where(row < 2 * n, sink_ref[1:2, :],
                                 jnp.where(row < 3 * n, sink_ref[2:3, :], sink_ref[3:4, :])))

    def unstack(o, n):
        return jnp.concatenate([jnp.where(half0, o[0:n], o[2 * n:3 * n]),
                                jnp.where(half0, o[n:2 * n], o[3 * n:4 * n])], axis=1)

    def lat_block(n, _):
        r0 = pl.multiple_of(n * blk, blk)
        kpos = jj + (n - 1) * blk
        valid = band & (kpos >= 0) & (kpos < t)
        q4 = stacked_queries(q_s, r0, blk)
        s = _dot_nt(q4, jnp.concatenate([k_s[pl.ds(r0, 3 * blk), :], kc_s[...]], axis=0))
        s_loc, s_ctx = jnp.where(valid, s[:, 0:3 * blk], NEG), s[:, 3 * blk:]
        sink = stacked_sink(blk)
        m = jnp.maximum(wide(jnp.maximum(jnp.max(s_loc, axis=1, keepdims=True),
                                         jnp.max(s_ctx, axis=1, keepdims=True)), LANES), sink)
        p_loc = jnp.exp2(s_loc - jnp.concatenate([m] * 3, axis=1))
        p_ctx = jnp.exp2(s_ctx - jnp.concatenate([m] * (cl // LANES), axis=1))
        den = wide(jnp.sum(p_loc, axis=1, keepdims=True) + jnp.sum(p_ctx, axis=1, keepdims=True), LANES) \
            + jnp.exp2(sink - m)
        o = _dot(jnp.concatenate([p_loc, p_ctx], axis=1).astype(bf16),
                 jnp.concatenate([v_s[pl.ds(r0, 3 * blk), :], vc_s[...]], axis=0))
        ol_ref[pl.ds(r0, blk), :] = unstack(o / den, blk).astype(ol_ref.dtype)
        return 0

    lax.fori_loop(0, t // blk, lat_block, 0, unroll=SWA_UNROLL)

    if need_ctx:
        half = cl // 2
        for r0 in (0, half):
            s = _dot_nt(stacked_queries(qc_s, r0, half), kc_s[...])
            sink = stacked_sink(half)
            m = jnp.maximum(wide(jnp.max(s, axis=1, keepdims=True), LANES), sink)
            p = jnp.exp2(s - jnp.concatenate([m] * (cl // LANES), axis=1))
            den = wide(jnp.sum(p, axis=1, keepdims=True), LANES) + jnp.exp2(sink - m)
            oc_ref[r0:r0 + half, :] = unstack(_dot(p.astype(bf16), vc_s[...]) / den, half).astype(oc_ref.dtype)
    else:
        oc_ref[...] = jnp.zeros(oc_ref.shape, oc_ref.dtype)


def _swa_head_order(a, axis):
    blocks = jnp.split(a, HEADS, axis=axis)
    return jnp.concatenate([blocks[0], blocks[2], blocks[1], blocks[3]], axis=axis)


def _swa(w_c, w_l, q_gain, k_gain, sink, cos, sin, batch, need_ctx):
    cl, t = w_c.shape[0] // batch, w_l.shape[0] // batch
    kvw = SWA_KV_HEADS * HEAD_DIM
    wc_spec, wl_spec = _seg_blocks(cl, t, 2 * GROUP_W)
    oc_spec, ol_spec = _seg_blocks(cl, t, GROUP_W)
    return pl.pallas_call(
        functools.partial(_swa_body, need_ctx=need_ctx),
        grid=(batch,),
        in_specs=[wc_spec, wl_spec, _resident((1, GROUP_W)), _resident((1, kvw)), _resident((HEADS, LANES)),
                  _resident((t, LANES)), _resident((t, LANES))],
        out_specs=[oc_spec, ol_spec],
        out_shape=[jax.ShapeDtypeStruct((batch * cl, GROUP_W), bf16),
                   jax.ShapeDtypeStruct((batch * t, GROUP_W), bf16)],
        scratch_shapes=[pltpu.VMEM((t, GROUP_W), bf16), pltpu.VMEM((cl, GROUP_W), bf16),
                        pltpu.VMEM((t + 2 * SWA_BLK, kvw), bf16), pltpu.VMEM((t + 2 * SWA_BLK, kvw), bf16),
                        pltpu.VMEM((cl, kvw), bf16), pltpu.VMEM((cl, kvw), bf16)],
        compiler_params=_params(("parallel",)),
        name="swa",
    )(w_c, w_l, jnp.tile(q_gain, HEADS).reshape(1, GROUP_W), jnp.tile(k_gain, SWA_KV_HEADS).reshape(1, kvw),
      jnp.broadcast_to(sink.reshape(HEADS, 1), (HEADS, LANES)), cos, sin)


def _rope_tables(t, rot_dim, lane0):
    pos = jnp.arange(t)
    row, col = (pos // GRID_W).astype(f32), (pos % GRID_W).astype(f32)
    nf = rot_dim // 4
    inv = ROPE_BASE ** (-jnp.arange(nf, dtype=f32) / nf)
    ar, ac = row[:, None] * inv, col[:, None] * inv
    ang = jnp.concatenate([ar, ar, ac, ac], axis=-1)
    return jnp.cos(ang), jnp.sin(ang)


def _pack_in_weight(w_in):
    o = 0
    seg = {}
    for name, n in (("qkvo", 4 * GROUP_W), ("mg", 4 * HEADS), ("a_q", GROUP_W), ("a_kv", GROUP_W // 2),
                    ("a_kr", MLA_ROPE), ("swa", 2 * GROUP_W), ("z", GROUP_W), ("xbc", SSD_XBC), ("dt", 2 * HEADS)):
        seg[name] = w_in[:, o:o + n]
        o += n
    d = w_in.shape[0]
    zeros = lambda n: jnp.zeros((d, n), w_in.dtype)
    packed = jnp.concatenate([
        seg["qkvo"],
        seg["a_q"], seg["a_kv"], zeros(MLA_NOPE), seg["a_kr"], zeros(LANES - MLA_QK),
        _swa_head_order(seg["swa"][:, :GROUP_W], 1), seg["swa"][:, GROUP_W:],
        seg["z"], seg["xbc"]], axis=1)
    gate_t = jnp.concatenate([seg["mg"], seg["dt"], zeros(GT_ROWS - 4 * HEADS - 2 * HEADS)], axis=1).T
    return packed.astype(bf16), gate_t.astype(bf16)


def kernel(x, c, ctx, c_ctx, w_mod, b_mod, ffn1_norm, ffn1_wi, ffn1_wo, mix_norm, w_in, w_out, mlstm_gate_b, mlstm_out_norm, mla_q_norm, mla_kv_norm, mla_wq_b, mla_wkv_b, mla_q_gain, mla_k_gain, swa_q_gain, swa_k_gain, swa_sink, ssd_conv_w, ssd_conv_b, ssd_dt_bias, ssd_a_log, ssd_d, ssd_norm, ffn2_norm, ffn2_wi, ffn2_wo):
    b, t, d = x.shape
    cl = ctx.shape[1]
    depth = w_mod.shape[0]
    dff = ffn1_wo.shape[1]

    cos_m, sin_m = _rope_tables(t, MLA_ROPE, MLA_NOPE)
    pad_id = lambda tab, fill: jnp.concatenate(
        [jnp.full((t, MLA_NOPE), fill, f32), tab, jnp.full((t, LANES - MLA_QK), fill, f32)], axis=1)
    cos_m, sin_m = pad_id(cos_m, 1.0), pad_id(sin_m, 0.0)
    cos_s, sin_s = _rope_tables(t, HEAD_DIM, 0)
    cos_s, sin_s = jnp.tile(cos_s, (1, LANES // HEAD_DIM)), jnp.tile(sin_s, (1, LANES // HEAD_DIM))

    h = x.reshape(b * t, d)
    hc = ctx.reshape(b * cl, d)
    cc = jnp.concatenate([c, c_ctx[None, :]], axis=0)
    for l in range(depth):
        need_ctx = l < depth - 1
        mod = _modulation(cc, w_mod, b_mod, l).reshape(b + 1, N_MOD, d)
        streams = ((0, t), (b, None))

        ffn1_w = (ffn1_norm[l].reshape(1, d), ffn1_wi[l][:, :dff].astype(bf16), ffn1_wi[l][:, dff:].astype(bf16),
                  ffn1_wo[l].astype(bf16))
        ffn2_w = (ffn2_norm[l].reshape(1, d), ffn2_wi[l][:, :dff].astype(bf16), ffn2_wi[l][:, dff:].astype(bf16),
                  ffn2_wo[l].astype(bf16))
        h = _ffn(h, mod, *ffn1_w, 0, *streams[0])
        hc = _ffn(hc, mod, *ffn1_w, 0, *streams[1])
        w_packed, w_gate_t = _pack_in_weight(w_in[l])
        in_w = (mix_norm[l].reshape(1, d), w_packed, w_gate_t)
        qkvo_l, mla_l, swa_l, z_l, xbc_l, gt_l = _inproj(h, mod, *in_w, *streams[0])
        qkvo_c, mla_c, swa_c, z_c, xbc_c, gt_c = _inproj(hc, mod, *in_w, *streams[1])
        a_c, a_l = _mlstm(qkvo_c, qkvo_l, gt_c, gt_l, mlstm_gate_b[l], mlstm_out_norm[l], b, need_ctx)
        m_c, m_l = _mla(mla_c, mla_l, mla_q_norm[l], mla_kv_norm[l], mla_wq_b[l], mla_wkv_b[l], mla_q_gain[l],
                        mla_k_gain[l], cos_m, sin_m, b, need_ctx)
        s_c, s_l = _swa(swa_c, swa_l, swa_q_gain[l], swa_k_gain[l], swa_sink[l], cos_s, sin_s, b, need_ctx)
        d_c, d_l = _ssd(z_c, z_l, xbc_c, xbc_l, gt_c, gt_l, ssd_conv_w[l], ssd_conv_b[l], ssd_dt_bias[l],
                        ssd_a_log[l], ssd_d[l], ssd_norm[l], b, need_ctx)
        wo_b = jnp.concatenate([w_out[l][:2 * GROUP_W], _swa_head_order(w_out[l][2 * GROUP_W:3 * GROUP_W], 0),
                                w_out[l][3 * GROUP_W:]], axis=0).astype(bf16)
        h = _mix_ffn(h, mod, (a_l, m_l, s_l, d_l), wo_b, *ffn2_w, *streams[0])
        if need_ctx:
            hc = _mix_ffn(hc, mod, (a_c, m_c, s_c, d_c), wo_b, *ffn2_w, *streams[1])
    return h.reshape(b, t, d)
```

```python
import functools

import jax
import jax.numpy as jnp
from jax import lax
from jax.experimental import pallas as pl
from jax.experimental.pallas import tpu as pltpu

f32 = jnp.float32
bf16 = jnp.bfloat16

RMS_EPS = 1e-6
ROPE_BASE = 10000.0
GRID_W = 64
N_MOD = 9
HEADS = 4
HEAD_DIM = 64
GROUP_W = HEADS * HEAD_DIM
MLA_NOPE = 64
MLA_ROPE = 32
MLA_QK = MLA_NOPE + MLA_ROPE
SWA_KV_HEADS = 2
SWA_WINDOW = 128
SWA_BLK = 128
SSD_STATE = 64
SSD_GROUPS = 2
SSD_CONV = 5
SSD_XBC = GROUP_W + 2 * SSD_GROUPS * SSD_STATE
SSD_NORM_GROUP = 128

LANES = 128
ROW_TILE = 512
WIDE_ROW_TILE = 1024
FF_CHUNK = 256
SCAN_CHUNK = 128
SCAN_UNROLL = 4
SWA_UNROLL = 4
SCAN_BATCH = 2
MLA_QBLK = 512
NEG = -1e30
LOG2E = 1.4426950408889634
VMEM_LIMIT = 56 * 1024 * 1024

P_QKVO = 0
P_MLA = 1024
P_SWA = 1536
P_Z = 2048
P_XBC = 2304
P_COLS = 2816
GT_MG = 0
GT_DT = 4 * HEADS
GT_ROWS = 32


def _dot(a, b):
    return jnp.dot(a, b, preferred_element_type=f32)


def _dot_nt(a, b):
    return lax.dot_general(a, b, (((1,), (1,)), ((), ())), preferred_element_type=f32)


def _sigmoid(x):
    return 1.0 / (1.0 + jnp.exp(-x))


def _silu(x):
    return x * _sigmoid(x)


def _softplus(x):
    return jnp.maximum(x, 0.0) + jnp.log(1.0 + jnp.exp(-jnp.abs(x)))


def _log_sigmoid(x):
    return -_softplus(-x)


def _split3(x):
    hi = x.astype(bf16)
    r = x - hi.astype(f32)
    mid = r.astype(bf16)
    lo = (r - mid.astype(f32)).astype(bf16)
    return hi, mid, lo


def _dot_tri(x, tri):
    m = x.shape[0]
    terms = jnp.concatenate([p.astype(f32) for p in _split3(x)], axis=0).astype(bf16)
    y = _dot(terms, tri)
    return y[0:m] + y[m:2 * m] + y[2 * m:3 * m]


def _dot2(x, w):
    hi = x.astype(bf16)
    mid = (x - hi.astype(f32)).astype(bf16)
    return _dot(hi, w) + _dot(mid, w)


def _group_mean_matrix(n, gsz):
    r = lax.broadcasted_iota(jnp.int32, (n, n), 0) // gsz
    c = lax.broadcasted_iota(jnp.int32, (n, n), 1) // gsz
    return jnp.where(r == c, 1.0 / gsz, 0.0).astype(bf16)


def _group_rsqrt(x, gmat):
    return lax.rsqrt(_dot2(x * x, gmat) + RMS_EPS)


def _row_rsqrt(x, n_real):
    return lax.rsqrt(jnp.sum(x * x, axis=1, keepdims=True) * (1.0 / n_real) + RMS_EPS)


def _rotate_half_matrix(quarter):
    r = lax.broadcasted_iota(jnp.int32, (LANES, LANES), 0)
    c = lax.broadcasted_iota(jnp.int32, (LANES, LANES), 1)
    first = (c % (2 * quarter)) < quarter
    return jnp.where(first & (r == c + quarter), -1.0, jnp.where(~first & (r == c - quarter), 1.0, 0.0)).astype(bf16)


def _rope(x, cos, sin, rot):
    return x * cos + _dot2(x, rot) * sin


def _resident(shape):
    nd = len(shape)
    return pl.BlockSpec(shape, lambda *_: (0,) * nd, pipeline_mode=pl.Buffered(1))


def _params(sem):
    return pltpu.CompilerParams(dimension_semantics=sem, vmem_limit_bytes=VMEM_LIMIT)


def _mod_body(c_ref, w_ref, b_ref, o_ref):
    s = _silu(c_ref[...])
    o_ref[...] = jnp.dot(s, w_ref[...], preferred_element_type=f32,
                         precision=lax.Precision.HIGHEST) + b_ref[...]


def _modulation(cc, w_all, b_all, layer):
    m, d = cc.shape
    depth, _, n = w_all.shape
    tn = 1024
    return pl.pallas_call(
        _mod_body,
        grid=(n // tn,),
        in_specs=[pl.BlockSpec((m, d), lambda j: (0, 0)),
                  pl.BlockSpec((None, d, tn), lambda j: (layer, 0, j)),
                  pl.BlockSpec((None, 1, tn), lambda j: (layer, 0, j))],
        out_specs=pl.BlockSpec((m, tn), lambda j: (0, j)),
        out_shape=jax.ShapeDtypeStruct((m, n), f32),
        compiler_params=_params(("arbitrary",)),
        name="modulation",
    )(cc, w_all, b_all.reshape(depth, 1, n))


def _norm_mod(x, gain, shift, scale):
    xn = x * _row_rsqrt(x, x.shape[1]) * gain
    return (xn * (1.0 + scale) + shift).astype(bf16)


def _swiglu_half_step(x, mod_ref, g_ref, wig_ref, wiu_ref, wo_ref, mi):
    xb = _norm_mod(x, g_ref[...], mod_ref[mi:mi + 1, :], mod_ref[mi + 1:mi + 2, :])
    acc = jnp.zeros(x.shape, f32)
    for c in range(wig_ref.shape[1] // FF_CHUNK):
        sl = slice(c * FF_CHUNK, (c + 1) * FF_CHUNK)
        g = _dot(xb, wig_ref[:, sl])
        u = _dot(xb, wiu_ref[:, sl])
        acc = acc + _dot((_silu(g) * u).astype(bf16), wo_ref[sl, :])
    return x + (0.5 * mod_ref[mi + 2:mi + 3, :]) * acc


def _ffn_body(x_ref, mod_ref, g_ref, wig_ref, wiu_ref, wo_ref, o_ref, *, mi):
    o_ref[...] = _swiglu_half_step(x_ref[...], mod_ref, g_ref, wig_ref, wiu_ref, wo_ref, mi)


def _mix_ffn_body(x_ref, mod_ref, a_ref, m_ref, w_ref, s_ref, wout_ref, g_ref, wig_ref, wiu_ref, wo_ref, o_ref):
    acc = _dot(a_ref[...], wout_ref[0:GROUP_W, :])
    acc = acc + _dot(m_ref[...], wout_ref[GROUP_W:2 * GROUP_W, :])
    acc = acc + _dot(w_ref[...], wout_ref[2 * GROUP_W:3 * GROUP_W, :])
    acc = acc + _dot(s_ref[...], wout_ref[3 * GROUP_W:4 * GROUP_W, :])
    x = x_ref[...] + mod_ref[5:6, :] * acc
    o_ref[...] = _swiglu_half_step(x, mod_ref, g_ref, wig_ref, wiu_ref, wo_ref, 6)


def _mod_spec(d, mod_base, rows_per_mod, tile):
    if rows_per_mod is None:
        return pl.BlockSpec((None, N_MOD, d), lambda i: (mod_base, 0, 0))
    tiles_per_mod = rows_per_mod // tile
    return pl.BlockSpec((None, N_MOD, d), lambda i: (mod_base + i // tiles_per_mod, 0, 0))


def _ffn(x, mod, gain, wig, wiu, wo, mi, mod_base, rows_per_mod):
    rows, d = x.shape
    dff = wig.shape[1]
    return pl.pallas_call(
        functools.partial(_ffn_body, mi=mi),
        grid=(rows // WIDE_ROW_TILE,),
        in_specs=[pl.BlockSpec((WIDE_ROW_TILE, d), lambda i: (i, 0)),
                  _mod_spec(d, mod_base, rows_per_mod, WIDE_ROW_TILE),
                  _resident((1, d)), _resident((d, dff)), _resident((d, dff)), _resident((dff, d))],
        out_specs=pl.BlockSpec((WIDE_ROW_TILE, d), lambda i: (i, 0)),
        out_shape=jax.ShapeDtypeStruct((rows, d), f32),
        compiler_params=_params(("parallel",)),
        name="ffn",
    )(x, mod, gain, wig, wiu, wo)


def _inproj_body(x_ref, mod_ref, g_ref, w_ref, wt_ref, qkvo_ref, mla_ref, swa_ref, z_ref, xbc_ref, gt_ref):
    xb = _norm_mod(x_ref[...], g_ref[...], mod_ref[3:4, :], mod_ref[4:5, :])
    qkvo_ref[...] = _dot(xb, w_ref[:, P_QKVO:P_MLA]).astype(bf16)
    mla_ref[...] = _dot(xb, w_ref[:, P_MLA:P_SWA]).astype(bf16)
    swa_ref[...] = _dot(xb, w_ref[:, P_SWA:P_Z]).astype(bf16)
    z_ref[...] = _dot(xb, w_ref[:, P_Z:P_XBC]).astype(bf16)
    xbc_ref[...] = _dot(xb, w_ref[:, P_XBC:P_COLS]).astype(bf16)
    gt_ref[...] = _dot_nt(wt_ref[...], xb)


def _inproj(x, mod, gain, w, wt, mod_base, rows_per_mod):
    rows, d = x.shape
    widths = [P_MLA - P_QKVO, P_SWA - P_MLA, P_Z - P_SWA, P_XBC - P_Z, P_COLS - P_XBC]
    return pl.pallas_call(
        _inproj_body,
        grid=(rows // WIDE_ROW_TILE,),
        in_specs=[pl.BlockSpec((WIDE_ROW_TILE, d), lambda i: (i, 0)),
                  _mod_spec(d, mod_base, rows_per_mod, WIDE_ROW_TILE),
                  _resident((1, d)), _resident((d, P_COLS)), _resident((GT_ROWS, d))],
        out_specs=[pl.BlockSpec((WIDE_ROW_TILE, n), lambda i: (i, 0)) for n in widths]
                  + [pl.BlockSpec((GT_ROWS, WIDE_ROW_TILE), lambda i: (0, i))],
        out_shape=[jax.ShapeDtypeStruct((rows, n), bf16) for n in widths]
                  + [jax.ShapeDtypeStruct((GT_ROWS, rows), f32)],
        compiler_params=_params(("parallel",)),
        name="inproj",
    )(x, mod, gain, w, wt)


def _mix_ffn(x, mod, mixed, w_out, gain, wig, wiu, wo, mod_base, rows_per_mod):
    rows, d = x.shape
    dff = wig.shape[1]
    return pl.pallas_call(
        _mix_ffn_body,
        grid=(rows // ROW_TILE,),
        in_specs=[pl.BlockSpec((ROW_TILE, d), lambda i: (i, 0)),
                  _mod_spec(d, mod_base, rows_per_mod, ROW_TILE)]
                 + [pl.BlockSpec((ROW_TILE, GROUP_W), lambda i: (i, 0))] * 4
                 + [_resident(w_out.shape), _resident((1, d)), _resident((d, dff)), _resident((d, dff)),
                    _resident((dff, d))],
        out_specs=pl.BlockSpec((ROW_TILE, d), lambda i: (i, 0)),
        out_shape=jax.ShapeDtypeStruct((rows, d), f32),
        compiler_params=_params(("parallel",)),
        name="mix_ffn",
    )(x, mod, *mixed, w_out, gain, wig, wiu, wo)


def _tri_masks(n):
    r = lax.broadcasted_iota(jnp.int32, (n, n), 0)
    c = lax.broadcasted_iota(jnp.int32, (n, n), 1)
    return r >= c, r <= c


def _seg_blocks(cl, t, width):
    return (pl.BlockSpec((cl, width), lambda b: (b, 0)), pl.BlockSpec((t, width), lambda b: (b, 0)))


def _seg_blocks_t(cl, t, nrow, row0):
    blk = row0 // nrow
    return (pl.BlockSpec((nrow, cl), lambda b: (blk, b)), pl.BlockSpec((nrow, t), lambda b: (blk, b)))


def _mlstm_body(qc_ref, ql_ref, grc_ref, grl_ref, gbr_ref, on_ref, oc_ref, ol_ref,
                hsc, hsl, ktc, ktl, *, need_ctx):
    L = SCAN_CHUNK
    PW = 2 * HEAD_DIM
    lower, upper = _tri_masks(L)
    tri_lo = jnp.where(lower, 1.0, 0.0).astype(bf16)
    tri_up = jnp.where(upper, 1.0, 0.0).astype(bf16)
    lane = lax.broadcasted_iota(jnp.int32, (1, PW), 1)
    half0 = lane < HEAD_DIM
    rr = lax.broadcasted_iota(jnp.int32, (PW, PW), 0)
    cc = lax.broadcasted_iota(jnp.int32, (PW, PW), 1)
    eye = jnp.where(rr == cc, 1.0, 0.0).astype(bf16)
    zero_b = jnp.zeros((), bf16)
    lane_head = lax.broadcasted_iota(jnp.int32, (1, GROUP_W), 1) // HEAD_DIM
    row_head = lax.broadcasted_iota(jnp.int32, (GROUP_W, 1), 0) // HEAD_DIM
    rr4 = lax.broadcasted_iota(jnp.int32, (GROUP_W, 2 * GROUP_W), 0) // HEAD_DIM
    cc4 = (lax.broadcasted_iota(jnp.int32, (GROUP_W, 2 * GROUP_W), 1) % GROUP_W) // HEAD_DIM
    blockdiag4 = rr4 == cc4
    ones_head = [jnp.broadcast_to(jnp.where(lane_head == h, 1.0, 0.0), (L, GROUP_W)).astype(bf16)
                 for h in range(HEADS)]
    ones_all = jnp.ones((L, GROUP_W), bf16)

    def transpose_keys(q_ref, kt_ref):
        for r in range(0, q_ref.shape[0], L):
            for p in range(HEADS // 2):
                k = q_ref[r:r + L, GROUP_W + p * PW:GROUP_W + (p + 1) * PW] * (HEAD_DIM ** -0.5)
                kt_ref[p * PW:(p + 1) * PW, r:r + L] = _dot_nt(eye, k).astype(bf16)

    transpose_keys(qc_ref, ktc)
    transpose_keys(ql_ref, ktl)

    def direction(q_ref, kt_ref, gr_ref, j, rev, carry, off):
        r0 = pl.multiple_of(off + j * L, L)
        base = 2 * HEADS if rev else 0
        mask = upper if rev else lower
        grow = gr_ref[:, pl.ds(r0, L)] + gbr_ref[...]
        lfr = LOG2E * _log_sigmoid(grow)
        brows = _dot_tri(lfr, tri_lo if rev else tri_up)
        bcols = jnp.transpose(brows)
        r_all = LOG2E * grow[base:base + HEADS, :] - brows[base + HEADS:base + 2 * HEADS, :]
        wide = lambda x: jnp.broadcast_to(x, (HEADS, L))
        b_end = wide(jnp.sum(lfr[base + HEADS:base + 2 * HEADS, :], axis=1, keepdims=True))
        cn, m_all = carry
        mm = jnp.maximum(m_all, wide(jnp.max(r_all, axis=1, keepdims=True)))
        a_all = jnp.exp2(m_all - mm)
        w_all = jnp.exp2(r_all - mm)
        twice = lambda x: jnp.concatenate([x, x], axis=1)
        q_all = q_ref[pl.ds(r0, L), 0:GROUP_W]
        k_all = q_ref[pl.ds(r0, L), GROUP_W:2 * GROUP_W] * (HEAD_DIM ** -0.5)
        v_all = q_ref[pl.ds(r0, L), 2 * GROUP_W:3 * GROUP_W]
        kt_all = kt_ref[:, pl.ds(r0, L)]
        qk = _dot_nt(jnp.concatenate([jnp.where(lane_head == h, q_all, zero_b) for h in range(HEADS)], axis=0), k_all)
        sb, wi, em = [], [], []
        for h in range(HEADS):
            rm = jnp.where(mask, r_all[h:h + 1, :], NEG)
            m_h = jnp.broadcast_to(m_all[h:h + 1, :], (L, L))
            c = jnp.maximum(m_h, jnp.broadcast_to(jnp.max(rm, axis=1, keepdims=True), (L, L)))
            sb.append((qk[h * L:(h + 1) * L] * jnp.exp2(rm - c)).astype(bf16))
            wi.append(jnp.exp2(m_h - c))
            bcol = bcols[:, base + HEADS + h:base + HEADS + h + 1]
            em.append(jnp.exp2(-(jnp.broadcast_to(bcol, (L, L)) + c)))
        per_head = lambda xs: jnp.concatenate([jnp.where(half0, xs[0], xs[1]), jnp.where(half0, xs[2], xs[3])], axis=1)
        vo = jnp.concatenate([jnp.concatenate([jnp.where(lane_head == h, v_all, zero_b),
                                               ones_head[h]], axis=1)
                              for h in range(HEADS)], axis=0)
        hx = twice(per_head(wi)) * _dot(q_all, cn.astype(bf16)) + _dot(jnp.concatenate(sb, axis=1), vo)
        den = jnp.maximum(jnp.abs(hx[:, GROUP_W:2 * GROUP_W]), per_head(em))
        out = hx[:, 0:GROUP_W] / den
        wsel = jnp.where(row_head == 0, w_all[0:1, :], jnp.where(row_head == 1, w_all[1:2, :],
                         jnp.where(row_head == 2, w_all[2:3, :], w_all[3:4, :])))
        asel = jnp.where(row_head == 0, a_all[0:1, :], jnp.where(row_head == 1, a_all[1:2, :],
                         jnp.where(row_head == 2, a_all[2:3, :], a_all[3:4, :])))
        ktw = (kt_all.astype(f32) * wsel).astype(bf16)
        upd = _dot(ktw, jnp.concatenate([v_all, ones_all], axis=1))
        asel4 = jnp.concatenate([asel] * (2 * GROUP_W // L), axis=1)
        return (asel4 * cn + jnp.where(blockdiag4, upd, 0.0), b_end + mm), out

    def segment(q_ref, kt_ref, gr_ref, hs_ref, carry):
        seg = q_ref.shape[0] // SCAN_BATCH
        n = seg // L

        def body(i, carry):
            out, pieces = [], []
            for g, (fw, bw) in enumerate(carry):
                fw, h_f = direction(q_ref, kt_ref, gr_ref, i, False, fw, g * seg)
                bw, h_b = direction(q_ref, kt_ref, gr_ref, n - 1 - i, True, bw, g * seg)
                out.append((fw, bw))
                pieces += [h_f, h_b]
            hs_ref[i] = jnp.concatenate(pieces, axis=1)
            return tuple(out)

        return lax.fori_loop(0, n, body, carry, unroll=SCAN_UNROLL)

    def init():
        return jnp.zeros((GROUP_W, 2 * GROUP_W), f32), jnp.zeros((HEADS, L), f32)

    carry = segment(qc_ref, ktc, grc_ref, hsc, tuple((init(), init()) for _ in range(SCAN_BATCH)))
    segment(ql_ref, ktl, grl_ref, hsl, carry)

    gmat = _group_mean_matrix(GROUP_W, HEAD_DIM)

    def finish(q_ref, hs_ref, o_ref):
        seg = o_ref.shape[0] // SCAN_BATCH
        n = seg // L
        for g in range(SCAN_BATCH):
            for j in range(n):
                r = g * seg + j * L
                c0 = g * 2 * GROUP_W
                hh = hs_ref[j, :, c0:c0 + GROUP_W] + hs_ref[n - 1 - j, :, c0 + GROUP_W:c0 + 2 * GROUP_W]
                hn = hh * _group_rsqrt(hh, gmat) * on_ref[...]
                og = q_ref[r:r + L, 3 * GROUP_W:4 * GROUP_W].astype(f32)
                o_ref[r:r + L, :] = (_sigmoid(og) * hn).astype(o_ref.dtype)

    finish(ql_ref, hsl, ol_ref)
    if need_ctx:
        finish(qc_ref, hsc, oc_ref)
    else:
        oc_ref[...] = jnp.zeros(oc_ref.shape, oc_ref.dtype)


def _mlstm(qkvo_c, qkvo_l, gt_c, gt_l, gate_b, out_norm, batch, need_ctx):
    cl, t = qkvo_c.shape[0] // batch, qkvo_l.shape[0] // batch
    ng = 4 * HEADS
    bc, bt = SCAN_BATCH * cl, SCAN_BATCH * t
    gb_row = gate_b.reshape(ng, 1)
    qc_spec, ql_spec = _seg_blocks(bc, bt, 4 * GROUP_W)
    grc_spec, grl_spec = _seg_blocks_t(bc, bt, ng, GT_MG)
    oc_spec, ol_spec = _seg_blocks(bc, bt, GROUP_W)
    return pl.pallas_call(
        functools.partial(_mlstm_body, need_ctx=need_ctx),
        grid=(batch // SCAN_BATCH,),
        in_specs=[qc_spec, ql_spec, grc_spec, grl_spec, _resident((ng, 1)), _resident((1, GROUP_W))],
        out_specs=[oc_spec, ol_spec],
        out_shape=[jax.ShapeDtypeStruct((batch * cl, GROUP_W), bf16),
                   jax.ShapeDtypeStruct((batch * t, GROUP_W), bf16)],
        scratch_shapes=[pltpu.VMEM((cl // SCAN_CHUNK, SCAN_CHUNK, 2 * SCAN_BATCH * GROUP_W), f32),
                        pltpu.VMEM((t // SCAN_CHUNK, SCAN_CHUNK, 2 * SCAN_BATCH * GROUP_W), f32),
                        pltpu.VMEM((GROUP_W, bc), bf16), pltpu.VMEM((GROUP_W, bt), bf16)],
        compiler_params=_params(("parallel",)),
        name="mlstm",
    )(qkvo_c, qkvo_l, gt_c, gt_l, gb_row, out_norm.reshape(1, GROUP_W))


def _ssd_body(zc_ref, zl_ref, xc_ref, xl_ref, drc_ref, drl_ref, cw_ref, cb_ref, dbr_ref,
              alr_ref, dsk_ref, ng_ref, oc_ref, ol_ref,
              xac, xal, ysc, ysl, xpad, btc, btl, *, need_ctx):
    L = SCAN_CHUNK
    N = SSD_STATE
    lower, upper = _tri_masks(L)
    tri_lo = jnp.where(lower, 1.0, 0.0).astype(bf16)
    tri_up = jnp.where(upper, 1.0, 0.0).astype(bf16)
    a_row = -LOG2E * jnp.exp(alr_ref[...])
    pad = 8
    half = SSD_CONV // 2

    def conv_act(x_ref, xa_ref):
        n = x_ref.shape[0] // SCAN_BATCH
        zeros = jnp.zeros((pad, SSD_XBC), f32)
        blk = 256
        for g in range(SCAN_BATCH):
            xpad[0:pad, :] = zeros
            xpad[pad + n:2 * pad + n, :] = zeros
            for r in range(0, n, blk):
                xpad[pad + r:pad + r + blk, :] = x_ref[g * n + r:g * n + r + blk, :].astype(f32)
            for r in range(0, n, blk):
                slab = xpad[r:r + blk + 2 * pad, :]
                y = jnp.zeros((blk, SSD_XBC), f32) + cb_ref[...]
                for kk in range(SSD_CONV):
                    s = kk - half
                    shifted = slab if s == 0 else pltpu.roll(slab, (-s) % slab.shape[0], 0)
                    y = y + cw_ref[kk:kk + 1, :] * shifted[pad:pad + blk, :]
                xa_ref[g * n + r:g * n + r + blk, :] = _silu(y)

    conv_act(xc_ref, xac)

    PW = 2 * HEAD_DIM
    lane = lax.broadcasted_iota(jnp.int32, (1, PW), 1)
    half0 = lane < HEAD_DIM
    row_half0 = lax.broadcasted_iota(jnp.int32, (PW, 1), 0) < N
    rr = lax.broadcasted_iota(jnp.int32, (PW, PW), 0)
    cc = lax.broadcasted_iota(jnp.int32, (PW, PW), 1)
    eye = jnp.where(rr == cc, 1.0, 0.0).astype(bf16)
    zero_b = jnp.zeros((), bf16)

    def transpose_b(xa_ref, bt_ref):
        for r in range(0, xa_ref.shape[0], L):
            bm = xa_ref[r:r + L, GROUP_W:GROUP_W + PW].astype(bf16)
            bt_ref[:, r:r + L] = _dot_nt(eye, bm).astype(bf16)

    transpose_b(xac, btc)

    def direction(xa_ref, bt_ref, dr_ref, j, rev, states, off):
        r0 = pl.multiple_of(off + j * L, L)
        base = HEADS if rev else 0
        mask = upper if rev else lower
        dtr = _softplus(dr_ref[:, pl.ds(r0, L)] + dbr_ref[...])
        ar = dtr * a_row
        cumr = _dot_tri(ar, tri_lo if rev else tri_up)
        cumc = jnp.transpose(cumr)
        cum_end = jnp.broadcast_to(jnp.sum(ar[base:base + HEADS, :], axis=1, keepdims=True), (HEADS, L))
        dec_all = jnp.exp2(cum_end - cumr[base:base + HEADS, :]) * dtr[base:base + HEADS, :]
        aexp_all = jnp.exp2(cum_end)
        bp = xa_ref[pl.ds(r0, L), GROUP_W:GROUP_W + PW].astype(bf16)
        cp = xa_ref[pl.ds(r0, L), GROUP_W + PW:GROUP_W + 2 * PW].astype(bf16)
        btp = bt_ref[:, pl.ds(r0, L)].astype(f32)
        new_states, outs = [], []
        for g in range(SSD_GROUPS):
            G = _dot_nt(jnp.where(half0 if g == 0 else ~half0, cp, zero_b), bp)
            xpb = xa_ref[pl.ds(r0, L), g * PW:(g + 1) * PW].astype(bf16)
            uu = _dot(jnp.concatenate([btp * dec_all[2 * g:2 * g + 1, :], btp * dec_all[2 * g + 1:2 * g + 2, :]],
                                      axis=0).astype(bf16), xpb)
            ws, ecs, us = [], [], []
            for e in range(2):
                h = 2 * g + e
                idx = base + h
                cum_cb = jnp.broadcast_to(cumc[:, idx:idx + 1], (L, L))
                seg = jnp.exp2(jnp.where(mask, cum_cb - cumr[idx:idx + 1, :], NEG))
                ws.append((G * seg * dtr[idx:idx + 1, :]).astype(bf16))
                ecs.append(jnp.exp2(cum_cb))
                us.append(aexp_all[h:h + 1, :] * states[g] + uu[e * PW:(e + 1) * PW])
            xbd = jnp.concatenate([jnp.where(half0, xpb, zero_b), jnp.where(half0, zero_b, xpb)], axis=0)
            y = _dot(jnp.concatenate(ws, axis=1), xbd) + jnp.where(half0, ecs[0], ecs[1]) * _dot(cp, states[g].astype(bf16))
            outs.append(y)
            new_states.append(jnp.where(row_half0 if g == 0 else ~row_half0, jnp.where(half0, us[0], us[1]), 0.0))
        return new_states, jnp.concatenate(outs, axis=1)

    def segment(xa_ref, bt_ref, dr_ref, ys_ref, carry):
        seg = xa_ref.shape[0] // SCAN_BATCH
        n = seg // L

        def body(i, carry):
            out, pieces = [], []
            for g, (fw, bw) in enumerate(carry):
                fw, y_f = direction(xa_ref, bt_ref, dr_ref, i, False, fw, g * seg)
                bw, y_b = direction(xa_ref, bt_ref, dr_ref, n - 1 - i, True, bw, g * seg)
                out.append((fw, bw))
                pieces += [y_f, y_b]
            ys_ref[i] = jnp.concatenate(pieces, axis=1)
            return tuple(out)

        return lax.fori_loop(0, n, body, carry, unroll=SCAN_UNROLL)

    init = [jnp.zeros((PW, PW), f32) for _ in range(SSD_GROUPS)]
    carry = segment(xac, btc, drc_ref, ysc, tuple((init, list(init)) for _ in range(SCAN_BATCH)))
    conv_act(xl_ref, xal)
    transpose_b(xal, btl)
    segment(xal, btl, drl_ref, ysl, carry)

    gmat = _group_mean_matrix(GROUP_W, SSD_NORM_GROUP)

    def finish(z_ref, xa_ref, ys_ref, o_ref):
        seg = o_ref.shape[0] // SCAN_BATCH
        n = seg // L
        for g in range(SCAN_BATCH):
            for j in range(n):
                r = g * seg + j * L
                c0 = g * 2 * GROUP_W
                yy = ys_ref[j, :, c0:c0 + GROUP_W] + ys_ref[n - 1 - j, :, c0 + GROUP_W:c0 + 2 * GROUP_W] \
                    + dsk_ref[...] * xa_ref[r:r + L, 0:GROUP_W]
                gt = yy * _silu(z_ref[r:r + L, :].astype(f32))
                o_ref[r:r + L, :] = (gt * _group_rsqrt(gt, gmat) * ng_ref[...]).astype(o_ref.dtype)

    finish(zl_ref, xal, ysl, ol_ref)
    if need_ctx:
        finish(zc_ref, xac, ysc, oc_ref)
    else:
        oc_ref[...] = jnp.zeros(oc_ref.shape, oc_ref.dtype)


def _ssd(z_c, z_l, xbc_c, xbc_l, gt_c, gt_l, conv_w, conv_b, dt_bias, a_log, d_skip, norm_g, batch, need_ctx):
    cl, t = z_c.shape[0] // batch, z_l.shape[0] // batch
    nd = 2 * HEADS
    bc, bt = SCAN_BATCH * cl, SCAN_BATCH * t
    cw = jnp.zeros((8, SSD_XBC), f32).at[:SSD_CONV].set(conv_w)
    zc_spec, zl_spec = _seg_blocks(bc, bt, GROUP_W)
    xc_spec, xl_spec = _seg_blocks(bc, bt, SSD_XBC)
    drc_spec, drl_spec = _seg_blocks_t(bc, bt, nd, GT_DT)
    oc_spec, ol_spec = _seg_blocks(bc, bt, GROUP_W)
    return pl.pallas_call(
        functools.partial(_ssd_body, need_ctx=need_ctx),
        grid=(batch // SCAN_BATCH,),
        in_specs=[zc_spec, zl_spec, xc_spec, xl_spec, drc_spec, drl_spec,
                  _resident((8, SSD_XBC)), _resident((1, SSD_XBC)), _resident((nd, 1)), _resident((nd, 1)),
                  _resident((1, GROUP_W)), _resident((1, GROUP_W))],
        out_specs=[oc_spec, ol_spec],
        out_shape=[jax.ShapeDtypeStruct((batch * cl, GROUP_W), bf16),
                   jax.ShapeDtypeStruct((batch * t, GROUP_W), bf16)],
        scratch_shapes=[pltpu.VMEM((bc, SSD_XBC), f32), pltpu.VMEM((bt, SSD_XBC), f32),
                        pltpu.VMEM((cl // SCAN_CHUNK, SCAN_CHUNK, 2 * SCAN_BATCH * GROUP_W), f32),
                        pltpu.VMEM((t // SCAN_CHUNK, SCAN_CHUNK, 2 * SCAN_BATCH * GROUP_W), f32),
                        pltpu.VMEM((t + 16, SSD_XBC), f32),
                        pltpu.VMEM((2 * SSD_STATE, bc), bf16), pltpu.VMEM((2 * SSD_STATE, bt), bf16)],
        compiler_params=_params(("parallel",)),
        name="ssd",
    )(z_c, z_l, xbc_c, xbc_l, gt_c, gt_l, cw, conv_b.reshape(1, SSD_XBC),
      dt_bias.reshape(nd, 1), a_log.reshape(nd, 1),
      jnp.repeat(d_skip, HEAD_DIM).reshape(1, GROUP_W), norm_g.reshape(1, GROUP_W))


def _mla_body(ac_ref, al_ref, qn_ref, kvn_ref, wq_ref, wkv_ref, qg_ref, kg_ref, krg_ref, cos_ref, sin_ref,
              oc_ref, ol_ref, q_s, k_s, vt_s, *, need_ctx):
    cl, t = ac_ref.shape[0], al_ref.shape[0]
    scale = MLA_QK ** -0.5 * LOG2E
    rr = lax.broadcasted_iota(jnp.int32, (LANES, LANES), 0)
    cc = lax.broadcasted_iota(jnp.int32, (LANES, LANES), 1)
    in_nope = (rr < MLA_NOPE) & (cc < MLA_NOPE)
    in_rope = (rr >= MLA_NOPE) & (rr < MLA_QK) & (cc >= MLA_NOPE) & (cc < MLA_QK)
    head_gmat = jnp.where(in_nope, 1.0 / MLA_NOPE, jnp.where(in_rope, 1.0 / MLA_ROPE, 0.0)).astype(bf16)
    rot = _rotate_half_matrix(MLA_ROPE // 4)
    er = lax.broadcasted_iota(jnp.int32, (GROUP_W, GROUP_W), 0)
    ec = lax.broadcasted_iota(jnp.int32, (GROUP_W, GROUP_W), 1)
    eye = jnp.where(er == ec, 1.0, 0.0).astype(bf16)

    def head_norm(x):
        return x * _group_rsqrt(x, head_gmat)

    def project(a_ref, row0, n, roped):
        blk = 256
        for r in range(0, n, blk):
            a = a_ref[r:r + blk, :].astype(f32)
            aq, akv, akr = a[:, 0:GROUP_W], a[:, GROUP_W:GROUP_W + LANES], a[:, GROUP_W + LANES:GROUP_W + 2 * LANES]
            qh = _dot((aq * _row_rsqrt(aq, GROUP_W) * qn_ref[...]).astype(bf16), wq_ref[...])
            kv = _dot((akv * _row_rsqrt(akv, kvn_ref.shape[1]) * kvn_ref[...]).astype(bf16), wkv_ref[...])
            kr = akr * _row_rsqrt(akr, MLA_ROPE) * krg_ref[...]
            if roped:
                cos, sin = cos_ref[r:r + blk, :], sin_ref[r:r + blk, :]
                kr = _rope(kr, cos, sin, rot)
            for h in range(HEADS):
                qx = head_norm(qh[:, h * LANES:(h + 1) * LANES]) * qg_ref[...]
                if roped:
                    qx = _rope(qx, cos, sin, rot)
                q_s[h, row0 + r:row0 + r + blk, :] = (qx * scale).astype(bf16)
                kx = kv[:, h * LANES:(h + 1) * LANES]
                kx = kx * _row_rsqrt(kx, MLA_NOPE) * kg_ref[...]
                k_s[h, row0 + r:row0 + r + blk, :] = (kx + kr).astype(bf16)
            vt_s[:, row0 + r:row0 + r + blk] = _dot_nt(eye, kv[:, HEADS * LANES:].astype(bf16)).astype(bf16)

    project(ac_ref, 0, cl, False)
    project(al_ref, cl, t, True)

    def attend(q0, nq, nk, o_ref, o0):
        outs = []
        sts = [_dot_nt(k_s[h, 0:nk, :], q_s[h, pl.ds(q0, nq), :]) for h in range(HEADS)]
        for h in range(HEADS):
            st = sts[h]
            p = jnp.exp2(st - jnp.max(st, axis=0, keepdims=True))
            den = jnp.sum(p, axis=0, keepdims=True)
            outs.append(_dot(vt_s[h * HEAD_DIM:(h + 1) * HEAD_DIM, 0:nk], p.astype(bf16)) / den)
        ot = jnp.concatenate(outs, axis=0).astype(bf16)
        for c in range(0, nq, GROUP_W):
            o_ref[pl.ds(o0 + c, GROUP_W), :] = _dot_nt(eye, ot[:, c:c + GROUP_W]).astype(o_ref.dtype)

    def body(i, _):
        o0 = pl.multiple_of(i * MLA_QBLK, MLA_QBLK)
        attend(cl + o0, MLA_QBLK, cl + t, ol_ref, o0)
        return 0

    lax.fori_loop(0, t // MLA_QBLK, body, 0)
    if need_ctx:
        attend(0, cl, cl, oc_ref, 0)
    else:
        oc_ref[...] = jnp.zeros(oc_ref.shape, oc_ref.dtype)


def _mla(a_c, a_l, q_norm, kv_norm, wq_b, wkv_b, q_gain, k_gain, cos, sin, batch, need_ctx):
    cl, t = a_c.shape[0] // batch, a_l.shape[0] // batch
    kvl = kv_norm.shape[0]
    wq = jnp.zeros((GROUP_W, HEADS, LANES), f32).at[:, :, :MLA_QK].set(wq_b.reshape(GROUP_W, HEADS, MLA_QK))
    wkv4 = wkv_b.reshape(kvl, HEADS, MLA_NOPE + HEAD_DIM)
    wk = jnp.zeros((kvl, HEADS, LANES), f32).at[:, :, :MLA_NOPE].set(wkv4[:, :, :MLA_NOPE])
    wkv = jnp.concatenate([wk.reshape(kvl, HEADS * LANES), wkv4[:, :, MLA_NOPE:].reshape(kvl, GROUP_W)], axis=1)
    slab = lambda v, off: jnp.zeros((1, LANES), f32).at[0, off:off + v.shape[0]].set(v)
    qg = slab(q_gain, 0)
    kg = slab(k_gain[:MLA_NOPE], 0)
    krg = slab(k_gain[MLA_NOPE:], MLA_NOPE)
    ac_spec, al_spec = _seg_blocks(cl, t, 2 * GROUP_W)
    oc_spec, ol_spec = _seg_blocks(cl, t, GROUP_W)
    return pl.pallas_call(
        functools.partial(_mla_body, need_ctx=need_ctx),
        grid=(batch,),
        in_specs=[ac_spec, al_spec, _resident((1, GROUP_W)), _resident((1, LANES)),
                  _resident((GROUP_W, HEADS * LANES)), _resident((kvl, HEADS * LANES + GROUP_W)),
                  _resident((1, LANES)), _resident((1, LANES)), _resident((1, LANES)),
                  _resident((t, LANES)), _resident((t, LANES))],
        out_specs=[oc_spec, ol_spec],
        out_shape=[jax.ShapeDtypeStruct((batch * cl, GROUP_W), bf16),
                   jax.ShapeDtypeStruct((batch * t, GROUP_W), bf16)],
        scratch_shapes=[pltpu.VMEM((HEADS, cl + t, LANES), bf16), pltpu.VMEM((HEADS, cl + t, LANES), bf16),
                        pltpu.VMEM((GROUP_W, cl + t), bf16)],
        compiler_params=_params(("parallel",)),
        name="mla",
    )(a_c, a_l, q_norm.reshape(1, GROUP_W), kv_norm.reshape(1, kvl),
      wq.reshape(GROUP_W, HEADS * LANES).astype(bf16), wkv.astype(bf16), qg, kg, krg, cos, sin)


def _swa_body(wc_ref, wl_ref, qg_ref, kg_ref, sink_ref, cos_ref, sin_ref, oc_ref, ol_ref,
              q_s, qc_s, k_s, v_s, kc_s, vc_s, *, need_ctx):
    cl, t = wc_ref.shape[0], wl_ref.shape[0]
    scale = HEAD_DIM ** -0.5 * LOG2E
    kvw = SWA_KV_HEADS * HEAD_DIM
    blk = SWA_BLK
    gmat_q = _group_mean_matrix(GROUP_W, HEAD_DIM)
    gmat_k = _group_mean_matrix(kvw, HEAD_DIM)
    rot = _rotate_half_matrix(HEAD_DIM // 4)

    def project(w_ref, n, roped, qdst, kdst, vdst, row0):
        step = 256
        for r in range(0, n, step):
            w = w_ref[r:r + step, :].astype(f32)
            q, k = w[:, 0:GROUP_W], w[:, GROUP_W:GROUP_W + kvw]
            q = q * _group_rsqrt(q, gmat_q) * qg_ref[...]
            k = k * _group_rsqrt(k, gmat_k) * kg_ref[...]
            if roped:
                cos, sin = cos_ref[r:r + step, :], sin_ref[r:r + step, :]
                q = jnp.concatenate([_rope(q[:, 0:LANES], cos, sin, rot),
                                     _rope(q[:, LANES:2 * LANES], cos, sin, rot)], axis=1)
                k = _rope(k, cos, sin, rot)
            qdst[r:r + step, :] = (q * scale).astype(bf16)
            kdst[row0 + r:row0 + r + step, :] = k.astype(bf16)
            vdst[row0 + r:row0 + r + step, :] = w_ref[r:r + step, GROUP_W + kvw:GROUP_W + 2 * kvw]

    zeros = jnp.zeros((blk, kvw), bf16)
    for s in (k_s, v_s):
        s[0:blk, :] = zeros
        s[blk + t:2 * blk + t, :] = zeros
    project(wc_ref, cl, False, qc_s, kc_s, vc_s, 0)
    project(wl_ref, t, True, q_s, k_s, v_s, blk)

    rr = lax.broadcasted_iota(jnp.int32, (HEADS * blk, 3 * blk), 0) % blk
    jj = lax.broadcasted_iota(jnp.int32, (HEADS * blk, 3 * blk), 1)
    band = (jj - rr >= 0) & (jj - rr <= 2 * SWA_WINDOW)
    half0 = lax.broadcasted_iota(jnp.int32, (1, LANES), 1) < HEAD_DIM
    zero_b = jnp.zeros((), bf16)

    def wide(x, n):
        return jnp.broadcast_to(x, (x.shape[0], n))

    def stacked_queries(q_ref, r0, n):
        qa, qb = q_ref[pl.ds(r0, n), 0:LANES], q_ref[pl.ds(r0, n), LANES:2 * LANES]
        return jnp.concatenate([jnp.where(half0, qa, zero_b), jnp.where(half0, qb, zero_b),
                                jnp.where(half0, zero_b, qa), jnp.where(half0, zero_b, qb)], axis=0)

    def stacked_sink(n):
        row = lax.broadcasted_iota(jnp.int32, (HEADS * n, 1), 0)
        return LOG2E * jnp.where(row < n, sink_ref[0:1, :], jnp.where(row < 2 * n, sink_ref[1:2, :],
                                 jnp.where(row < 3 * n, sink_ref[2:3, :], sink_ref[3:4, :])))

    def unstack(o, n):
        return jnp.concatenate([jnp.where(half0, o[0:n], o[2 * n:3 * n]),
                                jnp.where(half0, o[n:2 * n], o[3 * n:4 * n])], axis=1)

    def lat_block(n, _):
        r0 = pl.multiple_of(n * blk, blk)
        kpos = jj + (n - 1) * blk
        valid = band & (kpos >= 0) & (kpos < t)
        q4 = stacked_queries(q_s, r0, blk)
        s = _dot_nt(q4, jnp.concatenate([k_s[pl.ds(r0, 3 * blk), :], kc_s[...]], axis=0))
        s_loc, s_ctx = jnp.where(valid, s[:, 0:3 * blk], NEG), s[:, 3 * blk:]
        sink = stacked_sink(blk)
        m = jnp.maximum(wide(jnp.maximum(jnp.max(s_loc, axis=1, keepdims=True),
                                         jnp.max(s_ctx, axis=1, keepdims=True)), LANES), sink)
        p_loc = jnp.exp2(s_loc - jnp.concatenate([m] * 3, axis=1))
        p_ctx = jnp.exp2(s_ctx - jnp.concatenate([m] * (cl // LANES), axis=1))
        den = wide(jnp.sum(p_loc, axis=1, keepdims=True) + jnp.sum(p_ctx, axis=1, keepdims=True), LANES) \
            + jnp.exp2(sink - m)
        o = _dot(jnp.concatenate([p_loc, p_ctx], axis=1).astype(bf16),
                 jnp.concatenate([v_s[pl.ds(r0, 3 * blk), :], vc_s[...]], axis=0))
        ol_ref[pl.ds(r0, blk), :] = unstack(o / den, blk).astype(ol_ref.dtype)
        return 0

    lax.fori_loop(0, t // blk, lat_block, 0, unroll=SWA_UNROLL)

    if need_ctx:
        half = cl // 2
        for r0 in (0, half):
            s = _dot_nt(stacked_queries(qc_s, r0, half), kc_s[...])
            sink = stacked_sink(half)
            m = jnp.maximum(wide(jnp.max(s, axis=1, keepdims=True), LANES), sink)
            p = jnp.exp2(s - jnp.concatenate([m] * (cl // LANES), axis=1))
            den = wide(jnp.sum(p, axis=1, keepdims=True), LANES) + jnp.exp2(sink - m)
            oc_ref[r0:r0 + half, :] = unstack(_dot(p.astype(bf16), vc_s[...]) / den, half).astype(oc_ref.dtype)
    else:
        oc_ref[...] = jnp.zeros(oc_ref.shape, oc_ref.dtype)


def _swa_head_order(a, axis):
    blocks = jnp.split(a, HEADS, axis=axis)
    return jnp.concatenate([blocks[0], blocks[2], blocks[1], blocks[3]], axis=axis)


def _swa(w_c, w_l, q_gain, k_gain, sink, cos, sin, batch, need_ctx):
    cl, t = w_c.shape[0] // batch, w_l.shape[0] // batch
    kvw = SWA_KV_HEADS * HEAD_DIM
    wc_spec, wl_spec = _seg_blocks(cl, t, 2 * GROUP_W)
    oc_spec, ol_spec = _seg_blocks(cl, t, GROUP_W)
    return pl.pallas_call(
        functools.partial(_swa_body, need_ctx=need_ctx),
        grid=(batch,),
        in_specs=[wc_spec, wl_spec, _resident((1, GROUP_W)), _resident((1, kvw)), _resident((HEADS, LANES)),
                  _resident((t, LANES)), _resident((t, LANES))],
        out_specs=[oc_spec, ol_spec],
        out_shape=[jax.ShapeDtypeStruct((batch * cl, GROUP_W), bf16),
                   jax.ShapeDtypeStruct((batch * t, GROUP_W), bf16)],
        scratch_shapes=[pltpu.VMEM((t, GROUP_W), bf16), pltpu.VMEM((cl, GROUP_W), bf16),
                        pltpu.VMEM((t + 2 * SWA_BLK, kvw), bf16), pltpu.VMEM((t + 2 * SWA_BLK, kvw), bf16),
                        pltpu.VMEM((cl, kvw), bf16), pltpu.VMEM((cl, kvw), bf16)],
        compiler_params=_params(("parallel",)),
        name="swa",
    )(w_c, w_l, jnp.tile(q_gain, HEADS).reshape(1, GROUP_W), jnp.tile(k_gain, SWA_KV_HEADS).reshape(1, kvw),
      jnp.broadcast_to(sink.reshape(HEADS, 1), (HEADS, LANES)), cos, sin)


def _rope_tables(t, rot_dim, lane0):
    pos = jnp.arange(t)
    row, col = (pos // GRID_W).astype(f32), (pos % GRID_W).astype(f32)
    nf = rot_dim // 4
    inv = ROPE_BASE ** (-jnp.arange(nf, dtype=f32) / nf)
    ar, ac = row[:, None] * inv, col[:, None] * inv
    ang = jnp.concatenate([ar, ar, ac, ac], axis=-1)
    return jnp.cos(ang), jnp.sin(ang)


def _pack_in_weight(w_in):
    o = 0
    seg = {}
    for name, n in (("qkvo", 4 * GROUP_W), ("mg", 4 * HEADS), ("a_q", GROUP_W), ("a_kv", GROUP_W // 2),
                    ("a_kr", MLA_ROPE), ("swa", 2 * GROUP_W), ("z", GROUP_W), ("xbc", SSD_XBC), ("dt", 2 * HEADS)):
        seg[name] = w_in[:, o:o + n]
        o += n
    d = w_in.shape[0]
    zeros = lambda n: jnp.zeros((d, n), w_in.dtype)
    packed = jnp.concatenate([
        seg["qkvo"],
        seg["a_q"], seg["a_kv"], zeros(MLA_NOPE), seg["a_kr"], zeros(LANES - MLA_QK),
        _swa_head_order(seg["swa"][:, :GROUP_W], 1), seg["swa"][:, GROUP_W:],
        seg["z"], seg["xbc"]], axis=1)
    gate_t = jnp.concatenate([seg["mg"], seg["dt"], zeros(GT_ROWS - 4 * HEADS - 2 * HEADS)], axis=1).T
    return packed.astype(bf16), gate_t.astype(bf16)


def kernel(x, c, ctx, c_ctx, w_mod, b_mod, ffn1_norm, ffn1_wi, ffn1_wo, mix_norm, w_in, w_out, mlstm_gate_b, mlstm_out_norm, mla_q_norm, mla_kv_norm, mla_wq_b, mla_wkv_b, mla_q_gain, mla_k_gain, swa_q_gain, swa_k_gain, swa_sink, ssd_conv_w, ssd_conv_b, ssd_dt_bias, ssd_a_log, ssd_d, ssd_norm, ffn2_norm, ffn2_wi, ffn2_wo):
    b, t, d = x.shape
    cl = ctx.shape[1]
    depth = w_mod.shape[0]
    dff = ffn1_wo.shape[1]

    cos_m, sin_m = _rope_tables(t, MLA_ROPE, MLA_NOPE)
    pad_id = lambda tab, fill: jnp.concatenate(
        [jnp.full((t, MLA_NOPE), fill, f32), tab, jnp.full((t, LANES - MLA_QK), fill, f32)], axis=1)
    cos_m, sin_m = pad_id(cos_m, 1.0), pad_id(sin_m, 0.0)
    cos_s, sin_s = _rope_tables(t, HEAD_DIM, 0)
    cos_s, sin_s = jnp.tile(cos_s, (1, LANES // HEAD_DIM)), jnp.tile(sin_s, (1, LANES // HEAD_DIM))

    h = x.reshape(b * t, d)
    hc = ctx.reshape(b * cl, d)
    cc = jnp.concatenate([c, c_ctx[None, :]], axis=0)
    for l in range(depth):
        need_ctx = l < depth - 1
        mod = _modulation(cc, w_mod, b_mod, l).reshape(b + 1, N_MOD, d)
        streams = ((0, t), (b, None))

        ffn1_w = (ffn1_norm[l].reshape(1, d), ffn1_wi[l][:, :dff].astype(bf16), ffn1_wi[l][:, dff:].astype(bf16),
                  ffn1_wo[l].astype(bf16))
        ffn2_w = (ffn2_norm[l].reshape(1, d), ffn2_wi[l][:, :dff].astype(bf16), ffn2_wi[l][:, dff:].astype(bf16),
                  ffn2_wo[l].astype(bf16))
        h = _ffn(h, mod, *ffn1_w, 0, *streams[0])
        hc = _ffn(hc, mod, *ffn1_w, 0, *streams[1])
        w_packed, w_gate_t = _pack_in_weight(w_in[l])
        in_w = (mix_norm[l].reshape(1, d), w_packed, w_gate_t)
        qkvo_l, mla_l, swa_l, z_l, xbc_l, gt_l = _inproj(h, mod, *in_w, *streams[0])
        qkvo_c, mla_c, swa_c, z_c, xbc_c, gt_c = _inproj(hc, mod, *in_w, *streams[1])
        a_c, a_l = _mlstm(qkvo_c, qkvo_l, gt_c, gt_l, mlstm_gate_b[l], mlstm_out_norm[l], b, need_ctx)
        m_c, m_l = _mla(mla_c, mla_l, mla_q_norm[l], mla_kv_norm[l], mla_wq_b[l], mla_wkv_b[l], mla_q_gain[l],
                        mla_k_gain[l], cos_m, sin_m, b, need_ctx)
        s_c, s_l = _swa(swa_c, swa_l, swa_q_gain[l], swa_k_gain[l], swa_sink[l], cos_s, sin_s, b, need_ctx)
        d_c, d_l = _ssd(z_c, z_l, xbc_c, xbc_l, gt_c, gt_l, ssd_conv_w[l], ssd_conv_b[l], ssd_dt_bias[l],
                        ssd_a_log[l], ssd_d[l], ssd_norm[l], b, need_ctx)
        wo_b = jnp.concatenate([w_out[l][:2 * GROUP_W], _swa_head_order(w_out[l][2 * GROUP_W:3 * GROUP_W], 0),
                                w_out[l][3 * GROUP_W:]], axis=0).astype(bf16)
        h = _mix_ffn(h, mod, (a_l, m_l, s_l, d_l), wo_b, *ffn2_w, *streams[0])
        if need_ctx:
            hc = _mix_ffn(hc, mod, (a_c, m_c, s_c, d_c), wo_b, *ffn2_w, *streams[1])
    return h.reshape(b, t, d)
```

```python
import functools

import jax
import jax.numpy as jnp
from jax import lax
from jax.experimental import pallas as pl
from jax.experimental.pallas import tpu as pltpu

f32 = jnp.float32
bf16 = jnp.bfloat16

RMS_EPS = 1e-6
ROPE_BASE = 10000.0
GRID_W = 64
N_MOD = 9
HEADS = 4
HEAD_DIM = 64
GROUP_W = HEADS * HEAD_DIM
MLA_NOPE = 64
MLA_ROPE = 32
MLA_QK = MLA_NOPE + MLA_ROPE
SWA_KV_HEADS = 2
SWA_WINDOW = 128
SWA_BLK = 128
SSD_STATE = 64
SSD_GROUPS = 2
SSD_CONV = 5
SSD_XBC = GROUP_W + 2 * SSD_GROUPS * SSD_STATE
SSD_NORM_GROUP = 128

LANES = 128
ROW_TILE = 512
WIDE_ROW_TILE = 1024
FF_CHUNK = 256
SCAN_CHUNK = 128
SCAN_UNROLL = 4
SWA_UNROLL = 4
SCAN_BATCH = 2
MLA_QBLK = 512
NEG = -1e30
LOG2E = 1.4426950408889634
VMEM_LIMIT = 56 * 1024 * 1024

P_QKVO = 0
P_MLA = 1024
P_SWA = 1536
P_Z = 2048
P_XBC = 2304
P_COLS = 2816
GT_MG = 0
GT_DT = 4 * HEADS
GT_ROWS = 32


def _dot(a, b):
    return jnp.dot(a, b, preferred_element_type=f32)


def _dot_nt(a, b):
    return lax.dot_general(a, b, (((1,), (1,)), ((), ())), preferred_element_type=f32)


def _sigmoid(x):
    return 1.0 / (1.0 + jnp.exp(-x))


def _silu(x):
    return x * _sigmoid(x)


def _softplus(x):
    return jnp.maximum(x, 0.0) + jnp.log(1.0 + jnp.exp(-jnp.abs(x)))


def _log_sigmoid(x):
    return -_softplus(-x)


def _split3(x):
    hi = x.astype(bf16)
    r = x - hi.astype(f32)
    mid = r.astype(bf16)
    lo = (r - mid.astype(f32)).astype(bf16)
    return hi, mid, lo


def _dot_tri(x, tri):
    m = x.shape[0]
    terms = jnp.concatenate([p.astype(f32) for p in _split3(x)], axis=0).astype(bf16)
    y = _dot(terms, tri)
    return y[0:m] + y[m:2 * m] + y[2 * m:3 * m]


def _dot2(x, w):
    hi = x.astype(bf16)
    mid = (x - hi.astype(f32)).astype(bf16)
    return _dot(hi, w) + _dot(mid, w)


def _group_mean_matrix(n, gsz):
    r = lax.broadcasted_iota(jnp.int32, (n, n), 0) // gsz
    c = lax.broadcasted_iota(jnp.int32, (n, n), 1) // gsz
    return jnp.where(r == c, 1.0 / gsz, 0.0).astype(bf16)


def _group_rsqrt(x, gmat):
    return lax.rsqrt(_dot2(x * x, gmat) + RMS_EPS)


def _row_rsqrt(x, n_real):
    return lax.rsqrt(jnp.sum(x * x, axis=1, keepdims=True) * (1.0 / n_real) + RMS_EPS)


def _rotate_half_matrix(quarter):
    r = lax.broadcasted_iota(jnp.int32, (LANES, LANES), 0)
    c = lax.broadcasted_iota(jnp.int32, (LANES, LANES), 1)
    first = (c % (2 * quarter)) < quarter
    return jnp.where(first & (r == c + quarter), -1.0, jnp.where(~first & (r == c - quarter), 1.0, 0.0)).astype(bf16)


def _rope(x, cos, sin, rot):
    return x * cos + _dot2(x, rot) * sin


def _resident(shape):
    nd = len(shape)
    return pl.BlockSpec(shape, lambda *_: (0,) * nd, pipeline_mode=pl.Buffered(1))


def _params(sem):
    return pltpu.CompilerParams(dimension_semantics=sem, vmem_limit_bytes=VMEM_LIMIT)


def _mod_body(c_ref, w_ref, b_ref, o_ref):
    s = _silu(c_ref[...])
    o_ref[...] = jnp.dot(s, w_ref[...], preferred_element_type=f32,
                         precision=lax.Precision.HIGHEST) + b_ref[...]


def _modulation(cc, w_all, b_all, layer):
    m, d = cc.shape
    depth, _, n = w_all.shape
    tn = 1024
    return pl.pallas_call(
        _mod_body,
        grid=(n // tn,),
        in_specs=[pl.BlockSpec((m, d), lambda j: (0, 0)),
                  pl.BlockSpec((None, d, tn), lambda j: (layer, 0, j)),
                  pl.BlockSpec((None, 1, tn), lambda j: (layer, 0, j))],
        out_specs=pl.BlockSpec((m, tn), lambda j: (0, j)),
        out_shape=jax.ShapeDtypeStruct((m, n), f32),
        compiler_params=_params(("arbitrary",)),
        name="modulation",
    )(cc, w_all, b_all.reshape(depth, 1, n))


def _norm_mod(x, gain, shift, scale):
    xn = x * _row_rsqrt(x, x.shape[1]) * gain
    return (xn * (1.0 + scale) + shift).astype(bf16)


def _swiglu_half_step(x, mod_ref, g_ref, wig_ref, wiu_ref, wo_ref, mi):
    xb = _norm_mod(x, g_ref[...], mod_ref[mi:mi + 1, :], mod_ref[mi + 1:mi + 2, :])
    acc = jnp.zeros(x.shape, f32)
    for c in range(wig_ref.shape[1] // FF_CHUNK):
        sl = slice(c * FF_CHUNK, (c + 1) * FF_CHUNK)
        g = _dot(xb, wig_ref[:, sl])
        u = _dot(xb, wiu_ref[:, sl])
        acc = acc + _dot((_silu(g) * u).astype(bf16), wo_ref[sl, :])
    return x + (0.5 * mod_ref[mi + 2:mi + 3, :]) * acc


def _ffn_body(x_ref, mod_ref, g_ref, wig_ref, wiu_ref, wo_ref, o_ref, *, mi):
    o_ref[...] = _swiglu_half_step(x_ref[...], mod_ref, g_ref, wig_ref, wiu_ref, wo_ref, mi)


def _mix_ffn_body(x_ref, mod_ref, a_ref, m_ref, w_ref, s_ref, wout_ref, g_ref, wig_ref, wiu_ref, wo_ref, o_ref):
    acc = _dot(a_ref[...], wout_ref[0:GROUP_W, :])
    acc = acc + _dot(m_ref[...], wout_ref[GROUP_W:2 * GROUP_W, :])
    acc = acc + _dot(w_ref[...], wout_ref[2 * GROUP_W:3 * GROUP_W, :])
    acc = acc + _dot(s_ref[...], wout_ref[3 * GROUP_W:4 * GROUP_W, :])
    x = x_ref[...] + mod_ref[5:6, :] * acc
    o_ref[...] = _swiglu_half_step(x, mod_ref, g_ref, wig_ref, wiu_ref, wo_ref, 6)


def _mod_spec(d, mod_base, rows_per_mod, tile):
    if rows_per_mod is None:
        return pl.BlockSpec((None, N_MOD, d), lambda i: (mod_base, 0, 0))
    tiles_per_mod = rows_per_mod // tile
    return pl.BlockSpec((None, N_MOD, d), lambda i: (mod_base + i // tiles_per_mod, 0, 0))


def _ffn(x, mod, gain, wig, wiu, wo, mi, mod_base, rows_per_mod):
    rows, d = x.shape
    dff = wig.shape[1]
    return pl.pallas_call(
        functools.partial(_ffn_body, mi=mi),
        grid=(rows // WIDE_ROW_TILE,),
        in_specs=[pl.BlockSpec((WIDE_ROW_TILE, d), lambda i: (i, 0)),
                  _mod_spec(d, mod_base, rows_per_mod, WIDE_ROW_TILE),
                  _resident((1, d)), _resident((d, dff)), _resident((d, dff)), _resident((dff, d))],
        out_specs=pl.BlockSpec((WIDE_ROW_TILE, d), lambda i: (i, 0)),
        out_shape=jax.ShapeDtypeStruct((rows, d), f32),
        compiler_params=_params(("parallel",)),
        name="ffn",
    )(x, mod, gain, wig, wiu, wo)


def _inproj_body(x_ref, mod_ref, g_ref, w_ref, wt_ref, qkvo_ref, mla_ref, swa_ref, z_ref, xbc_ref, gt_ref):
    xb = _norm_mod(x_ref[...], g_ref[...], mod_ref[3:4, :], mod_ref[4:5, :])
    qkvo_ref[...] = _dot(xb, w_ref[:, P_QKVO:P_MLA]).astype(bf16)
    mla_ref[...] = _dot(xb, w_ref[:, P_MLA:P_SWA]).astype(bf16)
    swa_ref[...] = _dot(xb, w_ref[:, P_SWA:P_Z]).astype(bf16)
    z_ref[...] = _dot(xb, w_ref[:, P_Z:P_XBC]).astype(bf16)
    xbc_ref[...] = _dot(xb, w_ref[:, P_XBC:P_COLS]).astype(bf16)
    gt_ref[...] = _dot_nt(wt_ref[...], xb)


def _inproj(x, mod, gain, w, wt, mod_base, rows_per_mod):
    rows, d = x.shape
    widths = [P_MLA - P_QKVO, P_SWA - P_MLA, P_Z - P_SWA, P_XBC - P_Z, P_COLS - P_XBC]
    return pl.pallas_call(
        _inproj_body,
        grid=(rows // WIDE_ROW_TILE,),
        in_specs=[pl.BlockSpec((WIDE_ROW_TILE, d), lambda i: (i, 0)),
                  _mod_spec(d, mod_base, rows_per_mod, WIDE_ROW_TILE),
                  _resident((1, d)), _resident((d, P_COLS)), _resident((GT_ROWS, d))],
        out_specs=[pl.BlockSpec((WIDE_ROW_TILE, n), lambda i: (i, 0)) for n in widths]
                  + [pl.BlockSpec((GT_ROWS, WIDE_ROW_TILE), lambda i: (0, i))],
        out_shape=[jax.ShapeDtypeStruct((rows, n), bf16) for n in widths]
                  + [jax.ShapeDtypeStruct((GT_ROWS, rows), f32)],
        compiler_params=_params(("parallel",)),
        name="inproj",
    )(x, mod, gain, w, wt)


def _mix_ffn(x, mod, mixed, w_out, gain, wig, wiu, wo, mod_base, rows_per_mod):
    rows, d = x.shape
    dff = wig.shape[1]
    return pl.pallas_call(
        _mix_ffn_body,
        grid=(rows // ROW_TILE,),
        in_specs=[pl.BlockSpec((ROW_TILE, d), lambda i: (i, 0)),
                  _mod_spec(d, mod_base, rows_per_mod, ROW_TILE)]
                 + [pl.BlockSpec((ROW_TILE, GROUP_W), lambda i: (i, 0))] * 4
                 + [_resident(w_out.shape), _resident((1, d)), _resident((d, dff)), _resident((d, dff)),
                    _resident((dff, d))],
        out_specs=pl.BlockSpec((ROW_TILE, d), lambda i: (i, 0)),
        out_shape=jax.ShapeDtypeStruct((rows, d), f32),
        compiler_params=_params(("parallel",)),
        name="mix_ffn",
    )(x, mod, *mixed, w_out, gain, wig, wiu, wo)


def _tri_masks(n):
    r = lax.broadcasted_iota(jnp.int32, (n, n), 0)
    c = lax.broadcasted_iota(jnp.int32, (n, n), 1)
    return r >= c, r <= c


def _seg_blocks(cl, t, width):
    return (pl.BlockSpec((cl, width), lambda b: (b, 0)), pl.BlockSpec((t, width), lambda b: (b, 0)))


def _seg_blocks_t(cl, t, nrow, row0):
    blk = row0 // nrow
    return (pl.BlockSpec((nrow, cl), lambda b: (blk, b)), pl.BlockSpec((nrow, t), lambda b: (blk, b)))


def _mlstm_body(qc_ref, ql_ref, grc_ref, grl_ref, gbr_ref, on_ref, oc_ref, ol_ref,
                hsc, hsl, ktc, ktl, *, need_ctx):
    L = SCAN_CHUNK
    PW = 2 * HEAD_DIM
    lower, upper = _tri_masks(L)
    tri_lo = jnp.where(lower, 1.0, 0.0).astype(bf16)
    tri_up = jnp.where(upper, 1.0, 0.0).astype(bf16)
    lane = lax.broadcasted_iota(jnp.int32, (1, PW), 1)
    half0 = lane < HEAD_DIM
    rr = lax.broadcasted_iota(jnp.int32, (PW, PW), 0)
    cc = lax.broadcasted_iota(jnp.int32, (PW, PW), 1)
    eye = jnp.where(rr == cc, 1.0, 0.0).astype(bf16)
    zero_b = jnp.zeros((), bf16)
    lane_head = lax.broadcasted_iota(jnp.int32, (1, GROUP_W), 1) // HEAD_DIM
    row_head = lax.broadcasted_iota(jnp.int32, (GROUP_W, 1), 0) // HEAD_DIM
    rr4 = lax.broadcasted_iota(jnp.int32, (GROUP_W, 2 * GROUP_W), 0) // HEAD_DIM
    cc4 = (lax.broadcasted_iota(jnp.int32, (GROUP_W, 2 * GROUP_W), 1) % GROUP_W) // HEAD_DIM
    blockdiag4 = rr4 == cc4
    ones_head = [jnp.broadcast_to(jnp.where(lane_head == h, 1.0, 0.0), (L, GROUP_W)).astype(bf16)
                 for h in range(HEADS)]
    ones_all = jnp.ones((L, GROUP_W), bf16)

    def transpose_keys(q_ref, kt_ref):
        for r in range(0, q_ref.shape[0], L):
            for p in range(HEADS // 2):
                k = q_ref[r:r + L, GROUP_W + p * PW:GROUP_W + (p + 1) * PW] * (HEAD_DIM ** -0.5)
                kt_ref[p * PW:(p + 1) * PW, r:r + L] = _dot_nt(eye, k).astype(bf16)

    transpose_keys(qc_ref, ktc)
    transpose_keys(ql_ref, ktl)

    def direction(q_ref, kt_ref, gr_ref, j, rev, carry, off):
        r0 = pl.multiple_of(off + j * L, L)
        base = 2 * HEADS if rev else 0
        mask = upper if rev else lower
        grow = gr_ref[:, pl.ds(r0, L)] + gbr_ref[...]
        lfr = LOG2E * _log_sigmoid(grow)
        brows = _dot_tri(lfr, tri_lo if rev else tri_up)
        bcols = jnp.transpose(brows)
        r_all = LOG2E * grow[base:base + HEADS, :] - brows[base + HEADS:base + 2 * HEADS, :]
        wide = lambda x: jnp.broadcast_to(x, (HEADS, L))
        b_end = wide(jnp.sum(lfr[base + HEADS:base + 2 * HEADS, :], axis=1, keepdims=True))
        cn, m_all = carry
        mm = jnp.maximum(m_all, wide(jnp.max(r_all, axis=1, keepdims=True)))
        a_all = jnp.exp2(m_all - mm)
        w_all = jnp.exp2(r_all - mm)
        twice = lambda x: jnp.concatenate([x, x], axis=1)
        q_all = q_ref[pl.ds(r0, L), 0:GROUP_W]
        k_all = q_ref[pl.ds(r0, L), GROUP_W:2 * GROUP_W] * (HEAD_DIM ** -0.5)
        v_all = q_ref[pl.ds(r0, L), 2 * GROUP_W:3 * GROUP_W]
        kt_all = kt_ref[:, pl.ds(r0, L)]
        qk = _dot_nt(jnp.concatenate([jnp.where(lane_head == h, q_all, zero_b) for h in range(HEADS)], axis=0), k_all)
        sb, wi, em = [], [], []
        for h in range(HEADS):
            rm = jnp.where(mask, r_all[h:h + 1, :], NEG)
            m_h = jnp.broadcast_to(m_all[h:h + 1, :], (L, L))
            c = jnp.maximum(m_h, jnp.broadcast_to(jnp.max(rm, axis=1, keepdims=True), (L, L)))
            sb.append((qk[h * L:(h + 1) * L] * jnp.exp2(rm - c)).astype(bf16))
            wi.append(jnp.exp2(m_h - c))
            bcol = bcols[:, base + HEADS + h:base + HEADS + h + 1]
            em.append(jnp.exp2(-(jnp.broadcast_to(bcol, (L, L)) + c)))
        per_head = lambda xs: jnp.concatenate([jnp.where(half0, xs[0], xs[1]), jnp.where(half0, xs[2], xs[3])], axis=1)
        vo = jnp.concatenate([jnp.concatenate([jnp.where(lane_head == h, v_all, zero_b),
                                               ones_head[h]], axis=1)
                              for h in range(HEADS)], axis=0)
        hx = twice(per_head(wi)) * _dot(q_all, cn.astype(bf16)) + _dot(jnp.concatenate(sb, axis=1), vo)
        den = jnp.maximum(jnp.abs(hx[:, GROUP_W:2 * GROUP_W]), per_head(em))
        out = hx[:, 0:GROUP_W] / den
        wsel = jnp.where(row_head == 0, w_all[0:1, :], jnp.where(row_head == 1, w_all[1:2, :],
                         jnp.where(row_head == 2, w_all[2:3, :], w_all[3:4, :])))
        asel = jnp.where(row_head == 0, a_all[0:1, :], jnp.where(row_head == 1, a_all[1:2, :],
                         jnp.where(row_head == 2, a_all[2:3, :], a_all[3:4, :])))
        ktw = (kt_all.astype(f32) * wsel).astype(bf16)
        upd = _dot(ktw, jnp.concatenate([v_all, ones_all], axis=1))
        asel4 = jnp.concatenate([asel] * (2 * GROUP_W // L), axis=1)
        return (asel4 * cn + jnp.where(blockdiag4, upd, 0.0), b_end + mm), out

    def segment(q_ref, kt_ref, gr_ref, hs_ref, carry):
        seg = q_ref.shape[0] // SCAN_BATCH
        n = seg // L

        def body(i, carry):
            out, pieces = [], []
            for g, (fw, bw) in enumerate(carry):
                fw, h_f = direction(q_ref, kt_ref, gr_ref, i, False, fw, g * seg)
                bw, h_b = direction(q_ref, kt_ref, gr_ref, n - 1 - i, True, bw, g * seg)
                out.append((fw, bw))
                pieces += [h_f, h_b]
            hs_ref[i] = jnp.concatenate(pieces, axis=1)
            return tuple(out)

        return lax.fori_loop(0, n, body, carry, unroll=SCAN_UNROLL)

    def init():
        return jnp.zeros((GROUP_W, 2 * GROUP_W), f32), jnp.zeros((HEADS, L), f32)

    carry = segment(qc_ref, ktc, grc_ref, hsc, tuple((init(), init()) for _ in range(SCAN_BATCH)))
    segment(ql_ref, ktl, grl_ref, hsl, carry)

    gmat = _group_mean_matrix(GROUP_W, HEAD_DIM)

    def finish(q_ref, hs_ref, o_ref):
        seg = o_ref.shape[0] // SCAN_BATCH
        n = seg // L
        for g in range(SCAN_BATCH):
            for j in range(n):
                r = g * seg + j * L
                c0 = g * 2 * GROUP_W
                hh = hs_ref[j, :, c0:c0 + GROUP_W] + hs_ref[n - 1 - j, :, c0 + GROUP_W:c0 + 2 * GROUP_W]
                hn = hh * _group_rsqrt(hh, gmat) * on_ref[...]
                og = q_ref[r:r + L, 3 * GROUP_W:4 * GROUP_W].astype(f32)
                o_ref[r:r + L, :] = (_sigmoid(og) * hn).astype(o_ref.dtype)

    finish(ql_ref, hsl, ol_ref)
    if need_ctx:
        finish(qc_ref, hsc, oc_ref)
    else:
        oc_ref[...] = jnp.zeros(oc_ref.shape, oc_ref.dtype)


def _mlstm(qkvo_c, qkvo_l, gt_c, gt_l, gate_b, out_norm, batch, need_ctx):
    cl, t = qkvo_c.shape[0] // batch, qkvo_l.shape[0] // batch
    ng = 4 * HEADS
    bc, bt = SCAN_BATCH * cl, SCAN_BATCH * t
    gb_row = gate_b.reshape(ng, 1)
    qc_spec, ql_spec = _seg_blocks(bc, bt, 4 * GROUP_W)
    grc_spec, grl_spec = _seg_blocks_t(bc, bt, ng, GT_MG)
    oc_spec, ol_spec = _seg_blocks(bc, bt, GROUP_W)
    return pl.pallas_call(
        functools.partial(_mlstm_body, need_ctx=need_ctx),
        grid=(batch // SCAN_BATCH,),
        in_specs=[qc_spec, ql_spec, grc_spec, grl_spec, _resident((ng, 1)), _resident((1, GROUP_W))],
        out_specs=[oc_spec, ol_spec],
        out_shape=[jax.ShapeDtypeStruct((batch * cl, GROUP_W), bf16),
                   jax.ShapeDtypeStruct((batch * t, GROUP_W), bf16)],
        scratch_shapes=[pltpu.VMEM((cl // SCAN_CHUNK, SCAN_CHUNK, 2 * SCAN_BATCH * GROUP_W), f32),
                        pltpu.VMEM((t // SCAN_CHUNK, SCAN_CHUNK, 2 * SCAN_BATCH * GROUP_W), f32),
                        pltpu.VMEM((GROUP_W, bc), bf16), pltpu.VMEM((GROUP_W, bt), bf16)],
        compiler_params=_params(("parallel",)),
        name="mlstm",
    )(qkvo_c, qkvo_l, gt_c, gt_l, gb_row, out_norm.reshape(1, GROUP_W))


def _ssd_body(zc_ref, zl_ref, xc_ref, xl_ref, drc_ref, drl_ref, cw_ref, cb_ref, dbr_ref,
              alr_ref, dsk_ref, ng_ref, oc_ref, ol_ref,
              xac, xal, ysc, ysl, xpad, btc, btl, *, need_ctx):
    L = SCAN_CHUNK
    N = SSD_STATE
    lower, upper = _tri_masks(L)
    tri_lo = jnp.where(lower, 1.0, 0.0).astype(bf16)
    tri_up = jnp.where(upper, 1.0, 0.0).astype(bf16)
    a_row = -LOG2E * jnp.exp(alr_ref[...])
    pad = 8
    half = SSD_CONV // 2

    def conv_act(x_ref, xa_ref):
        n = x_ref.shape[0] // SCAN_BATCH
        zeros = jnp.zeros((pad, SSD_XBC), f32)
        blk = 256
        for g in range(SCAN_BATCH):
            xpad[0:pad, :] = zeros
            xpad[pad + n:2 * pad + n, :] = zeros
            for r in range(0, n, blk):
                xpad[pad + r:pad + r + blk, :] = x_ref[g * n + r:g * n + r + blk, :].astype(f32)
            for r in range(0, n, blk):
                slab = xpad[r:r + blk + 2 * pad, :]
                y = jnp.zeros((blk, SSD_XBC), f32) + cb_ref[...]
                for kk in range(SSD_CONV):
                    s = kk - half
                    shifted = slab if s == 0 else pltpu.roll(slab, (-s) % slab.shape[0], 0)
                    y = y + cw_ref[kk:kk + 1, :] * shifted[pad:pad + blk, :]
                xa_ref[g * n + r:g * n + r + blk, :] = _silu(y)

    conv_act(xc_ref, xac)

    PW = 2 * HEAD_DIM
    lane = lax.broadcasted_iota(jnp.int32, (1, PW), 1)
    half0 = lane < HEAD_DIM
    row_half0 = lax.broadcasted_iota(jnp.int32, (PW, 1), 0) < N
    rr = lax.broadcasted_iota(jnp.int32, (PW, PW), 0)
    cc = lax.broadcasted_iota(jnp.int32, (PW, PW), 1)
    eye = jnp.where(rr == cc, 1.0, 0.0).astype(bf16)
    zero_b = jnp.zeros((), bf16)

    def transpose_b(xa_ref, bt_ref):
        for r in range(0, xa_ref.shape[0], L):
            bm = xa_ref[r:r + L, GROUP_W:GROUP_W + PW].astype(bf16)
            bt_ref[:, r:r + L] = _dot_nt(eye, bm).astype(bf16)

    transpose_b(xac, btc)

    def direction(xa_ref, bt_ref, dr_ref, j, rev, states, off):
        r0 = pl.multiple_of(off + j * L, L)
        base = HEADS if rev else 0
        mask = upper if rev else lower
        dtr = _softplus(dr_ref[:, pl.ds(r0, L)] + dbr_ref[...])
        ar = dtr * a_row
        cumr = _dot_tri(ar, tri_lo if rev else tri_up)
        cumc = jnp.transpose(cumr)
        cum_end = jnp.broadcast_to(jnp.sum(ar[base:base + HEADS, :], axis=1, keepdims=True), (HEADS, L))
        dec_all = jnp.exp2(cum_end - cumr[base:base + HEADS, :]) * dtr[base:base + HEADS, :]
        aexp_all = jnp.exp2(cum_end)
        bp = xa_ref[pl.ds(r0, L), GROUP_W:GROUP_W + PW].astype(bf16)
        cp = xa_ref[pl.ds(r0, L), GROUP_W + PW:GROUP_W + 2 * PW].astype(bf16)
        btp = bt_ref[:, pl.ds(r0, L)].astype(f32)
        new_states, outs = [], []
        for g in range(SSD_GROUPS):
            G = _dot_nt(jnp.where(half0 if g == 0 else ~half0, cp, zero_b), bp)
            xpb = xa_ref[pl.ds(r0, L), g * PW:(g + 1) * PW].astype(bf16)
            uu = _dot(jnp.concatenate([btp * dec_all[2 * g:2 * g + 1, :], btp * dec_all[2 * g + 1:2 * g + 2, :]],
                                      axis=0).astype(bf16), xpb)
            ws, ecs, us = [], [], []
            for e in range(2):
                h = 2 * g + e
                idx = base + h
                cum_cb = jnp.broadcast_to(cumc[:, idx:idx + 1], (L, L))
                seg = jnp.exp2(jnp.where(mask, cum_cb - cumr[idx:idx + 1, :], NEG))
                ws.append((G * seg * dtr[idx:idx + 1, :]).astype(bf16))
                ecs.append(jnp.exp2(cum_cb))
                us.append(aexp_all[h:h + 1, :] * states[g] + uu[e * PW:(e + 1) * PW])
            xbd = jnp.concatenate([jnp.where(half0, xpb, zero_b), jnp.where(half0, zero_b, xpb)], axis=0)
            y = _dot(jnp.concatenate(ws, axis=1), xbd) + jnp.where(half0, ecs[0], ecs[1]) * _dot(cp, states[g].astype(bf16))
            outs.append(y)
            new_states.append(jnp.where(row_half0 if g == 0 else ~row_half0, jnp.where(half0, us[0], us[1]), 0.0))
        return new_states, jnp.concatenate(outs, axis=1)

    def segment(xa_ref, bt_ref, dr_ref, ys_ref, carry):
        seg = xa_ref.shape[0] // SCAN_BATCH
        n = seg // L

        def body(i, carry):
            out, pieces = [], []
            for g, (fw, bw) in enumerate(carry):
                fw, y_f = direction(xa_ref, bt_ref, dr_ref, i, False, fw, g * seg)
                bw, y_b = direction(xa_ref, bt_ref, dr_ref, n - 1 - i, True, bw, g * seg)
                out.append((fw, bw))
                pieces += [y_f, y_b]
            ys_ref[i] = jnp.concatenate(pieces, axis=1)
            return tuple(out)

        return lax.fori_loop(0, n, body, carry, unroll=SCAN_UNROLL)

    init = [jnp.zeros((PW, PW), f32) for _ in range(SSD_GROUPS)]
    carry = segment(xac, btc, drc_ref, ysc, tuple((init, list(init)) for _ in range(SCAN_BATCH)))
    conv_act(xl_ref, xal)
    transpose_b(xal, btl)
    segment(xal, btl, drl_ref, ysl, carry)

    gmat = _group_mean_matrix(GROUP_W, SSD_NORM_GROUP)

    def finish(z_ref, xa_ref, ys_ref, o_ref):
        seg = o_ref.shape[0] // SCAN_BATCH
        n = seg // L
        for g in range(SCAN_BATCH):
            for j in range(n):
                r = g * seg + j * L
                c0 = g * 2 * GROUP_W
                yy = ys_ref[j, :, c0:c0 + GROUP_W] + ys_ref[n - 1 - j, :, c0 + GROUP_W:c0 + 2 * GROUP_W] \
                    + dsk_ref[...] * xa_ref[r:r + L, 0:GROUP_W]
                gt = yy * _silu(z_ref[r:r + L, :].astype(f32))
                o_ref[r:r + L, :] = (gt * _group_rsqrt(gt, gmat) * ng_ref[...]).astype(o_ref.dtype)

    finish(zl_ref, xal, ysl, ol_ref)
    if need_ctx:
        finish(zc_ref, xac, ysc, oc_ref)
    else:
        oc_ref[...] = jnp.zeros(oc_ref.shape, oc_ref.dtype)


def _ssd(z_c, z_l, xbc_c, xbc_l, gt_c, gt_l, conv_w, conv_b, dt_bias, a_log, d_skip, norm_g, batch, need_ctx):
    cl, t = z_c.shape[0] // batch, z_l.shape[0] // batch
    nd = 2 * HEADS
    bc, bt = SCAN_BATCH * cl, SCAN_BATCH * t
    cw = jnp.zeros((8, SSD_XBC), f32).at[:SSD_CONV].set(conv_w)
    zc_spec, zl_spec = _seg_blocks(bc, bt, GROUP_W)
    xc_spec, xl_spec = _seg_blocks(bc, bt, SSD_XBC)
    drc_spec, drl_spec = _seg_blocks_t(bc, bt, nd, GT_DT)
    oc_spec, ol_spec = _seg_blocks(bc, bt, GROUP_W)
    return pl.pallas_call(
        functools.partial(_ssd_body, need_ctx=need_ctx),
        grid=(batch // SCAN_BATCH,),
        in_specs=[zc_spec, zl_spec, xc_spec, xl_spec, drc_spec, drl_spec,
                  _resident((8, SSD_XBC)), _resident((1, SSD_XBC)), _resident((nd, 1)), _resident((nd, 1)),
                  _resident((1, GROUP_W)), _resident((1, GROUP_W))],
        out_specs=[oc_spec, ol_spec],
        out_shape=[jax.ShapeDtypeStruct((batch * cl, GROUP_W), bf16),
                   jax.ShapeDtypeStruct((batch * t, GROUP_W), bf16)],
        scratch_shapes=[pltpu.VMEM((bc, SSD_XBC), f32), pltpu.VMEM((bt, SSD_XBC), f32),
                        pltpu.VMEM((cl // SCAN_CHUNK, SCAN_CHUNK, 2 * SCAN_BATCH * GROUP_W), f32),
                        pltpu.VMEM((t // SCAN_CHUNK, SCAN_CHUNK, 2 * SCAN_BATCH * GROUP_W), f32),
                        pltpu.VMEM((t + 16, SSD_XBC), f32),
                        pltpu.VMEM((2 * SSD_STATE, bc), bf16), pltpu.VMEM((2 * SSD_STATE, bt), bf16)],
        compiler_params=_params(("parallel",)),
        name="ssd",
    )(z_c, z_l, xbc_c, xbc_l, gt_c, gt_l, cw, conv_b.reshape(1, SSD_XBC),
      dt_bias.reshape(nd, 1), a_log.reshape(nd, 1),
      jnp.repeat(d_skip, HEAD_DIM).reshape(1, GROUP_W), norm_g.reshape(1, GROUP_W))


def _mla_body(ac_ref, al_ref, qn_ref, kvn_ref, wq_ref, wkv_ref, qg_ref, kg_ref, krg_ref, cos_ref, sin_ref,
              oc_ref, ol_ref, q_s, k_s, vt_s, *, need_ctx):
    cl, t = ac_ref.shape[0], al_ref.shape[0]
    scale = MLA_QK ** -0.5 * LOG2E
    rr = lax.broadcasted_iota(jnp.int32, (LANES, LANES), 0)
    cc = lax.broadcasted_iota(jnp.int32, (LANES, LANES), 1)
    in_nope = (rr < MLA_NOPE) & (cc < MLA_NOPE)
    in_rope = (rr >= MLA_NOPE) & (rr < MLA_QK) & (cc >= MLA_NOPE) & (cc < MLA_QK)
    head_gmat = jnp.where(in_nope, 1.0 / MLA_NOPE, jnp.where(in_rope, 1.0 / MLA_ROPE, 0.0)).astype(bf16)
    rot = _rotate_half_matrix(MLA_ROPE // 4)
    er = lax.broadcasted_iota(jnp.int32, (GROUP_W, GROUP_W), 0)
    ec = lax.broadcasted_iota(jnp.int32, (GROUP_W, GROUP_W), 1)
    eye = jnp.where(er == ec, 1.0, 0.0).astype(bf16)

    def head_norm(x):
        return x * _group_rsqrt(x, head_gmat)

    def project(a_ref, row0, n, roped):
        blk = 256
        for r in range(0, n, blk):
            a = a_ref[r:r + blk, :].astype(f32)
            aq, akv, akr = a[:, 0:GROUP_W], a[:, GROUP_W:GROUP_W + LANES], a[:, GROUP_W + LANES:GROUP_W + 2 * LANES]
            qh = _dot((aq * _row_rsqrt(aq, GROUP_W) * qn_ref[...]).astype(bf16), wq_ref[...])
            kv = _dot((akv * _row_rsqrt(akv, kvn_ref.shape[1]) * kvn_ref[...]).astype(bf16), wkv_ref[...])
            kr = akr * _row_rsqrt(akr, MLA_ROPE) * krg_ref[...]
            if roped:
                cos, sin = cos_ref[r:r + blk, :], sin_ref[r:r + blk, :]
                kr = _rope(kr, cos, sin, rot)
            for h in range(HEADS):
                qx = head_norm(qh[:, h * LANES:(h + 1) * LANES]) * qg_ref[...]
                if roped:
                    qx = _rope(qx, cos, sin, rot)
                q_s[h, row0 + r:row0 + r + blk, :] = (qx * scale).astype(bf16)
                kx = kv[:, h * LANES:(h + 1) * LANES]
                kx = kx * _row_rsqrt(kx, MLA_NOPE) * kg_ref[...]
                k_s[h, row0 + r:row0 + r + blk, :] = (kx + kr).astype(bf16)
            vt_s[:, row0 + r:row0 + r + blk] = _dot_nt(eye, kv[:, HEADS * LANES:].astype(bf16)).astype(bf16)

    project(ac_ref, 0, cl, False)
    project(al_ref, cl, t, True)

    def attend(q0, nq, nk, o_ref, o0):
        outs = []
        sts = [_dot_nt(k_s[h, 0:nk, :], q_s[h, pl.ds(q0, nq), :]) for h in range(HEADS)]
        for h in range(HEADS):
            st = sts[h]
            p = jnp.exp2(st - jnp.max(st, axis=0, keepdims=True))
            den = jnp.sum(p, axis=0, keepdims=True)
            outs.append(_dot(vt_s[h * HEAD_DIM:(h + 1) * HEAD_DIM, 0:nk], p.astype(bf16)) / den)
        ot = jnp.concatenate(outs, axis=0).astype(bf16)
        for c in range(0, nq, GROUP_W):
            o_ref[pl.ds(o0 + c, GROUP_W), :] = _dot_nt(eye, ot[:, c:c + GROUP_W]).astype(o_ref.dtype)

    def body(i, _):
        o0 = pl.multiple_of(i * MLA_QBLK, MLA_QBLK)
        attend(cl + o0, MLA_QBLK, cl + t, ol_ref, o0)
        return 0

    lax.fori_loop(0, t // MLA_QBLK, body, 0)
    if need_ctx:
        attend(0, cl, cl, oc_ref, 0)
    else:
        oc_ref[...] = jnp.zeros(oc_ref.shape, oc_ref.dtype)


def _mla(a_c, a_l, q_norm, kv_norm, wq_b, wkv_b, q_gain, k_gain, cos, sin, batch, need_ctx):
    cl, t = a_c.shape[0] // batch, a_l.shape[0] // batch
    kvl = kv_norm.shape[0]
    wq = jnp.zeros((GROUP_W, HEADS, LANES), f32).at[:, :, :MLA_QK].set(wq_b.reshape(GROUP_W, HEADS, MLA_QK))
    wkv4 = wkv_b.reshape(kvl, HEADS, MLA_NOPE + HEAD_DIM)
    wk = jnp.zeros((kvl, HEADS, LANES), f32).at[:, :, :MLA_NOPE].set(wkv4[:, :, :MLA_NOPE])
    wkv = jnp.concatenate([wk.reshape(kvl, HEADS * LANES), wkv4[:, :, MLA_NOPE:].reshape(kvl, GROUP_W)], axis=1)
    slab = lambda v, off: jnp.zeros((1, LANES), f32).at[0, off:off + v.shape[0]].set(v)
    qg = slab(q_gain, 0)
    kg = slab(k_gain[:MLA_NOPE], 0)
    krg = slab(k_gain[MLA_NOPE:], MLA_NOPE)
    ac_spec, al_spec = _seg_blocks(cl, t, 2 * GROUP_W)
    oc_spec, ol_spec = _seg_blocks(cl, t, GROUP_W)
    return pl.pallas_call(
        functools.partial(_mla_body, need_ctx=need_ctx),
        grid=(batch,),
        in_specs=[ac_spec, al_spec, _resident((1, GROUP_W)), _resident((1, LANES)),
                  _resident((GROUP_W, HEADS * LANES)), _resident((kvl, HEADS * LANES + GROUP_W)),
                  _resident((1, LANES)), _resident((1, LANES)), _resident((1, LANES)),
                  _resident((t, LANES)), _resident((t, LANES))],
        out_specs=[oc_spec, ol_spec],
        out_shape=[jax.ShapeDtypeStruct((batch * cl, GROUP_W), bf16),
                   jax.ShapeDtypeStruct((batch * t, GROUP_W), bf16)],
        scratch_shapes=[pltpu.VMEM((HEADS, cl + t, LANES), bf16), pltpu.VMEM((HEADS, cl + t, LANES), bf16),
                        pltpu.VMEM((GROUP_W, cl + t), bf16)],
        compiler_params=_params(("parallel",)),
        name="mla",
    )(a_c, a_l, q_norm.reshape(1, GROUP_W), kv_norm.reshape(1, kvl),
      wq.reshape(GROUP_W, HEADS * LANES).astype(bf16), wkv.astype(bf16), qg, kg, krg, cos, sin)


def _swa_body(wc_ref, wl_ref, qg_ref, kg_ref, sink_ref, cos_ref, sin_ref, oc_ref, ol_ref,
              q_s, qc_s, k_s, v_s, kc_s, vc_s, *, need_ctx):
    cl, t = wc_ref.shape[0], wl_ref.shape[0]
    scale = HEAD_DIM ** -0.5 * LOG2E
    kvw = SWA_KV_HEADS * HEAD_DIM
    blk = SWA_BLK
    gmat_q = _group_mean_matrix(GROUP_W, HEAD_DIM)
    gmat_k = _group_mean_matrix(kvw, HEAD_DIM)
    rot = _rotate_half_matrix(HEAD_DIM // 4)

    def project(w_ref, n, roped, qdst, kdst, vdst, row0):
        step = 256
        for r in range(0, n, step):
            w = w_ref[r:r + step, :].astype(f32)
            q, k = w[:, 0:GROUP_W], w[:, GROUP_W:GROUP_W + kvw]
            q = q * _group_rsqrt(q, gmat_q) * qg_ref[...]
            k = k * _group_rsqrt(k, gmat_k) * kg_ref[...]
            if roped:
                cos, sin = cos_ref[r:r + step, :], sin_ref[r:r + step, :]
                q = jnp.concatenate([_rope(q[:, 0:LANES], cos, sin, rot),
                                     _rope(q[:, LANES:2 * LANES], cos, sin, rot)], axis=1)
                k = _rope(k, cos, sin, rot)
            qdst[r:r + step, :] = (q * scale).astype(bf16)
            kdst[row0 + r:row0 + r + step, :] = k.astype(bf16)
            vdst[row0 + r:row0 + r + step, :] = w_ref[r:r + step, GROUP_W + kvw:GROUP_W + 2 * kvw]

    zeros = jnp.zeros((blk, kvw), bf16)
    for s in (k_s, v_s):
        s[0:blk, :] = zeros
        s[blk + t:2 * blk + t, :] = zeros
    project(wc_ref, cl, False, qc_s, kc_s, vc_s, 0)
    project(wl_ref, t, True, q_s, k_s, v_s, blk)

    rr = lax.broadcasted_iota(jnp.int32, (HEADS * blk, 3 * blk), 0) % blk
    jj = lax.broadcasted_iota(jnp.int32, (HEADS * blk, 3 * blk), 1)
    band = (jj - rr >= 0) & (jj - rr <= 2 * SWA_WINDOW)
    half0 = lax.broadcasted_iota(jnp.int32, (1, LANES), 1) < HEAD_DIM
    zero_b = jnp.zeros((), bf16)

    def wide(x, n):
        return jnp.broadcast_to(x, (x.shape[0], n))

    def stacked_queries(q_ref, r0, n):
        qa, qb = q_ref[pl.ds(r0, n), 0:LANES], q_ref[pl.ds(r0, n), LANES:2 * LANES]
        return jnp.concatenate([jnp.where(half0, qa, zero_b), jnp.where(half0, qb, zero_b),
                                jnp.where(half0, zero_b, qa), jnp.where(half0, zero_b, qb)], axis=0)

    def stacked_sink(n):
        row = lax.broadcasted_iota(jnp.int32, (HEADS * n, 1), 0)
        return LOG2E * jnp.where(row < n, sink_ref[0:1, :], jnp.where(row < 2 * n, sink_ref[1:2, :],
                                 jnp.where(row < 3 * n, sink_ref[2:3, :], sink_ref[3:4, :])))

    def unstack(o, n):
        return jnp.concatenate([jnp.where(half0, o[0:n], o[2 * n:3 * n]),
                                jnp.where(half0, o[n:2 * n], o[3 * n:4 * n])], axis=1)

    def block_scores(n):
        r0 = pl.multiple_of(n * blk, blk)
        q4 = stacked_queries(q_s, r0, blk)
        return _dot_nt(q4, jnp.concatenate([k_s[pl.ds(r0, 3 * blk), :], kc_s[...]], axis=0))

    def block_finish(n, s):
        r0 = pl.multiple_of(n * blk, blk)
        kpos = jj + (n - 1) * blk
        valid = band & (kpos >= 0) & (kpos < t)
        s_loc, s_ctx = jnp.where(valid, s[:, 0:3 * blk], NEG), s[:, 3 * blk:]
        sink = stacked_sink(blk)
        m = jnp.maximum(wide(jnp.maximum(jnp.max(s_loc, axis=1, keepdims=True),
                                         jnp.max(s_ctx, axis=1, keepdims=True)), LANES), sink)
        p_loc = jnp.exp2(s_loc - jnp.concatenate([m] * 3, axis=1))
        p_ctx = jnp.exp2(s_ctx - jnp.concatenate([m] * (cl // LANES), axis=1))
        den = wide(jnp.sum(p_loc, axis=1, keepdims=True) + jnp.sum(p_ctx, axis=1, keepdims=True), LANES) \
            + jnp.exp2(sink - m)
        o = _dot(jnp.concatenate([p_loc, p_ctx], axis=1).astype(bf16),
                 jnp.concatenate([v_s[pl.ds(r0, 3 * blk), :], vc_s[...]], axis=0))
        ol_ref[pl.ds(r0, blk), :] = unstack(o / den, blk).astype(ol_ref.dtype)

    def lat_blocks(i, _):
        ns = [i * SWA_UNROLL + u for u in range(SWA_UNROLL)]
        scores = [block_scores(n) for n in ns]
        for n, s in zip(ns, scores):
            block_finish(n, s)
        return 0

    lax.fori_loop(0, t // (blk * SWA_UNROLL), lat_blocks, 0)

    if need_ctx:
        half = cl // 2
        for r0 in (0, half):
            s = _dot_nt(stacked_queries(qc_s, r0, half), kc_s[...])
            sink = stacked_sink(half)
            m = jnp.maximum(wide(jnp.max(s, axis=1, keepdims=True), LANES), sink)
            p = jnp.exp2(s - jnp.concatenate([m] * (cl // LANES), axis=1))
            den = wide(jnp.sum(p, axis=1, keepdims=True), LANES) + jnp.exp2(sink - m)
            oc_ref[r0:r0 + half, :] = unstack(_dot(p.astype(bf16), vc_s[...]) / den, half).astype(oc_ref.dtype)
    else:
        oc_ref[...] = jnp.zeros(oc_ref.shape, oc_ref.dtype)


def _swa_head_order(a, axis):
    blocks = jnp.split(a, HEADS, axis=axis)
    return jnp.concatenate([blocks[0], blocks[2], blocks[1], blocks[3]], axis=axis)


def _swa(w_c, w_l, q_gain, k_gain, sink, cos, sin, batch, need_ctx):
    cl, t = w_c.shape[0] // batch, w_l.shape[0] // batch
    kvw = SWA_KV_HEADS * HEAD_DIM
    wc_spec, wl_spec = _seg_blocks(cl, t, 2 * GROUP_W)
    oc_spec, ol_spec = _seg_blocks(cl, t, GROUP_W)
    return pl.pallas_call(
        functools.partial(_swa_body, need_ctx=need_ctx),
        grid=(batch,),
        in_specs=[wc_spec, wl_spec, _resident((1, GROUP_W)), _resident((1, kvw)), _resident((HEADS, LANES)),
                  _resident((t, LANES)), _resident((t, LANES))],
        out_specs=[oc_spec, ol_spec],
        out_shape=[jax.ShapeDtypeStruct((batch * cl, GROUP_W), bf16),
                   jax.ShapeDtypeStruct((batch * t, GROUP_W), bf16)],
        scratch_shapes=[pltpu.VMEM((t, GROUP_W), bf16), pltpu.VMEM((cl, GROUP_W), bf16),
                        pltpu.VMEM((t + 2 * SWA_BLK, kvw), bf16), pltpu.VMEM((t + 2 * SWA_BLK, kvw), bf16),
                        pltpu.VMEM((cl, kvw), bf16), pltpu.VMEM((cl, kvw), bf16)],
        compiler_params=_params(("parallel",)),
        name="swa",
    )(w_c, w_l, jnp.tile(q_gain, HEADS).reshape(1, GROUP_W), jnp.tile(k_gain, SWA_KV_HEADS).reshape(1, kvw),
      jnp.broadcast_to(sink.reshape(HEADS, 1), (HEADS, LANES)), cos, sin)


def _rope_tables(t, rot_dim, lane0):
    pos = jnp.arange(t)
    row, col = (pos // GRID_W).astype(f32), (pos % GRID_W).astype(f32)
    nf = rot_dim // 4
    inv = ROPE_BASE ** (-jnp.arange(nf, dtype=f32) / nf)
    ar, ac = row[:, None] * inv, col[:, None] * inv
    ang = jnp.concatenate([ar, ar, ac, ac], axis=-1)
    return jnp.cos(ang), jnp.sin(ang)


def _pack_in_weight(w_in):
    o = 0
    seg = {}
    for name, n in (("qkvo", 4 * GROUP_W), ("mg", 4 * HEADS), ("a_q", GROUP_W), ("a_kv", GROUP_W // 2),
                    ("a_kr", MLA_ROPE), ("swa", 2 * GROUP_W), ("z", GROUP_W), ("xbc", SSD_XBC), ("dt", 2 * HEADS)):
        seg[name] = w_in[:, o:o + n]
        o += n
    d = w_in.shape[0]
    zeros = lambda n: jnp.zeros((d, n), w_in.dtype)
    packed = jnp.concatenate([
        seg["qkvo"],
        seg["a_q"], seg["a_kv"], zeros(MLA_NOPE), seg["a_kr"], zeros(LANES - MLA_QK),
        _swa_head_order(seg["swa"][:, :GROUP_W], 1), seg["swa"][:, GROUP_W:],
        seg["z"], seg["xbc"]], axis=1)
    gate_t = jnp.concatenate([seg["mg"], seg["dt"], zeros(GT_ROWS - 4 * HEADS - 2 * HEADS)], axis=1).T
    return packed.astype(bf16), gate_t.astype(bf16)


def kernel(x, c, ctx, c_ctx, w_mod, b_mod, ffn1_norm, ffn1_wi, ffn1_wo, mix_norm, w_in, w_out, mlstm_gate_b, mlstm_out_norm, mla_q_norm, mla_kv_norm, mla_wq_b, mla_wkv_b, mla_q_gain, mla_k_gain, swa_q_gain, swa_k_gain, swa_sink, ssd_conv_w, ssd_conv_b, ssd_dt_bias, ssd_a_log, ssd_d, ssd_norm, ffn2_norm, ffn2_wi, ffn2_wo):
    b, t, d = x.shape
    cl = ctx.shape[1]
    depth = w_mod.shape[0]
    dff = ffn1_wo.shape[1]

    cos_m, sin_m = _rope_tables(t, MLA_ROPE, MLA_NOPE)
    pad_id = lambda tab, fill: jnp.concatenate(
        [jnp.full((t, MLA_NOPE), fill, f32), tab, jnp.full((t, LANES - MLA_QK), fill, f32)], axis=1)
    cos_m, sin_m = pad_id(cos_m, 1.0), pad_id(sin_m, 0.0)
    cos_s, sin_s = _rope_tables(t, HEAD_DIM, 0)
    cos_s, sin_s = jnp.tile(cos_s, (1, LANES // HEAD_DIM)), jnp.tile(sin_s, (1, LANES // HEAD_DIM))

    h = x.reshape(b * t, d)
    hc = ctx.reshape(b * cl, d)
    cc = jnp.concatenate([c, c_ctx[None, :]], axis=0)
    for l in range(depth):
        need_ctx = l < depth - 1
        mod = _modulation(cc, w_mod, b_mod, l).reshape(b + 1, N_MOD, d)
        streams = ((0, t), (b, None))

        ffn1_w = (ffn1_norm[l].reshape(1, d), ffn1_wi[l][:, :dff].astype(bf16), ffn1_wi[l][:, dff:].astype(bf16),
                  ffn1_wo[l].astype(bf16))
        ffn2_w = (ffn2_norm[l].reshape(1, d), ffn2_wi[l][:, :dff].astype(bf16), ffn2_wi[l][:, dff:].astype(bf16),
                  ffn2_wo[l].astype(bf16))
        h = _ffn(h, mod, *ffn1_w, 0, *streams[0])
        hc = _ffn(hc, mod, *ffn1_w, 0, *streams[1])
        w_packed, w_gate_t = _pack_in_weight(w_in[l])
        in_w = (mix_norm[l].reshape(1, d), w_packed, w_gate_t)
        qkvo_l, mla_l, swa_l, z_l, xbc_l, gt_l = _inproj(h, mod, *in_w, *streams[0])
        qkvo_c, mla_c, swa_c, z_c, xbc_c, gt_c = _inproj(hc, mod, *in_w, *streams[1])
        a_c, a_l = _mlstm(qkvo_c, qkvo_l, gt_c, gt_l, mlstm_gate_b[l], mlstm_out_norm[l], b, need_ctx)
        m_c, m_l = _mla(mla_c, mla_l, mla_q_norm[l], mla_kv_norm[l], mla_wq_b[l], mla_wkv_b[l], mla_q_gain[l],
                        mla_k_gain[l], cos_m, sin_m, b, need_ctx)
        s_c, s_l = _swa(swa_c, swa_l, swa_q_gain[l], swa_k_gain[l], swa_sink[l], cos_s, sin_s, b, need_ctx)
        d_c, d_l = _ssd(z_c, z_l, xbc_c, xbc_l, gt_c, gt_l, ssd_conv_w[l], ssd_conv_b[l], ssd_dt_bias[l],
                        ssd_a_log[l], ssd_d[l], ssd_norm[l], b, need_ctx)
        wo_b = jnp.concatenate([w_out[l][:2 * GROUP_W], _swa_head_order(w_out[l][2 * GROUP_W:3 * GROUP_W], 0),
                                w_out[l][3 * GROUP_W:]], axis=0).astype(bf16)
        h = _mix_ffn(h, mod, (a_l, m_l, s_l, d_l), wo_b, *ffn2_w, *streams[0])
        if need_ctx:
            hc = _mix_ffn(hc, mod, (a_c, m_c, s_c, d_c), wo_b, *ffn2_w, *streams[1])
    return h.reshape(b, t, d)
```
